```python
import math
import jax, jax.numpy as jnp
from jax import lax
import numpy as np


D_MODEL = 2048
BATCH = 4
SEQ = 4096
DEPTH = 2
DEC_BATCH = 32
DEC_SEQ = 64
PAST_LEN = 2048

CHUNK = 64
Q_BLOCK = 128
EPS = 1e-6
NEG_INF = -1e30
ROPE_THETA = 10000.0
MLA_HEADS = 16
Q_LORA = 512
KV_LORA = 512
NOPE_DIM = 128
ROPE_DIM = 64
V_DIM = 128
MLA_QK_DIM = NOPE_DIM + ROPE_DIM
MLA_SCALE = MLA_QK_DIM ** -0.5
RET_HEADS = 8
RET_QK_DIM = 128
RET_V_DIM = 256
RET_QK_WIDTH = RET_HEADS * RET_QK_DIM
RET_V_WIDTH = RET_HEADS * RET_V_DIM
PEER_HEADS = 8
N_KEYS = 128
N_EXPERTS = N_KEYS * N_KEYS
PEER_QDIM = 256
PEER_HALF = PEER_QDIM // 2
PEER_TOPK = 16
PEER_BLOCK = 128
IN_SIZES = (Q_LORA, KV_LORA, ROPE_DIM, RET_QK_WIDTH, RET_QK_WIDTH, RET_V_WIDTH, RET_V_WIDTH, D_MODEL, D_MODEL)
IN_WIDTH = Q_LORA + KV_LORA + ROPE_DIM + 2 * RET_QK_WIDTH + 2 * RET_V_WIDTH + 2 * D_MODEL

kernel_name = 'hybrid_mla_retention_peer_stream'


def rmsnorm(x, g):
    xf = x.astype(jnp.float32)
    y = xf * lax.rsqrt(jnp.mean(xf * xf, axis=-1, keepdims=True) + EPS)
    return (y * g.astype(jnp.float32)).astype(x.dtype)


def ada_params(c, w, b):
    m = jax.nn.silu(c) @ w + b
    return m.reshape(c.shape[0], 6, D_MODEL)


def modulate(x, g, shift, scale):
    return rmsnorm(x, g) * (1.0 + scale[:, None, :]) + shift[:, None, :]


def rotary(x, pos):
    d = x.shape[-1]
    half = d // 2
    inv = ROPE_THETA ** (-jnp.arange(half, dtype=jnp.float32) * 2.0 / d)
    ang = pos.astype(jnp.float32)[:, None] * inv[None, :]
    cos = jnp.cos(ang)[None, :, None, :]
    sin = jnp.sin(ang)[None, :, None, :]
    xf = x.astype(jnp.float32)
    x1, x2 = xf[..., :half], xf[..., half:]
    return jnp.concatenate([x1 * cos - x2 * sin, x1 * sin + x2 * cos], axis=-1).astype(x.dtype)


def mla_prompt(q_nope, q_rope, ckv, krope, w_ukv):
    B, S = q_nope.shape[:2]
    kv = jnp.einsum('bsc,chd->bshd', ckv, w_ukv)
    k_nope, v = kv[..., :NOPE_DIM], kv[..., NOPE_DIM:]
    outs = []
    for q0 in range(0, S, Q_BLOCK):
        kend = q0 + Q_BLOCK
        s = (jnp.einsum('bqhd,bkhd->bhqk', q_nope[:, q0:kend], k_nope[:, :kend])
             + jnp.einsum('bqhd,bkd->bhqk', q_rope[:, q0:kend], krope[:, :kend]))
        s = s.astype(jnp.float32) * MLA_SCALE
        qpos = jnp.arange(q0, kend)
        kpos = jnp.arange(kend)
        mask = (kpos[None, :] // CHUNK) <= (qpos[:, None] // CHUNK)
        p = jax.nn.softmax(jnp.where(mask[None, None], s, NEG_INF), axis=-1).astype(v.dtype)
        outs.append(jnp.einsum('bhqk,bkhd->bqhd', p, v[:, :kend]))
    return jnp.concatenate(outs, axis=1)


def mla_sample(q_nope, q_rope, ckv_all, krope_all, w_ukv):
    w_uk = w_ukv[..., :NOPE_DIM]
    w_uv = w_ukv[..., NOPE_DIM:]
    q_lat = jnp.einsum('bqhd,chd->bqhc', q_nope, w_uk)
    s = (jnp.einsum('bqhc,bkc->bhqk', q_lat, ckv_all)
         + jnp.einsum('bqhd,bkd->bhqk', q_rope, krope_all))
    p = jax.nn.softmax(s.astype(jnp.float32) * MLA_SCALE, axis=-1).astype(ckv_all.dtype)
    o_lat = jnp.einsum('bhqk,bkc->bqhc', p, ckv_all)
    return jnp.einsum('bqhc,chd->bqhd', o_lat, w_uv)


def ret_log_gamma():
    return jnp.log1p(-(2.0 ** (-5.0 - jnp.arange(RET_HEADS, dtype=jnp.float32))))


def retention_chunk(q, k, v, state):
    L = q.shape[1]
    lg = ret_log_gamma()
    idx = jnp.arange(L, dtype=jnp.float32)
    diff = idx[:, None] - idx[None, :]
    dmask = jnp.where(diff[None] >= 0, jnp.exp(jnp.maximum(diff, 0.0)[None] * lg[:, None, None]), 0.0)
    inner = jnp.einsum('bqhd,bkhd->bhqk', q, k) * dmask[None]
    o = jnp.einsum('bhqk,bkhe->bqhe', inner, v)
    q_decay = jnp.exp((idx + 1.0)[:, None] * lg[None, :])
    o = o + jnp.einsum('bqhd,bhde->bqhe', q * q_decay[None, :, :, None], state)
    k_decay = jnp.exp((L - 1.0 - idx)[:, None] * lg[None, :])
    new_state = (state * jnp.exp(L * lg)[None, :, None, None]
                 + jnp.einsum('bkhd,bkhe->bhde', k * k_decay[None, :, :, None], v))
    return o, new_state


def retention_prompt(q, k, v):
    B, T = q.shape[:2]
    n_chunks = T // CHUNK

    def chunks(a):
        return jnp.swapaxes(a.reshape(B, n_chunks, CHUNK, *a.shape[2:]), 0, 1)

    def step(state, qkv):
        o, state = retention_chunk(*qkv, state)
        return state, o

    state0 = jnp.zeros((B, RET_HEADS, RET_QK_DIM, RET_V_DIM), jnp.float32)
    state, o = lax.scan(step, state0, (chunks(q), chunks(k), chunks(v)))
    return jnp.swapaxes(o, 0, 1).reshape(B, T, RET_HEADS, RET_V_DIM), state


def head_groupnorm(o):
    mu = jnp.mean(o, axis=-1, keepdims=True)
    var = jnp.mean(jnp.square(o - mu), axis=-1, keepdims=True)
    return (o - mu) * lax.rsqrt(var + EPS)


def token_mixer(h, pos, ckv_past, krope_past, ret_state, w_in, g_q, w_uq, g_kv, w_ukv, w_a, w_b, w_o):
    B, T, _ = h.shape
    split_at = np.cumsum(IN_SIZES)[:-1].tolist()
    cq, ckv, kr, rq, rk, rv, rg, ga, gb = jnp.split(h @ w_in, split_at, axis=-1)
    q = (rmsnorm(cq, g_q) @ w_uq).reshape(B, T, MLA_HEADS, MLA_QK_DIM)
    q_nope = q[..., :NOPE_DIM]
    q_rope = rotary(q[..., NOPE_DIM:], pos)
    ckv = rmsnorm(ckv, g_kv)
    kr = rotary(kr[:, :, None, :], pos)[:, :, 0, :]
    if ckv_past is None:
        o_a = mla_prompt(q_nope, q_rope, ckv, kr, w_ukv)
    else:
        o_a = mla_sample(q_nope, q_rope, jnp.concatenate([ckv_past, ckv], axis=1),
                         jnp.concatenate([krope_past, kr], axis=1), w_ukv)
    y_a = o_a.reshape(B, T, MLA_HEADS * V_DIM) @ w_a
    f32 = jnp.float32
    rq = rotary(rq.reshape(B, T, RET_HEADS, RET_QK_DIM), pos).astype(f32)
    rk = rotary(rk.reshape(B, T, RET_HEADS, RET_QK_DIM), pos).astype(f32) * (RET_QK_DIM ** -0.5)
    rv = rv.reshape(B, T, RET_HEADS, RET_V_DIM).astype(f32)
    if ret_state is None:
        o_b, new_state = retention_prompt(rq, rk, rv)
    else:
        o_b, new_state = retention_chunk(rq, rk, rv, ret_state.astype(f32))
    o_b = head_groupnorm(o_b).reshape(B, T, RET_V_WIDTH)
    y_b = (jax.nn.silu(rg.astype(f32)) * o_b).astype(h.dtype) @ w_b
    mixed = jax.nn.sigmoid(ga) * y_a + jax.nn.sigmoid(gb) * y_b
    return mixed @ w_o, ckv, kr, new_state.astype(h.dtype)


def peer(h, w_pq, k1, k2, u_emb, v_emb):
    B, T, D = h.shape
    x = h.reshape(B * T, D)
    n = x.shape[0]
    n_pad = (-n) % PEER_BLOCK
    blocks = jnp.pad(x, ((0, n_pad), (0, 0))).reshape(-1, PEER_BLOCK, D)

    def block_fn(xb):
        q = (xb @ w_pq).reshape(PEER_BLOCK, PEER_HEADS, 2, PEER_HALF)
        s1 = jnp.einsum('thd,kd->thk', q[:, :, 0], k1).astype(jnp.float32)
        s2 = jnp.einsum('thd,kd->thk', q[:, :, 1], k2).astype(jnp.float32)
        v1, i1 = lax.top_k(s1, PEER_TOPK)
        v2, i2 = lax.top_k(s2, PEER_TOPK)
        cand = (v1[..., :, None] + v2[..., None, :]).reshape(PEER_BLOCK, PEER_HEADS, PEER_TOPK * PEER_TOPK)
        cand_idx = (i1[..., :, None] * N_KEYS + i2[..., None, :]).reshape(PEER_BLOCK, PEER_HEADS, PEER_TOPK * PEER_TOPK)
        best, sel = lax.top_k(cand, PEER_TOPK)
        eidx = jnp.take_along_axis(cand_idx, sel, axis=-1)
        g = jax.nn.softmax(best, axis=-1).astype(xb.dtype)
        act = jax.nn.gelu(jnp.einsum('thkd,td->thk', u_emb[eidx], xb), approximate=False)
        return jnp.einsum('thk,thkd->td', g * act, v_emb[eidx])

    out = lax.map(block_fn, blocks).reshape(-1, D)[:n]
    return out.reshape(B, T, D)


def setup_inputs(seed: int = 0) -> dict:
    key = jax.random.key(seed)
    ks = jax.random.split(key, 32)
    f32 = jnp.float32

    def nrm(k, shape, scale):
        return jax.random.normal(k, shape, f32) * scale

    def gain(k, shape):
        return 1.0 + 0.02 * jax.random.normal(k, shape, f32)

    return {
        'x_prompt': nrm(ks[0], (BATCH, SEQ, D_MODEL), 1.0),
        'x_sample': nrm(ks[1], (DEC_BATCH, DEC_SEQ, D_MODEL), 1.0),
        'c_prompt': nrm(ks[2], (BATCH, D_MODEL), 1.0),
        'c_sample': nrm(ks[3], (DEC_BATCH, D_MODEL), 1.0),
        'cache_ckv': nrm(ks[4], (DEPTH, DEC_BATCH, PAST_LEN, KV_LORA), 1.0),
        'cache_krope': nrm(ks[5], (DEPTH, DEC_BATCH, PAST_LEN, ROPE_DIM), 1.0),
        'state_ret': nrm(ks[6], (DEPTH, DEC_BATCH, RET_HEADS, RET_QK_DIM, RET_V_DIM), 1.0),
        'w_ada': nrm(ks[7], (DEPTH, D_MODEL, 6 * D_MODEL), 0.5 * D_MODEL ** -0.5),
        'b_ada': nrm(ks[8], (DEPTH, 6 * D_MODEL), 0.01),
        'g_mix': gain(ks[9], (DEPTH, D_MODEL)),
        'g_ffn': gain(ks[10], (DEPTH, D_MODEL)),
        'w_in': nrm(ks[11], (DEPTH, D_MODEL, IN_WIDTH), D_MODEL ** -0.5),
        'g_q': gain(ks[12], (DEPTH, Q_LORA)),
        'w_uq': nrm(ks[13], (DEPTH, Q_LORA, MLA_HEADS * MLA_QK_DIM), Q_LORA ** -0.5),
        'g_kv': gain(ks[14], (DEPTH, KV_LORA)),
        'w_ukv': nrm(ks[15], (DEPTH, KV_LORA, MLA_HEADS, NOPE_DIM + V_DIM), KV_LORA ** -0.5),
        'w_a': nrm(ks[16], (DEPTH, MLA_HEADS * V_DIM, D_MODEL), (MLA_HEADS * V_DIM) ** -0.5),
        'w_b': nrm(ks[17], (DEPTH, RET_V_WIDTH, D_MODEL), RET_V_WIDTH ** -0.5),
        'w_o': nrm(ks[18], (DEPTH, D_MODEL, D_MODEL), D_MODEL ** -0.5),
        'w_pq': nrm(ks[19], (DEPTH, D_MODEL, PEER_HEADS * PEER_QDIM), D_MODEL ** -0.5),
        'peer_k1': nrm(ks[20], (DEPTH, N_KEYS, PEER_HALF), PEER_HALF ** -0.5),
        'peer_k2': nrm(ks[21], (DEPTH, N_KEYS, PEER_HALF), PEER_HALF ** -0.5),
        'peer_u': nrm(ks[22], (DEPTH, N_EXPERTS, D_MODEL), D_MODEL ** -0.5),
        'peer_v': nrm(ks[23], (DEPTH, N_EXPERTS, D_MODEL), 0.25),
        'g_final': gain(ks[24], (D_MODEL,)),
    }


def reference(x_prompt, x_sample, c_prompt, c_sample, cache_ckv, cache_krope, state_ret,
              w_ada, b_ada, g_mix, g_ffn, w_in, g_q, w_uq, g_kv, w_ukv, w_a, w_b, w_o,
              w_pq, peer_k1, peer_k2, peer_u, peer_v, g_final):
    pos_p = jnp.arange(x_prompt.shape[1], dtype=jnp.int32)
    pos_s = cache_ckv.shape[2] + jnp.arange(x_sample.shape[1], dtype=jnp.int32)
    xp, xs = x_prompt, x_sample
    ckv_p, kr_p, st_p, ckv_s, kr_s, st_s = [], [], [], [], [], []
    for l in range(DEPTH):
        mix_w = (w_in[l], g_q[l], w_uq[l], g_kv[l], w_ukv[l], w_a[l], w_b[l], w_o[l])
        mp = ada_params(c_prompt, w_ada[l], b_ada[l])
        ms = ada_params(c_sample, w_ada[l], b_ada[l])
        yp, a1, a2, a3 = token_mixer(modulate(xp, g_mix[l], mp[:, 0], mp[:, 1]), pos_p, None, None, None, *mix_w)
        ys, b1, b2, b3 = token_mixer(modulate(xs, g_mix[l], ms[:, 0], ms[:, 1]), pos_s,
                                     cache_ckv[l], cache_krope[l], state_ret[l], *mix_w)
        xp = xp + mp[:, 2][:, None, :] * yp
        xs = xs + ms[:, 2][:, None, :] * ys
        ckv_p.append(a1); kr_p.append(a2); st_p.append(a3)
        ckv_s.append(b1); kr_s.append(b2); st_s.append(b3)
        peer_w = (w_pq[l], peer_k1[l], peer_k2[l], peer_u[l], peer_v[l])
        xp = xp + mp[:, 5][:, None, :] * peer(modulate(xp, g_ffn[l], mp[:, 3], mp[:, 4]), *peer_w)
        xs = xs + ms[:, 5][:, None, :] * peer(modulate(xs, g_ffn[l], ms[:, 3], ms[:, 4]), *peer_w)
    y_prompt = rmsnorm(xp, g_final)
    y_sample = rmsnorm(xs, g_final)
    return (y_prompt, y_sample,
            jnp.stack(ckv_p, 0), jnp.stack(kr_p, 0), jnp.stack(st_p, 0),
            jnp.stack(ckv_s, 0), jnp.stack(kr_s, 0), jnp.stack(st_s, 0))
```

```python
import functools

import numpy as np
import jax
import jax.numpy as jnp
from jax import lax
from jax.experimental import pallas as pl
from jax.experimental.pallas import tpu as pltpu

F32 = jnp.float32
BF16 = jnp.bfloat16

D_MODEL = 2048
CHUNK = 64
EPS = 1e-6
ROPE_THETA = 10000.0
MLA_HEADS = 16
Q_LORA = 512
KV_LORA = 512
NOPE_DIM = 128
ROPE_DIM = 64
V_DIM = 128
MLA_SCALE = (NOPE_DIM + ROPE_DIM) ** -0.5
QK_PAD = 256
KV_IN = KV_LORA + 128
RET_HEADS = 8
RET_QK_DIM = 128
RET_V_DIM = 256
PEER_HEADS = 8
N_KEYS = 128
PEER_HALF = 128
PEER_TOPK = 16
GROUP = 64

VMEM_LIMIT = 56 * 1024 * 1024


def _cparams(*sem):
    return pltpu.CompilerParams(dimension_semantics=sem, vmem_limit_bytes=VMEM_LIMIT)


def _modnorm_kernel(x_ref, g_ref, sc_ref, sh_ref, o_ref):
    x = x_ref[...]
    tr, d = x.shape
    y = x * lax.rsqrt(jnp.mean(x * x, axis=-1, keepdims=True) + EPS) * g_ref[...]
    y = y.reshape(tr // GROUP, GROUP, d) * (1.0 + sc_ref[...]) + sh_ref[...]
    o_ref[...] = y.reshape(tr, d).astype(o_ref.dtype)


def modnorm(x, g, scale_g, shift_g, tr=512):
    n, d = x.shape
    tr = min(tr, n)
    gpt = tr // GROUP
    return pl.pallas_call(
        _modnorm_kernel,
        out_shape=jax.ShapeDtypeStruct((n, d), BF16),
        grid=(n // tr,),
        in_specs=[pl.BlockSpec((tr, d), lambda i: (i, 0)),
                  pl.BlockSpec((1, d), lambda i: (0, 0)),
                  pl.BlockSpec((gpt, 1, d), lambda i: (i, 0, 0)),
                  pl.BlockSpec((gpt, 1, d), lambda i: (i, 0, 0))],
        out_specs=pl.BlockSpec((tr, d), lambda i: (i, 0)),
        compiler_params=_cparams("parallel"),
        name="modnorm",
    )(x, g.reshape(1, d), scale_g, shift_g)


def _rmsnorm_kernel(x_ref, g_ref, o_ref):
    x = x_ref[...]
    o_ref[...] = x * lax.rsqrt(jnp.mean(x * x, axis=-1, keepdims=True) + EPS) * g_ref[...]


def final_rmsnorm(x, g, tr=512):
    n, d = x.shape
    tr = min(tr, n)
    return pl.pallas_call(
        _rmsnorm_kernel,
        out_shape=jax.ShapeDtypeStruct((n, d), F32),
        grid=(n // tr,),
        in_specs=[pl.BlockSpec((tr, d), lambda i: (i, 0)),
                  pl.BlockSpec((1, d), lambda i: (0, 0))],
        out_specs=pl.BlockSpec((tr, d), lambda i: (i, 0)),
        compiler_params=_cparams("parallel"),
        name="final_rmsnorm",
    )(x, g.reshape(1, d))


def _mm_kernel(x_ref, w_ref, *rest, epilogue):
    extras, o_ref = rest[:-1], rest[-1]
    acc = jnp.dot(x_ref[...].astype(BF16), w_ref[...], preferred_element_type=F32)
    if epilogue is not None:
        acc = epilogue(acc, *[e[...] for e in extras])
    o_ref[...] = acc.astype(o_ref.dtype)


def matmul(x, w, out_dtype, tm=1024, tn=512, epilogue=None, extras=(), extra_specs=(), name="matmul"):
    m, k = x.shape
    n = w.shape[1]
    tm, tn = min(tm, m), min(tn, n)
    assert m % tm == 0 and n % tn == 0, (m, tm, n, tn)
    return pl.pallas_call(
        functools.partial(_mm_kernel, epilogue=epilogue),
        out_shape=jax.ShapeDtypeStruct((m, n), out_dtype),
        grid=(m // tm, n // tn),
        in_specs=[pl.BlockSpec((tm, k), lambda i, j: (i, 0)),
                  pl.BlockSpec((k, tn), lambda i, j: (0, j))] + list(extra_specs),
        out_specs=pl.BlockSpec((tm, tn), lambda i, j: (i, j)),
        compiler_params=_cparams("parallel", "arbitrary"),
        name=name,
    )(x, w, *extras)


def _rope_half(x, cos, sa, sb):
    return x * cos + pltpu.roll(x, 96, 1) * sa + pltpu.roll(x, 32, 1) * sb


def _ep_qrope(acc, cos, sa, sb):
    parts = []
    for h in range(acc.shape[1] // QK_PAD):
        parts.append(acc[:, h * QK_PAD:h * QK_PAD + NOPE_DIM])
        parts.append(_rope_half(acc[:, h * QK_PAD + NOPE_DIM:(h + 1) * QK_PAD], cos, sa, sb))
    return jnp.concatenate(parts, axis=1)


def _ep_gate_a(acc, ga):
    return jax.nn.sigmoid(ga.astype(F32)) * acc


def _ep_gate_b(acc, gb, m1):
    return jax.nn.sigmoid(gb.astype(F32)) * acc + m1


def _ep_resid(acc, x, gate):
    tm, tn = acc.shape
    y = acc.reshape(tm // GROUP, GROUP, tn) * gate
    return x + y.reshape(tm, tn)


def _inproj1_kernel(h_ref, w_ref, gq_ref, gkv_ref, cos_ref, sa_ref, sb_ref,
                    cq_ref, ckv_ref, kr_ref, kvin_ref):
    acc = jnp.dot(h_ref[...], w_ref[...], preferred_element_type=F32)
    cq = acc[:, :Q_LORA]
    ckv = acc[:, Q_LORA:Q_LORA + KV_LORA]
    kr = acc[:, Q_LORA + KV_LORA:]
    cqn = cq * lax.rsqrt(jnp.mean(cq * cq, axis=-1, keepdims=True) + EPS) * gq_ref[...]
    ckvn = ckv * lax.rsqrt(jnp.mean(ckv * ckv, axis=-1, keepdims=True) + EPS) * gkv_ref[...]
    krr = _rope_half(kr, cos_ref[...], sa_ref[...], sb_ref[...])
    cq_ref[...] = cqn.astype(BF16)
    ckv_ref[...] = ckvn
    kr_ref[...] = krr
    kvin_ref[:, :KV_LORA] = ckvn.astype(BF16)
    kvin_ref[:, KV_LORA:] = krr.astype(BF16)


def inproj1(h, w1, g_q, g_kv, cos, sa, sb, tm=512):
    n, k = h.shape
    tm = min(tm, n)
    w1n = w1.shape[1]
    row = lambda i: (i, 0)
    fix = lambda i: (0, 0)
    return pl.pallas_call(
        _inproj1_kernel,
        out_shape=(jax.ShapeDtypeStruct((n, Q_LORA), BF16),
                   jax.ShapeDtypeStruct((n, KV_LORA), F32),
                   jax.ShapeDtypeStruct((n, 128), F32),
                   jax.ShapeDtypeStruct((n, KV_IN), BF16)),
        grid=(n // tm,),
        in_specs=[pl.BlockSpec((tm, k), row), pl.BlockSpec((k, w1n), fix),
                  pl.BlockSpec((1, Q_LORA), fix), pl.BlockSpec((1, KV_LORA), fix),
                  pl.BlockSpec((tm, 128), row), pl.BlockSpec((tm, 128), row), pl.BlockSpec((tm, 128), row)],
        out_specs=(pl.BlockSpec((tm, Q_LORA), row), pl.BlockSpec((tm, KV_LORA), row),
                   pl.BlockSpec((tm, 128), row), pl.BlockSpec((tm, KV_IN), row)),
        compiler_params=_cparams("parallel"),
        name="inproj1",
    )(h, w1, g_q.reshape(1, -1), g_kv.reshape(1, -1), cos, sa, sb)


def _attn_prompt_kernel(q_ref, k_ref, v_ref, o_ref, *, tq):
    qi = pl.program_id(2)
    q = q_ref[...]
    nt = (((1,), (1,)), ((), ()))

    def scores(j):
        start = pl.multiple_of(j * tq, tq)
        k = k_ref[pl.ds(start, tq), :]
        v = v_ref[pl.ds(start, tq), :]
        return lax.dot_general(q, k, nt, preferred_element_type=F32), v

    def update(carry, s, v):
        m, l, acc = carry
        m_new = jnp.maximum(m, jnp.max(s, axis=-1, keepdims=True))
        alpha = jnp.exp(m - m_new)
        p = jnp.exp(s - m_new)
        l = alpha * l + jnp.sum(p, axis=-1, keepdims=True)
        acc = alpha * acc + jnp.dot(p.astype(BF16), v, preferred_element_type=F32)
        return m_new, l, acc

    def body(j, carry):
        s, v = scores(j)
        return update(carry, s, v)

    init = (jnp.full((tq, 1), -1e30, F32), jnp.zeros((tq, 1), F32), jnp.zeros((tq, V_DIM), F32))
    carry = lax.fori_loop(0, qi, body, init)
    s, v = scores(qi)
    rows = lax.broadcasted_iota(jnp.int32, (tq, tq), 0) // CHUNK
    cols = lax.broadcasted_iota(jnp.int32, (tq, tq), 1) // CHUNK
    s = jnp.where(cols <= rows, s, -1e30)
    m, l, acc = update(carry, s, v)
    o_ref[...] = (acc / l).astype(o_ref.dtype)


def attn_prompt(q_cat, kv, bp, s_len, tq=512):
    tq = min(tq, s_len)
    nq = s_len // tq
    vcol0 = MLA_HEADS * QK_PAD // V_DIM
    return pl.pallas_call(
        functools.partial(_attn_prompt_kernel, tq=tq),
        out_shape=jax.ShapeDtypeStruct((bp * s_len, MLA_HEADS * V_DIM), BF16),
        grid=(bp, MLA_HEADS, nq),
        in_specs=[pl.BlockSpec((tq, QK_PAD), lambda b, h, i: (b * nq + i, h)),
                  pl.BlockSpec((s_len, QK_PAD), lambda b, h, i: (b, h)),
                  pl.BlockSpec((s_len, V_DIM), lambda b, h, i: (b, vcol0 + h))],
        out_specs=pl.BlockSpec((tq, V_DIM), lambda b, h, i: (b * nq + i, h)),
        compiler_params=_cparams("parallel", "parallel", "arbitrary"),
        name="attn_prompt",
    )(q_cat, kv, kv)


def _attn_sample_kernel(q_ref, ckv_ref, kr_ref, new_ref, wuk_ref, wuv_ref, o_ref,
                        qall_ref, kall_ref, olat_ref, *, t):
    nt = (((1,), (1,)), ((), ()))
    past = ckv_ref.shape[1]
    for h in range(MLA_HEADS):
        qn = q_ref[:, h * QK_PAD:h * QK_PAD + NOPE_DIM]
        qlat = jnp.dot(qn, wuk_ref[h], preferred_element_type=F32)
        qall_ref[h * t:(h + 1) * t, :KV_LORA] = qlat.astype(BF16)
        qall_ref[h * t:(h + 1) * t, KV_LORA:] = q_ref[:, h * QK_PAD + NOPE_DIM:(h + 1) * QK_PAD]
    kall_ref[:, :KV_LORA] = ckv_ref[0].astype(BF16)
    kall_ref[:, KV_LORA:KV_LORA + ROPE_DIM] = kr_ref[0].astype(BF16)
    kall_ref[:, KV_LORA + ROPE_DIM:] = jnp.zeros((past, KV_IN - KV_LORA - ROPE_DIM), BF16)
    qall = qall_ref[...]
    knew = new_ref[...]
    s_past = lax.dot_general(qall, kall_ref[...], nt, preferred_element_type=F32)
    s_new = lax.dot_general(qall, knew, nt, preferred_element_type=F32)
    m = jnp.maximum(jnp.max(s_past, axis=-1, keepdims=True), jnp.max(s_new, axis=-1, keepdims=True))
    p_past = jnp.exp(s_past - m)
    p_new = jnp.exp(s_new - m)
    l = jnp.sum(p_past, axis=-1, keepdims=True) + jnp.sum(p_new, axis=-1, keepdims=True)
    olat = (jnp.dot(p_past.astype(BF16), kall_ref[:, :KV_LORA], preferred_element_type=F32)
            + jnp.dot(p_new.astype(BF16), knew[:, :KV_LORA], preferred_element_type=F32))
    olat_ref[...] = (olat / l).astype(BF16)
    for h in range(MLA_HEADS):
        o = jnp.dot(olat_ref[h * t:(h + 1) * t, :], wuv_ref[h], preferred_element_type=F32)
        o_ref[:, h * V_DIM:(h + 1) * V_DIM] = o.astype(o_ref.dtype)


def attn_sample(q_cat, ckv_past, kr_past, kvin, wuk_t, wuv, row0, t):
    bs, past, _ = ckv_past.shape
    rb = row0 // t
    return pl.pallas_call(
        functools.partial(_attn_sample_kernel, t=t),
        out_shape=jax.ShapeDtypeStruct((bs * t, MLA_HEADS * V_DIM), BF16),
        grid=(bs,),
        in_specs=[pl.BlockSpec((t, MLA_HEADS * QK_PAD), lambda b: (rb + b, 0)),
                  pl.BlockSpec((1, past, KV_LORA), lambda b: (b, 0, 0)),
                  pl.BlockSpec((1, past, ROPE_DIM), lambda b: (b, 0, 0)),
                  pl.BlockSpec((t, KV_IN), lambda b: (rb + b, 0)),
                  pl.BlockSpec((MLA_HEADS, NOPE_DIM, KV_LORA), lambda b: (0, 0, 0)),
                  pl.BlockSpec((MLA_HEADS, KV_LORA, V_DIM), lambda b: (0, 0, 0))],
        out_specs=pl.BlockSpec((t, MLA_HEADS * V_DIM), lambda b: (b, 0)),
        scratch_shapes=[pltpu.VMEM((MLA_HEADS * t, KV_IN), BF16),
                        pltpu.VMEM((past, KV_IN), BF16),
                        pltpu.VMEM((MLA_HEADS * t, KV_LORA), BF16)],
        compiler_params=_cparams("parallel"),
        name="attn_sample",
    )(q_cat, ckv_past, kr_past, kvin, wuk_t, wuv)


def _ret_consts():
    lg = np.log1p(-(2.0 ** (-5.0 - np.arange(RET_HEADS, dtype=np.float64))))
    idx = np.arange(CHUNK, dtype=np.float64)
    diff = idx[:, None] - idx[None, :]
    dmask = np.where(diff[None] >= 0, np.exp(np.maximum(diff, 0.0)[None] * lg[:, None, None]), 0.0)
    qd = np.exp((idx + 1.0)[None, :] * lg[:, None])
    kd = np.exp((CHUNK - 1.0 - idx)[None, :] * lg[:, None])
    g = np.exp(CHUNK * lg)
    qd = np.broadcast_to(qd[:, :, None], (RET_HEADS, CHUNK, RET_QK_DIM))
    kd = np.broadcast_to(kd[:, :, None], (RET_HEADS, CHUNK, RET_QK_DIM))
    g = np.broadcast_to(g[:, None, None], (RET_HEADS, 1, RET_V_DIM))
    return (jnp.asarray(dmask, F32), jnp.asarray(qd, F32), jnp.asarray(kd, F32), jnp.asarray(g, F32))


def _retention_kernel(q_ref, k_ref, v_ref, g_ref, cos_ref, sin_ref, st0_ref, dm_ref, qd_ref, kd_ref,
                      gam_ref, o_ref, st_ref, state, *, cps):
    step = pl.program_id(2)
    nt = (((1,), (1,)), ((), ()))

    @pl.when(step == 0)
    def _():
        state[...] = st0_ref[0, 0]

    dm, qd, kd, gam = dm_ref[0], qd_ref[0], kd_ref[0], gam_ref[0]
    for c in range(cps):
        r = slice(c * CHUNK, (c + 1) * CHUNK)
        cos, sin = cos_ref[r, :], sin_ref[r, :]
        qf = q_ref[r, :].astype(F32)
        kf = k_ref[r, :].astype(F32)
        q = qf * cos + pltpu.roll(qf, 64, 1) * sin
        k = (kf * cos + pltpu.roll(kf, 64, 1) * sin) * (RET_QK_DIM ** -0.5)
        v = v_ref[r, :]
        st = state[...]
        inner = lax.dot_general(q.astype(BF16), k.astype(BF16), nt, preferred_element_type=F32) * dm
        o = (jnp.dot(inner.astype(BF16), v, preferred_element_type=F32)
             + jnp.dot((q * qd).astype(BF16), st.astype(BF16), preferred_element_type=F32))
        kt = jnp.transpose(k * kd).astype(BF16)
        state[...] = st * gam + jnp.dot(kt, v, preferred_element_type=F32)
        mu = jnp.mean(o, axis=-1, keepdims=True)
        oc = o - mu
        var = jnp.mean(oc * oc, axis=-1, keepdims=True)
        gate = g_ref[r, :].astype(F32)
        o_ref[r, :] = (gate * jax.nn.sigmoid(gate) * (oc * lax.rsqrt(var + EPS))).astype(o_ref.dtype)

    @pl.when(step == pl.num_programs(2) - 1)
    def _():
        st_ref[0, 0] = state[...]


def retention(big, cos, sin, state0, consts, row0, nseq, seq_len, cps):
    r = cps * CHUNK
    steps = seq_len // r
    rb0 = row0 // r
    dmask, qd, kd, gam = consts
    rowblk = lambda s, h, t: rb0 + s * steps + t
    hconst = lambda s, h, t: (h, 0, 0)
    nrows = nseq * seq_len
    return pl.pallas_call(
        functools.partial(_retention_kernel, cps=cps),
        out_shape=(jax.ShapeDtypeStruct((nrows, RET_HEADS * RET_V_DIM), BF16),
                   jax.ShapeDtypeStruct((nseq, RET_HEADS, RET_QK_DIM, RET_V_DIM), F32)),
        grid=(nseq, RET_HEADS, steps),
        in_specs=[pl.BlockSpec((r, RET_QK_DIM), lambda s, h, t: (rowblk(s, h, t), h)),
                  pl.BlockSpec((r, RET_QK_DIM), lambda s, h, t: (rowblk(s, h, t), RET_HEADS + h)),
                  pl.BlockSpec((r, RET_V_DIM), lambda s, h, t: (rowblk(s, h, t), RET_HEADS + h)),
                  pl.BlockSpec((r, RET_V_DIM), lambda s, h, t: (rowblk(s, h, t), 2 * RET_HEADS + h)),
                  pl.BlockSpec((r, RET_QK_DIM), lambda s, h, t: (rowblk(s, h, t), 0)),
                  pl.BlockSpec((r, RET_QK_DIM), lambda s, h, t: (rowblk(s, h, t), 0)),
                  pl.BlockSpec((1, 1, RET_QK_DIM, RET_V_DIM), lambda s, h, t: (s, h, 0, 0)),
                  pl.BlockSpec((1, CHUNK, CHUNK), hconst),
                  pl.BlockSpec((1, CHUNK, RET_QK_DIM), hconst),
                  pl.BlockSpec((1, CHUNK, RET_QK_DIM), hconst),
                  pl.BlockSpec((1, 1, RET_V_DIM), hconst)],
        out_specs=(pl.BlockSpec((r, RET_V_DIM), lambda s, h, t: (s * steps + t, h)),
                   pl.BlockSpec((1, 1, RET_QK_DIM, RET_V_DIM), lambda s, h, t: (s, h, 0, 0))),
        scratch_shapes=[pltpu.VMEM((RET_QK_DIM, RET_V_DIM), F32)],
        compiler_params=_cparams("parallel", "parallel", "arbitrary"),
        name="retention",
    )(big, big, big, big, cos, sin, state0, dmask, qd, kd, gam)


_CAND = [(a, b) for a in range(PEER_TOPK) for b in range(PEER_TOPK) if (a + 1) * (b + 1) <= PEER_TOPK]


def _top_distinct(s):
    vals, cnts = [], []
    for _ in range(PEER_TOPK):
        m = jnp.max(s, axis=0, keepdims=True)
        hit = s == m
        vals.append(m)
        cnts.append(jnp.sum(jnp.where(hit, 1.0, 0.0), axis=0, keepdims=True))
        s = jnp.where(hit, -jnp.inf, s)
    return vals, cnts


def _route_kernel(qt_ref, k1_ref, k2_ref, s1_ref, e1_ref, s2_ref, e2_ref, tau_ref, cand_ref, mult_ref):
    tt = qt_ref.shape[1]
    pad = cand_ref.shape[0] - len(_CAND)
    cand_ref[len(_CAND):, :] = jnp.full((pad, tt), -jnp.inf, F32)
    mult_ref[len(_CAND):, :] = jnp.zeros((pad, tt), F32)
    for h in range(PEER_HEADS):
        q1 = qt_ref[h * 2 * PEER_HALF:h * 2 * PEER_HALF + PEER_HALF, :].astype(BF16)
        q2 = qt_ref[h * 2 * PEER_HALF + PEER_HALF:(h + 1) * 2 * PEER_HALF, :].astype(BF16)
        s1 = jnp.dot(k1_ref[...], q1, preferred_element_type=F32)
        s2 = jnp.dot(k2_ref[...], q2, preferred_element_type=F32)
        v1, c1 = _top_distinct(s1)
        v2, c2 = _top_distinct(s2)
        for r, (a, b) in enumerate(_CAND):
            cand_ref[r:r + 1, :] = v1[a] + v2[b]
            mult_ref[r:r + 1, :] = c1[a] * c2[b]
        cand = cand_ref[...]
        mult = mult_ref[...]
        rest = cand
        cum = jnp.zeros_like(v1[0])
        tau = v1[0] + v2[0]
        for _ in range(PEER_TOPK):
            m = jnp.max(rest, axis=0, keepdims=True)
            hit = rest == m
            tau = jnp.where(cum < PEER_TOPK, m, tau)
            cum = cum + jnp.sum(jnp.where(hit, mult, 0.0), axis=0, keepdims=True)
            rest = jnp.where(hit, -jnp.inf, rest)
        top = v1[0] + v2[0]
        z = jnp.sum(jnp.where(cand >= tau, mult * jnp.exp(cand - top), 0.0), axis=0, keepdims=True)
        s1_ref[h] = s1
        s2_ref[h] = s2
        e1_ref[h] = jnp.exp(s1 - v1[0])
        e2_ref[h] = jnp.exp(s2 - v2[0]) / z
        tau_ref[h] = tau


def peer_route(qt, k1, k2, tt=256):
    n = qt.shape[1]
    tt = min(tt, n)
    big = jax.ShapeDtypeStruct((PEER_HEADS, N_KEYS, n), F32)
    bspec = pl.BlockSpec((PEER_HEADS, N_KEYS, tt), lambda i: (0, 0, i))
    return pl.pallas_call(
        _route_kernel,
        out_shape=(big, big, big, big, jax.ShapeDtypeStruct((PEER_HEADS, 1, n), F32)),
        grid=(n // tt,),
        in_specs=[pl.BlockSpec((PEER_HEADS * 2 * PEER_HALF, tt), lambda i: (0, i)),
                  pl.BlockSpec((N_KEYS, PEER_HALF), lambda i: (0, 0)),
                  pl.BlockSpec((N_KEYS, PEER_HALF), lambda i: (0, 0))],
        out_specs=(bspec, bspec, bspec, bspec, pl.BlockSpec((PEER_HEADS, 1, tt), lambda i: (0, 0, i))),
        scratch_shapes=[pltpu.VMEM((-(-len(_CAND) // 8) * 8, tt), F32)] * 2,
        compiler_params=_cparams("parallel"),
        name="peer_route",
    )(qt, k1, k2)


def _peer_expert_kernel(ht_ref, u_ref, vt_ref, s1_ref, e1_ref, s2_ref, e2_ref, tau_ref, o_ref,
                        acc_ref, wg_ref, *, ipt):
    e = pl.program_id(1)

    @pl.when(e == 0)
    def _():
        acc_ref[...] = jnp.zeros_like(acc_ref)

    act = jnp.dot(u_ref[...], ht_ref[...], preferred_element_type=F32)
    for ii in range(ipt):
        i = e * ipt + ii
        w = None
        for h in range(PEER_HEADS):
            ssum = s2_ref[h] + s1_ref[h, pl.ds(i, 1), :]
            c = jnp.where(ssum >= tau_ref[h], e2_ref[h], 0.0) * e1_ref[h, pl.ds(i, 1), :]
            w = c if w is None else w + c
        a = act[ii * N_KEYS:(ii + 1) * N_KEYS, :]
        gelu = 0.5 * a * (1.0 + lax.erf(a * (2.0 ** -0.5)))
        wg_ref[ii * N_KEYS:(ii + 1) * N_KEYS, :] = (w * gelu).astype(BF16)
    acc_ref[...] += jnp.dot(vt_ref[...], wg_ref[...], preferred_element_type=F32)

    @pl.when(e == pl.num_programs(1) - 1)
    def _():
        o_ref[...] = acc_ref[...]


def peer_experts(ht, u, vt, s1, e1, s2, e2, tau, tt=512, ipt=4):
    d, n = ht.shape
    ne = u.shape[0]
    tt = min(tt, n)
    te = ipt * N_KEYS
    rspec = pl.BlockSpec((PEER_HEADS, N_KEYS, tt), lambda t, e: (0, 0, t))
    return pl.pallas_call(
        functools.partial(_peer_expert_kernel, ipt=ipt),
        out_shape=jax.ShapeDtypeStruct((d, n), F32),
        grid=(n // tt, ne // te),
        in_specs=[pl.BlockSpec((d, tt), lambda t, e: (0, t)),
                  pl.BlockSpec((te, d), lambda t, e: (e, 0)),
                  pl.BlockSpec((d, te), lambda t, e: (0, e)),
                  rspec, rspec, rspec, rspec,
                  pl.BlockSpec((PEER_HEADS, 1, tt), lambda t, e: (0, 0, t))],
        out_specs=pl.BlockSpec((d, tt), lambda t, e: (0, t)),
        scratch_shapes=[pltpu.VMEM((d, tt), F32), pltpu.VMEM((te, tt), BF16)],
        compiler_params=_cparams("parallel", "arbitrary"),
        name="peer_experts",
    )(ht, u, vt, s1, e1, s2, e2, tau)


def _rope_tables(pos):
    pos = pos.astype(F32)[:, None]
    half = ROPE_DIM // 2
    inv = ROPE_THETA ** (-jnp.arange(half, dtype=F32) * 2.0 / ROPE_DIM)
    ang = pos * inv[None, :]
    c, s = jnp.cos(ang), jnp.sin(ang)
    z32 = jnp.zeros_like(c)
    z64 = jnp.concatenate([z32, z32], axis=1)
    cos64 = jnp.concatenate([c, c, z64], axis=1)
    sa64 = jnp.concatenate([-s, z32, z64], axis=1)
    sb64 = jnp.concatenate([z32, s, z64], axis=1)
    half = RET_QK_DIM // 2
    inv = ROPE_THETA ** (-jnp.arange(half, dtype=F32) * 2.0 / RET_QK_DIM)
    ang = pos * inv[None, :]
    c, s = jnp.cos(ang), jnp.sin(ang)
    return cos64, sa64, sb64, jnp.concatenate([c, c], axis=1), jnp.concatenate([-s, s], axis=1)


def _prep_layer(w_in, w_uq, w_ukv, w_a, w_b, w_o, w_pq, k1, k2, u, v):
    o1 = Q_LORA + KV_LORA + ROPE_DIM
    w1 = jnp.pad(w_in[:, :o1], ((0, 0), (0, 128 - ROPE_DIM))).astype(BF16)
    w2 = w_in[:, o1:].astype(BF16)
    uq = w_uq.reshape(Q_LORA, MLA_HEADS, NOPE_DIM + ROPE_DIM) * MLA_SCALE
    uq = jnp.pad(uq, ((0, 0), (0, 0), (0, QK_PAD - NOPE_DIM - ROPE_DIM)))
    uq = uq.reshape(Q_LORA, MLA_HEADS * QK_PAD).astype(BF16)
    wk = jnp.pad(w_ukv[..., :NOPE_DIM], ((0, KV_IN - KV_LORA), (0, 0), (0, QK_PAD - NOPE_DIM)))
    eye = jnp.zeros((KV_IN, QK_PAD), F32).at[KV_LORA + jnp.arange(ROPE_DIM), NOPE_DIM + jnp.arange(ROPE_DIM)].set(1.0)
    wk = (wk + eye[:, None, :]).reshape(KV_IN, MLA_HEADS * QK_PAD)
    wv = jnp.pad(w_ukv[..., NOPE_DIM:], ((0, KV_IN - KV_LORA), (0, 0), (0, 0))).reshape(KV_IN, MLA_HEADS * V_DIM)
    wkv = jnp.concatenate([wk, wv], axis=1).astype(BF16)
    wuk_t = jnp.transpose(w_ukv[..., :NOPE_DIM], (1, 2, 0)).astype(BF16)
    wuv = jnp.transpose(w_ukv[..., NOPE_DIM:], (1, 0, 2)).astype(BF16)
    return dict(w1=w1, w2=w2, uq=uq, wkv=wkv, wuk_t=wuk_t, wuv=wuv,
                w_a=w_a.astype(BF16), w_b=w_b.astype(BF16), w_o=w_o.astype(BF16),
                w_pq_t=w_pq.T.astype(BF16), k1=k1.astype(BF16), k2=k2.astype(BF16),
                u=u.astype(BF16), v_t=v.T.astype(BF16))


def kernel(x_prompt, x_sample, c_prompt, c_sample, cache_ckv, cache_krope, state_ret, w_ada, b_ada, g_mix, g_ffn, w_in, g_q, w_uq, g_kv, w_ukv, w_a, w_b, w_o, w_pq, peer_k1, peer_k2, peer_u, peer_v, g_final):
    bp, s_len, d = x_prompt.shape
    bs, t_s, _ = x_sample.shape
    depth = w_in.shape[0]
    past = cache_ckv.shape[2]
    n_p, n_s = bp * s_len, bs * t_s
    n = n_p + n_s
    assert t_s == GROUP and s_len % GROUP == 0

    x = jnp.concatenate([x_prompt.reshape(n_p, d), x_sample.reshape(n_s, d)], axis=0)
    c_all = jnp.concatenate([c_prompt, c_sample], axis=0)
    gidx = jnp.concatenate([jnp.repeat(jnp.arange(bp), s_len // GROUP), bp + jnp.arange(bs)])
    pos = jnp.concatenate([jnp.tile(jnp.arange(s_len), bp), jnp.tile(past + jnp.arange(t_s), bs)])
    cos64, sa64, sb64, cos128, sin128 = _rope_tables(pos)
    ret_c = _ret_consts()
    gspec = lambda tm, tn: pl.BlockSpec((tm // GROUP, 1, tn), lambda i, j: (i, 0, j))
    tile = lambda tm, tn, off=0: pl.BlockSpec((tm, tn), lambda i, j: (i, j + off))
    rope_spec = lambda tm: pl.BlockSpec((tm, 128), lambda i, j: (i, 0))

    outs = {k: [] for k in ("ckv_p", "kr_p", "st_p", "ckv_s", "kr_s", "st_s")}
    for l in range(depth):
        w = _prep_layer(w_in[l], w_uq[l], w_ukv[l], w_a[l], w_b[l], w_o[l], w_pq[l],
                        peer_k1[l], peer_k2[l], peer_u[l], peer_v[l])
        ada = matmul(jax.nn.silu(c_all), w_ada[l].astype(BF16), F32, name="ada") + b_ada[l][None, :]
        cond = ada.reshape(bp + bs, 6, d)[gidx]
        cg = [cond[:, k, :][:, None, :] for k in range(6)]

        h = modnorm(x, g_mix[l], cg[1], cg[0])
        cqn, ckvn, krr, kvin = inproj1(h, w["w1"], g_q[l], g_kv[l], cos64, sa64, sb64)
        big = matmul(h, w["w2"], BF16, tm=1024, tn=1024, name="inproj2")
        tmq = 512
        q_cat = matmul(cqn, w["uq"], BF16, tm=tmq, tn=1024, epilogue=_ep_qrope,
                       extras=(cos64, sa64, sb64), extra_specs=[rope_spec(tmq)] * 3, name="q_up")
        kv = matmul(kvin[:n_p], w["wkv"], BF16, tm=1024, tn=1024, name="kv_up")
        oa_p = attn_prompt(q_cat, kv, bp, s_len)
        oa_s = attn_sample(q_cat, cache_ckv[l], cache_krope[l], kvin, w["wuk_t"], w["wuv"], n_p, t_s)
        o_a = jnp.concatenate([oa_p, oa_s], axis=0)
        zero_state = jnp.zeros((bp, RET_HEADS, RET_QK_DIM, RET_V_DIM), F32)
        ob_p, st_p = retention(big, cos128, sin128, zero_state, ret_c, 0, bp, s_len, min(8, s_len // CHUNK))
        ob_s, st_s = retention(big, cos128, sin128, state_ret[l], ret_c, n_p, bs, t_s, 1)
        o_b = jnp.concatenate([ob_p, ob_s], axis=0)
        tm, tn = 1024, 512
        ga_off = (2 * RET_HEADS * RET_QK_DIM + 2 * RET_HEADS * RET_V_DIM) // tn
        m1 = matmul(o_a, w["w_a"], F32, tm=tm, tn=tn, epilogue=_ep_gate_a,
                    extras=(big,), extra_specs=[tile(tm, tn, ga_off)], name="proj_a")
        mixed = matmul(o_b, w["w_b"], BF16, tm=tm, tn=tn, epilogue=_ep_gate_b,
                       extras=(big, m1), extra_specs=[tile(tm, tn, ga_off + d // tn), tile(tm, tn)], name="proj_b")
        x = matmul(mixed, w["w_o"], F32, tm=tm, tn=tn, epilogue=_ep_resid,
                   extras=(x, cg[2]), extra_specs=[tile(tm, tn), gspec(tm, tn)], name="proj_o")

        outs["ckv_p"].append(ckvn[:n_p].reshape(bp, s_len, KV_LORA))
        outs["kr_p"].append(krr[:n_p, :ROPE_DIM].reshape(bp, s_len, ROPE_DIM))
        outs["st_p"].append(st_p)
        outs["ckv_s"].append(ckvn[n_p:].reshape(bs, t_s, KV_LORA))
        outs["kr_s"].append(krr[n_p:, :ROPE_DIM].reshape(bs, t_s, ROPE_DIM))
        outs["st_s"].append(st_s)

        h2t = modnorm(x, g_ffn[l], cg[4], cg[3]).T
        qt = matmul(w["w_pq_t"], h2t, F32, tm=2048, tn=512, name="peer_q")
        s1, e1, s2, e2, tau = peer_route(qt, w["k1"], w["k2"])
        pt = peer_experts(h2t, w["u"], w["v_t"], s1, e1, s2, e2, tau)
        x = x + (pt.T.reshape(n // GROUP, GROUP, d) * cg[5]).reshape(n, d)

    y = final_rmsnorm(x, g_final)
    st = lambda k: jnp.stack(outs[k], axis=0)
    return (y[:n_p].reshape(bp, s_len, d), y[n_p:].reshape(bs, t_s, d),
            st("ckv_p"), st("kr_p"), st("st_p"), st("ckv_s"), st("kr_s"), st("st_s"))
```

```python
import functools

import numpy as np
import jax
import jax.numpy as jnp
from jax import lax
from jax.experimental import pallas as pl
from jax.experimental.pallas import tpu as pltpu

F32 = jnp.float32
BF16 = jnp.bfloat16

D_MODEL = 2048
CHUNK = 64
EPS = 1e-6
ROPE_THETA = 10000.0
MLA_HEADS = 16
Q_LORA = 512
KV_LORA = 512
NOPE_DIM = 128
ROPE_DIM = 64
V_DIM = 128
MLA_SCALE = (NOPE_DIM + ROPE_DIM) ** -0.5
QK_PAD = 256
KV_IN = KV_LORA + 128
RET_HEADS = 8
RET_QK_DIM = 128
RET_V_DIM = 256
PEER_HEADS = 8
N_KEYS = 128
PEER_HALF = 128
PEER_TOPK = 16
GROUP = 64
LANES = 128
SUBLANES = 8

VMEM_LIMIT = 60 * 1024 * 1024


def _cparams(*sem):
    return pltpu.CompilerParams(dimension_semantics=sem, vmem_limit_bytes=VMEM_LIMIT)


def _modnorm_kernel(x_ref, g_ref, sc_ref, sh_ref, o_ref):
    x = x_ref[...]
    tr, d = x.shape
    y = x * lax.rsqrt(jnp.mean(x * x, axis=-1, keepdims=True) + EPS) * g_ref[...]
    y = y.reshape(tr // GROUP, GROUP, d) * (1.0 + sc_ref[...]) + sh_ref[...]
    o_ref[...] = y.reshape(tr, d).astype(o_ref.dtype)


def modnorm(x, g, scale_g, shift_g, tr=512):
    n, d = x.shape
    tr = min(tr, n)
    gpt = tr // GROUP
    return pl.pallas_call(
        _modnorm_kernel,
        out_shape=jax.ShapeDtypeStruct((n, d), BF16),
        grid=(n // tr,),
        in_specs=[pl.BlockSpec((tr, d), lambda i: (i, 0)),
                  pl.BlockSpec((1, d), lambda i: (0, 0)),
                  pl.BlockSpec((gpt, 1, d), lambda i: (i, 0, 0)),
                  pl.BlockSpec((gpt, 1, d), lambda i: (i, 0, 0))],
        out_specs=pl.BlockSpec((tr, d), lambda i: (i, 0)),
        compiler_params=_cparams("parallel"),
        name="modnorm",
    )(x, g.reshape(1, d), scale_g, shift_g)


def _rmsnorm_kernel(x_ref, g_ref, o_ref):
    x = x_ref[...]
    o_ref[...] = x * lax.rsqrt(jnp.mean(x * x, axis=-1, keepdims=True) + EPS) * g_ref[...]


def final_rmsnorm(x, g, row0, nrows, tr=512):
    d = x.shape[1]
    tr = min(tr, nrows)
    rb = row0 // tr
    return pl.pallas_call(
        _rmsnorm_kernel,
        out_shape=jax.ShapeDtypeStruct((nrows, d), F32),
        grid=(nrows // tr,),
        in_specs=[pl.BlockSpec((tr, d), lambda i: (rb + i, 0)),
                  pl.BlockSpec((1, d), lambda i: (0, 0))],
        out_specs=pl.BlockSpec((tr, d), lambda i: (i, 0)),
        compiler_params=_cparams("parallel"),
        name="final_rmsnorm",
    )(x, g.reshape(1, d))


def _mm_kernel(x_ref, w_ref, *rest, epilogue):
    extras, o_ref = rest[:-1], rest[-1]
    acc = jnp.dot(x_ref[...].astype(BF16), w_ref[...], preferred_element_type=F32)
    if epilogue is not None:
        acc = epilogue(acc, *[e[...] for e in extras])
    o_ref[...] = acc.astype(o_ref.dtype)


def matmul(x, w, out_dtype, tm=1024, tn=512, epilogue=None, extras=(), extra_specs=(), name="matmul",
           xl=None, wl=None, m=None):
    k = x.shape[-1]
    m = x.shape[-2] if m is None else m
    n = w.shape[-1]
    tm, tn = min(tm, m), min(tn, n)
    assert m % tm == 0 and n % tn == 0, (m, tm, n, tn)
    xspec = (pl.BlockSpec((tm, k), lambda i, j: (i, 0)) if xl is None
             else pl.BlockSpec((None, tm, k), lambda i, j: (xl, i, 0)))
    wspec = (pl.BlockSpec((k, tn), lambda i, j: (0, j)) if wl is None
             else pl.BlockSpec((None, k, tn), lambda i, j: (wl, 0, j)))
    return pl.pallas_call(
        functools.partial(_mm_kernel, epilogue=epilogue),
        out_shape=jax.ShapeDtypeStruct((m, n), out_dtype),
        grid=(m // tm, n // tn),
        in_specs=[xspec, wspec] + list(extra_specs),
        out_specs=pl.BlockSpec((tm, tn), lambda i, j: (i, j)),
        compiler_params=_cparams("parallel", "arbitrary"),
        name=name,
    )(x, w, *extras)


def _rope_half(x, cos, sa, sb):
    return x * cos + pltpu.roll(x, 96, 1) * sa + pltpu.roll(x, 32, 1) * sb


def _ep_qrope(acc, cos, sa, sb):
    parts = []
    for h in range(acc.shape[1] // QK_PAD):
        parts.append(acc[:, h * QK_PAD:h * QK_PAD + NOPE_DIM])
        parts.append(_rope_half(acc[:, h * QK_PAD + NOPE_DIM:(h + 1) * QK_PAD], cos, sa, sb))
    return jnp.concatenate(parts, axis=1)


def _ep_gate_a(acc, ga):
    return jax.nn.sigmoid(ga.astype(F32)) * acc


def _ep_gate_b(acc, gb, m1):
    return jax.nn.sigmoid(gb.astype(F32)) * acc + m1


def _ep_resid(acc, x, gate):
    tm, tn = acc.shape
    y = acc.reshape(tm // GROUP, GROUP, tn) * gate
    return x + y.reshape(tm, tn)


def _inproj1_kernel(h_ref, w_ref, gq_ref, gkv_ref, cos_ref, sa_ref, sb_ref,
                    cq_ref, ckv_ref, kr_ref, kvin_ref):
    acc = jnp.dot(h_ref[...], w_ref[...], preferred_element_type=F32)
    cq = acc[:, :Q_LORA]
    ckv = acc[:, Q_LORA:Q_LORA + KV_LORA]
    kr = acc[:, Q_LORA + KV_LORA:]
    cqn = cq * lax.rsqrt(jnp.mean(cq * cq, axis=-1, keepdims=True) + EPS) * gq_ref[...]
    ckvn = ckv * lax.rsqrt(jnp.mean(ckv * ckv, axis=-1, keepdims=True) + EPS) * gkv_ref[...]
    krr = _rope_half(kr, cos_ref[...], sa_ref[...], sb_ref[...])
    cq_ref[...] = cqn.astype(BF16)
    ckv_ref[...] = ckvn
    kr_ref[...] = krr
    kvin_ref[:, :KV_LORA] = ckvn.astype(BF16)
    kvin_ref[:, KV_LORA:] = krr.astype(BF16)


def inproj1(h, w1, l, g_q, g_kv, cos, sa, sb, tm=512):
    n, k = h.shape
    tm = min(tm, n)
    w1n = w1.shape[-1]
    row = lambda i: (i, 0)
    fix = lambda i: (0, 0)
    return pl.pallas_call(
        _inproj1_kernel,
        out_shape=(jax.ShapeDtypeStruct((n, Q_LORA), BF16),
                   jax.ShapeDtypeStruct((n, KV_LORA), F32),
                   jax.ShapeDtypeStruct((n, LANES), F32),
                   jax.ShapeDtypeStruct((n, KV_IN), BF16)),
        grid=(n // tm,),
        in_specs=[pl.BlockSpec((tm, k), row), pl.BlockSpec((None, k, w1n), lambda i: (l, 0, 0)),
                  pl.BlockSpec((1, Q_LORA), fix), pl.BlockSpec((1, KV_LORA), fix),
                  pl.BlockSpec((tm, LANES), row), pl.BlockSpec((tm, LANES), row), pl.BlockSpec((tm, LANES), row)],
        out_specs=(pl.BlockSpec((tm, Q_LORA), row), pl.BlockSpec((tm, KV_LORA), row),
                   pl.BlockSpec((tm, LANES), row), pl.BlockSpec((tm, KV_IN), row)),
        compiler_params=_cparams("parallel"),
        name="inproj1",
    )(h, w1, g_q.reshape(1, -1), g_kv.reshape(1, -1), cos, sa, sb)


def _attn_prompt_kernel(q_ref, k_ref, v_ref, o_ref, *, tq, tk, nh):
    qi = pl.program_id(2)
    nt = (((1,), (1,)), ((), ()))
    r = tk // tq
    qs = [q_ref[:, h * QK_PAD:(h + 1) * QK_PAD] for h in range(nh)]

    def scores(j, h):
        start = pl.multiple_of(j * tk, tk)
        k = k_ref[pl.ds(start, tk), h * QK_PAD:(h + 1) * QK_PAD]
        v = v_ref[pl.ds(start, tk), h * V_DIM:(h + 1) * V_DIM]
        return lax.dot_general(qs[h], k, nt, preferred_element_type=F32), v

    def update(carry, s, v):
        m, l, acc = carry
        m_new = jnp.maximum(m, jnp.max(s, axis=-1, keepdims=True))
        alpha = jnp.exp(m - m_new)
        p = jnp.exp(s - m_new)
        l = alpha * l + jnp.sum(p, axis=-1, keepdims=True)
        acc = alpha * acc + jnp.dot(p.astype(BF16), v, preferred_element_type=F32)
        return m_new, l, acc

    def body(j, carry):
        return tuple(update(carry[h], *scores(j, h)) for h in range(nh))

    init = tuple((jnp.full((tq, 1), -1e30, F32), jnp.zeros((tq, 1), F32), jnp.zeros((tq, V_DIM), F32))
                 for _ in range(nh))
    jd = qi // r
    carry = lax.fori_loop(0, jd, body, init)
    rows = (lax.broadcasted_iota(jnp.int32, (tq, tk), 0) + (qi % r) * tq) // CHUNK
    cols = lax.broadcasted_iota(jnp.int32, (tq, tk), 1) // CHUNK
    for h in range(nh):
        s, v = scores(jd, h)
        s = jnp.where(cols <= rows, s, -1e30)
        m, l, acc = update(carry[h], s, v)
        o_ref[:, h * V_DIM:(h + 1) * V_DIM] = (acc / l).astype(o_ref.dtype)


def attn_prompt(q_cat, kv, bp, s_len, tq=512, tk=1024, nh=2):
    tq = min(tq, s_len)
    tk = min(tk, s_len)
    nq = s_len // tq
    vcol0 = MLA_HEADS * QK_PAD // (V_DIM * nh)
    return pl.pallas_call(
        functools.partial(_attn_prompt_kernel, tq=tq, tk=tk, nh=nh),
        out_shape=jax.ShapeDtypeStruct((q_cat.shape[0], MLA_HEADS * V_DIM), BF16),
        grid=(bp, MLA_HEADS // nh, nq),
        in_specs=[pl.BlockSpec((tq, QK_PAD * nh), lambda b, h, i: (b * nq + i, h)),
                  pl.BlockSpec((s_len, QK_PAD * nh), lambda b, h, i: (b, h)),
                  pl.BlockSpec((s_len, V_DIM * nh), lambda b, h, i: (b, vcol0 + h))],
        out_specs=pl.BlockSpec((tq, V_DIM * nh), lambda b, h, i: (b * nq + i, h)),
        compiler_params=_cparams("parallel", "parallel", "arbitrary"),
        name="attn_prompt",
    )(q_cat, kv, kv)


def _attn_sample_kernel(q_ref, ckv_ref, kr_ref, new_ref, wuk_ref, wuv_ref, prev_ref, o_ref,
                        qall_ref, kall_ref, olat_ref, *, t):
    del prev_ref
    nt = (((1,), (1,)), ((), ()))
    past = ckv_ref.shape[0]
    for h in range(MLA_HEADS):
        qn = q_ref[:, h * QK_PAD:h * QK_PAD + NOPE_DIM]
        qlat = jnp.dot(qn, wuk_ref[h], preferred_element_type=F32)
        qall_ref[h * t:(h + 1) * t, :KV_LORA] = qlat.astype(BF16)
        qall_ref[h * t:(h + 1) * t, KV_LORA:] = q_ref[:, h * QK_PAD + NOPE_DIM:(h + 1) * QK_PAD]
    kall_ref[:, :KV_LORA] = ckv_ref[...].astype(BF16)
    kall_ref[:, KV_LORA:KV_LORA + ROPE_DIM] = kr_ref[...].astype(BF16)
    kall_ref[:, KV_LORA + ROPE_DIM:] = jnp.zeros((past, KV_IN - KV_LORA - ROPE_DIM), BF16)
    qall = qall_ref[...]
    knew = new_ref[...]
    s_past = lax.dot_general(qall, kall_ref[...], nt, preferred_element_type=F32)
    s_new = lax.dot_general(qall, knew, nt, preferred_element_type=F32)
    m = jnp.maximum(jnp.max(s_past, axis=-1, keepdims=True), jnp.max(s_new, axis=-1, keepdims=True))
    p_past = jnp.exp(s_past - m)
    p_new = jnp.exp(s_new - m)
    l = jnp.sum(p_past, axis=-1, keepdims=True) + jnp.sum(p_new, axis=-1, keepdims=True)
    olat = (jnp.dot(p_past.astype(BF16), kall_ref[:, :KV_LORA], preferred_element_type=F32)
            + jnp.dot(p_new.astype(BF16), knew[:, :KV_LORA], preferred_element_type=F32))
    olat_ref[...] = (olat / l).astype(BF16)
    for h in range(MLA_HEADS):
        o = jnp.dot(olat_ref[h * t:(h + 1) * t, :], wuv_ref[h], preferred_element_type=F32)
        o_ref[:, h * V_DIM:(h + 1) * V_DIM] = o.astype(o_ref.dtype)


def attn_sample(q_cat, ckv_past, kr_past, kvin, wuk_t, wuv, l, prev, row0, t):
    _, bs, past, _ = ckv_past.shape
    rb = row0 // t
    return pl.pallas_call(
        functools.partial(_attn_sample_kernel, t=t),
        out_shape=jax.ShapeDtypeStruct(prev.shape, prev.dtype),
        grid=(bs,),
        in_specs=[pl.BlockSpec((t, MLA_HEADS * QK_PAD), lambda b: (rb + b, 0)),
                  pl.BlockSpec((None, None, past, KV_LORA), lambda b: (l, b, 0, 0)),
                  pl.BlockSpec((None, None, past, ROPE_DIM), lambda b: (l, b, 0, 0)),
                  pl.BlockSpec((t, KV_IN), lambda b: (rb + b, 0)),
                  pl.BlockSpec((None, MLA_HEADS, NOPE_DIM, KV_LORA), lambda b: (l, 0, 0, 0)),
                  pl.BlockSpec((None, MLA_HEADS, KV_LORA, V_DIM), lambda b: (l, 0, 0, 0)),
                  pl.BlockSpec(memory_space=pl.ANY)],
        out_specs=pl.BlockSpec((t, MLA_HEADS * V_DIM), lambda b: (rb + b, 0)),
        scratch_shapes=[pltpu.VMEM((MLA_HEADS * t, KV_IN), BF16),
                        pltpu.VMEM((past, KV_IN), BF16),
                        pltpu.VMEM((MLA_HEADS * t, KV_LORA), BF16)],
        input_output_aliases={6: 0},
        compiler_params=_cparams("parallel"),
        name="attn_sample",
    )(q_cat, ckv_past, kr_past, kvin, wuk_t, wuv, prev)


def _ret_consts():
    lg = np.log1p(-(2.0 ** (-5.0 - np.arange(RET_HEADS, dtype=np.float64))))
    idx = np.arange(CHUNK, dtype=np.float64)
    diff = idx[:, None] - idx[None, :]
    dmask = np.where(diff[None] >= 0, np.exp(np.maximum(diff, 0.0)[None] * lg[:, None, None]), 0.0)
    qd = np.exp((idx + 1.0)[None, :] * lg[:, None])
    kd = np.exp((CHUNK - 1.0 - idx)[None, :] * lg[:, None])
    g = np.exp(CHUNK * lg)
    qd = np.broadcast_to(qd[:, :, None], (RET_HEADS, CHUNK, RET_QK_DIM))
    kd = np.broadcast_to(kd[:, :, None], (RET_HEADS, CHUNK, RET_QK_DIM))
    g = np.broadcast_to(g[:, None, None], (RET_HEADS, 1, RET_V_DIM))
    return (jnp.asarray(dmask, F32), jnp.asarray(qd, F32), jnp.asarray(kd, F32), jnp.asarray(g, F32))


def _retention_kernel(q_ref, k_ref, v_ref, g_ref, cos_ref, sin_ref, st0_ref, dm_ref, qd_ref, kd_ref,
                      gam_ref, *rest, cps):
    o_ref, st_ref, state = rest[-3:]
    step = pl.program_id(2)
    nt = (((1,), (1,)), ((), ()))

    @pl.when(step == 0)
    def _():
        state[...] = st0_ref[...]

    dm, qd, kd, gam = dm_ref[0], qd_ref[0], kd_ref[0], gam_ref[0]
    for c in range(cps):
        r = slice(c * CHUNK, (c + 1) * CHUNK)
        cos, sin = cos_ref[r, :], sin_ref[r, :]
        qf = q_ref[r, :].astype(F32)
        kf = k_ref[r, :].astype(F32)
        q = qf * cos + pltpu.roll(qf, 64, 1) * sin
        k = (kf * cos + pltpu.roll(kf, 64, 1) * sin) * (RET_QK_DIM ** -0.5)
        v = v_ref[r, :]
        st = state[...]
        inner = lax.dot_general(q.astype(BF16), k.astype(BF16), nt, preferred_element_type=F32) * dm
        o = (jnp.dot(inner.astype(BF16), v, preferred_element_type=F32)
             + jnp.dot((q * qd).astype(BF16), st.astype(BF16), preferred_element_type=F32))
        kt = jnp.transpose(k * kd).astype(BF16)
        state[...] = st * gam + jnp.dot(kt, v, preferred_element_type=F32)
        mu = jnp.mean(o, axis=-1, keepdims=True)
        oc = o - mu
        var = jnp.mean(oc * oc, axis=-1, keepdims=True)
        gate = g_ref[r, :].astype(F32)
        o_ref[r, :] = (gate * jax.nn.sigmoid(gate) * (oc * lax.rsqrt(var + EPS))).astype(o_ref.dtype)

    @pl.when(step == pl.num_programs(2) - 1)
    def _():
        st_ref[0, 0] = state[...]


def retention(big, cos, sin, state0, l, consts, row0, nseq, seq_len, cps, prev=None):
    r = cps * CHUNK
    steps = seq_len // r
    rb0 = row0 // r
    dmask, qd, kd, gam = consts
    rowblk = lambda s, h, t: rb0 + s * steps + t
    hconst = lambda s, h, t: (h, 0, 0)
    in_specs = [pl.BlockSpec((r, RET_QK_DIM), lambda s, h, t: (rowblk(s, h, t), h)),
                pl.BlockSpec((r, RET_QK_DIM), lambda s, h, t: (rowblk(s, h, t), RET_HEADS + h)),
                pl.BlockSpec((r, RET_V_DIM), lambda s, h, t: (rowblk(s, h, t), RET_HEADS + h)),
                pl.BlockSpec((r, RET_V_DIM), lambda s, h, t: (rowblk(s, h, t), 2 * RET_HEADS + h)),
                pl.BlockSpec((r, RET_QK_DIM), lambda s, h, t: (rowblk(s, h, t), 0)),
                pl.BlockSpec((r, RET_QK_DIM), lambda s, h, t: (rowblk(s, h, t), 0)),
                pl.BlockSpec((None, None, None, RET_QK_DIM, RET_V_DIM), lambda s, h, t: (l, s, h, 0, 0)),
                pl.BlockSpec((1, CHUNK, CHUNK), hconst),
                pl.BlockSpec((1, CHUNK, RET_QK_DIM), hconst),
                pl.BlockSpec((1, CHUNK, RET_QK_DIM), hconst),
                pl.BlockSpec((1, 1, RET_V_DIM), hconst)]
    args = [big, big, big, big, cos, sin, state0, dmask, qd, kd, gam]
    aliases = {}
    if prev is not None:
        in_specs.append(pl.BlockSpec(memory_space=pl.ANY))
        args.append(prev)
        aliases = {len(args) - 1: 0}
    return pl.pallas_call(
        functools.partial(_retention_kernel, cps=cps),
        out_shape=(jax.ShapeDtypeStruct((big.shape[0], RET_HEADS * RET_V_DIM), BF16),
                   jax.ShapeDtypeStruct((nseq, RET_HEADS, RET_QK_DIM, RET_V_DIM), F32)),
        grid=(nseq, RET_HEADS, steps),
        in_specs=in_specs,
        out_specs=(pl.BlockSpec((r, RET_V_DIM), lambda s, h, t: (rowblk(s, h, t), h)),
                   pl.BlockSpec((1, 1, RET_QK_DIM, RET_V_DIM), lambda s, h, t: (s, h, 0, 0))),
        scratch_shapes=[pltpu.VMEM((RET_QK_DIM, RET_V_DIM), F32)],
        input_output_aliases=aliases,
        compiler_params=_cparams("parallel", "parallel", "arbitrary"),
        name="retention",
    )(*args)


_CAND = [(a, b) for a in range(PEER_TOPK) for b in range(PEER_TOPK) if (a + 1) * (b + 1) <= PEER_TOPK]


def _top_distinct(s):
    vals, cnts = [], []
    for _ in range(PEER_TOPK):
        m = jnp.max(s, axis=0, keepdims=True)
        hit = s == m
        vals.append(m)
        cnts.append(jnp.sum(jnp.where(hit, 1.0, 0.0), axis=0, keepdims=True))
        s = jnp.where(hit, -jnp.inf, s)
    return vals, cnts


def _route_kernel(qt_ref, k1_ref, k2_ref, s1_ref, e1_ref, s2_ref, e2_ref, tau_ref, cand_ref, mult_ref):
    tt = qt_ref.shape[1]
    pad = cand_ref.shape[0] - len(_CAND)
    cand_ref[len(_CAND):, :] = jnp.full((pad, tt), -jnp.inf, F32)
    mult_ref[len(_CAND):, :] = jnp.zeros((pad, tt), F32)
    for h in range(PEER_HEADS):
        q1 = qt_ref[h * 2 * PEER_HALF:h * 2 * PEER_HALF + PEER_HALF, :].astype(BF16)
        q2 = qt_ref[h * 2 * PEER_HALF + PEER_HALF:(h + 1) * 2 * PEER_HALF, :].astype(BF16)
        s1 = jnp.dot(k1_ref[...], q1, preferred_element_type=F32)
        s2 = jnp.dot(k2_ref[...], q2, preferred_element_type=F32)
        v1, c1 = _top_distinct(s1)
        v2, c2 = _top_distinct(s2)
        for r, (a, b) in enumerate(_CAND):
            cand_ref[r:r + 1, :] = v1[a] + v2[b]
            mult_ref[r:r + 1, :] = c1[a] * c2[b]
        cand = cand_ref[...]
        mult = mult_ref[...]
        rest = cand
        cum = jnp.zeros_like(v1[0])
        tau = v1[0] + v2[0]
        for _ in range(PEER_TOPK):
            m = jnp.max(rest, axis=0, keepdims=True)
            hit = rest == m
            tau = jnp.where(cum < PEER_TOPK, m, tau)
            cum = cum + jnp.sum(jnp.where(hit, mult, 0.0), axis=0, keepdims=True)
            rest = jnp.where(hit, -jnp.inf, rest)
        top = v1[0] + v2[0]
        z = jnp.sum(jnp.where(cand >= tau, mult * jnp.exp(cand - top), 0.0), axis=0, keepdims=True)
        e1 = jnp.exp(s1 - v1[0])
        e2 = jnp.exp(s2 - v2[0]) / z
        for c in range(tt // LANES):
            lanes = slice(c * LANES, (c + 1) * LANES)
            s1_ref[h, c] = s1[:, lanes]
            s2_ref[h, c] = s2[:, lanes]
            e1_ref[h, c] = e1[:, lanes]
            e2_ref[h, c] = e2[:, lanes]
            tau_ref[h, c] = jnp.broadcast_to(tau[:, lanes], (SUBLANES, LANES))


def peer_route(qt, k1, k2, l, tt=256):
    n = qt.shape[1]
    tt = min(tt, n)
    nc = tt // LANES
    big = jax.ShapeDtypeStruct((PEER_HEADS, n // LANES, N_KEYS, LANES), F32)
    bspec = pl.BlockSpec((PEER_HEADS, nc, N_KEYS, LANES), lambda i: (0, i, 0, 0))
    tspec = pl.BlockSpec((PEER_HEADS, nc, SUBLANES, LANES), lambda i: (0, i, 0, 0))
    return pl.pallas_call(
        _route_kernel,
        out_shape=(big, big, big, big, jax.ShapeDtypeStruct((PEER_HEADS, n // LANES, SUBLANES, LANES), F32)),
        grid=(n // tt,),
        in_specs=[pl.BlockSpec((PEER_HEADS * 2 * PEER_HALF, tt), lambda i: (0, i)),
                  pl.BlockSpec((None, N_KEYS, PEER_HALF), lambda i: (l, 0, 0)),
                  pl.BlockSpec((None, N_KEYS, PEER_HALF), lambda i: (l, 0, 0))],
        out_specs=(bspec, bspec, bspec, bspec, tspec),
        scratch_shapes=[pltpu.VMEM((-(-len(_CAND) // SUBLANES) * SUBLANES, tt), F32)] * 2,
        compiler_params=_cparams("parallel"),
        name="peer_route",
    )(qt, k1, k2)


def _build_wg(ii0, n_ii, i_base, act_ref, wg_ref, s1_ref, e1_ref, s2_ref, e2_ref, tau_ref, sg_rows):
    nc = s2_ref.shape[1]
    s = sg_rows // SUBLANES
    for c in range(nc):
        lanes = slice(c * LANES, (c + 1) * LANES)
        for sg in range(N_KEYS // sg_rows):
            rows = slice(sg * sg_rows, (sg + 1) * sg_rows)
            acc = [None] * n_ii
            for h in range(PEER_HEADS):
                s2 = s2_ref[h, c, rows, :].reshape(s, SUBLANES, LANES)
                e2 = e2_ref[h, c, rows, :].reshape(s, SUBLANES, LANES)
                tau = tau_ref[h, c][None]
                for k in range(n_ii):
                    i = i_base + ii0 + k
                    s1r = jnp.broadcast_to(s1_ref[h, c, pl.ds(i, 1), :], (SUBLANES, LANES))[None]
                    e1r = jnp.broadcast_to(e1_ref[h, c, pl.ds(i, 1), :], (SUBLANES, LANES))[None]
                    cc = jnp.where(s2 + s1r >= tau, e2, 0.0) * e1r
                    acc[k] = cc if acc[k] is None else acc[k] + cc
            for k in range(n_ii):
                r0 = (ii0 + k) * N_KEYS + sg * sg_rows
                a = act_ref[r0:r0 + sg_rows, lanes]
                gelu = 0.5 * a * (1.0 + lax.erf(a * (2.0 ** -0.5)))
                wg_ref[r0:r0 + sg_rows, lanes] = (acc[k].reshape(sg_rows, LANES) * gelu).astype(BF16)


def _peer_expert_kernel(ht_ref, u_ref, vt_ref, s1_ref, e1_ref, s2_ref, e2_ref, tau_ref, o_ref,
                        act_ref, wg_ref, *, ipt):
    e = pl.program_id(1)
    half = ipt // 2 * N_KEYS

    @pl.when(e == 0)
    def _():
        o_ref[...] = jnp.zeros_like(o_ref)

    act_ref[:half, :] = jnp.dot(u_ref[:half, :], ht_ref[...], preferred_element_type=F32)
    act_ref[half:, :] = jnp.dot(u_ref[half:, :], ht_ref[...], preferred_element_type=F32)
    for ii0 in range(0, ipt, ipt // 2):
        _build_wg(ii0, ipt // 2, e * ipt, act_ref, wg_ref, s1_ref, e1_ref, s2_ref, e2_ref, tau_ref, 32)
    o_ref[...] += jnp.dot(vt_ref[...], wg_ref[...], preferred_element_type=F32)


def peer_experts(ht, u, vt, l, s1, e1, s2, e2, tau, tt=512, ipt=8):
    d, n = ht.shape
    ne = u.shape[1]
    tt = min(tt, n)
    te = ipt * N_KEYS
    one = pl.Buffered(1)
    rspec = pl.BlockSpec((PEER_HEADS, tt // LANES, N_KEYS, LANES), lambda t, e: (0, t, 0, 0), pipeline_mode=one)
    return pl.pallas_call(
        functools.partial(_peer_expert_kernel, ipt=ipt),
        out_shape=jax.ShapeDtypeStruct((d, n), F32),
        grid=(n // tt, ne // te),
        in_specs=[pl.BlockSpec((d, tt), lambda t, e: (0, t), pipeline_mode=one),
                  pl.BlockSpec((None, te, d), lambda t, e: (l, e, 0)),
                  pl.BlockSpec((None, d, te), lambda t, e: (l, 0, e)),
                  rspec, rspec, rspec, rspec,
                  pl.BlockSpec((PEER_HEADS, tt // LANES, SUBLANES, LANES), lambda t, e: (0, t, 0, 0),
                               pipeline_mode=one)],
        out_specs=pl.BlockSpec((d, tt), lambda t, e: (0, t)),
        scratch_shapes=[pltpu.VMEM((te, tt), F32), pltpu.VMEM((te, tt), BF16)],
        compiler_params=_cparams("parallel", "arbitrary"),
        name="peer_experts",
    )(ht, u, vt, s1, e1, s2, e2, tau)


def _rope_tables(pos):
    pos = pos.astype(F32)[:, None]
    half = ROPE_DIM // 2
    inv = ROPE_THETA ** (-jnp.arange(half, dtype=F32) * 2.0 / ROPE_DIM)
    ang = pos * inv[None, :]
    c, s = jnp.cos(ang), jnp.sin(ang)
    z32 = jnp.zeros_like(c)
    z64 = jnp.concatenate([z32, z32], axis=1)
    cos64 = jnp.concatenate([c, c, z64], axis=1)
    sa64 = jnp.concatenate([-s, z32, z64], axis=1)
    sb64 = jnp.concatenate([z32, s, z64], axis=1)
    half = RET_QK_DIM // 2
    inv = ROPE_THETA ** (-jnp.arange(half, dtype=F32) * 2.0 / RET_QK_DIM)
    ang = pos * inv[None, :]
    c, s = jnp.cos(ang), jnp.sin(ang)
    return cos64, sa64, sb64, jnp.concatenate([c, c], axis=1), jnp.concatenate([-s, s], axis=1)


def _prep_weights(w_ada, w_in, w_uq, w_ukv, w_a, w_b, w_o, w_pq, k1, k2, u, v):
    nl = w_in.shape[0]
    o1 = Q_LORA + KV_LORA + ROPE_DIM
    w1 = jnp.pad(w_in[:, :, :o1], ((0, 0), (0, 0), (0, LANES - ROPE_DIM))).astype(BF16)
    w2 = w_in[:, :, o1:].astype(BF16)
    uq = w_uq.reshape(nl, Q_LORA, MLA_HEADS, NOPE_DIM + ROPE_DIM) * MLA_SCALE
    uq = jnp.pad(uq, ((0, 0), (0, 0), (0, 0), (0, QK_PAD - NOPE_DIM - ROPE_DIM)))
    uq = uq.reshape(nl, Q_LORA, MLA_HEADS * QK_PAD).astype(BF16)
    wk = jnp.pad(w_ukv[..., :NOPE_DIM], ((0, 0), (0, KV_IN - KV_LORA), (0, 0), (0, QK_PAD - NOPE_DIM)))
    eye = jnp.zeros((KV_IN, QK_PAD), F32).at[KV_LORA + jnp.arange(ROPE_DIM), NOPE_DIM + jnp.arange(ROPE_DIM)].set(1.0)
    wk = (wk + eye[None, :, None, :]).reshape(nl, KV_IN, MLA_HEADS * QK_PAD)
    wv = jnp.pad(w_ukv[..., NOPE_DIM:], ((0, 0), (0, KV_IN - KV_LORA), (0, 0), (0, 0)))
    wv = wv.reshape(nl, KV_IN, MLA_HEADS * V_DIM)
    wkv = jnp.concatenate([wk, wv], axis=2).astype(BF16)
    wuk_t = jnp.transpose(w_ukv[..., :NOPE_DIM], (0, 2, 3, 1)).astype(BF16)
    wuv = jnp.transpose(w_ukv[..., NOPE_DIM:], (0, 2, 1, 3)).astype(BF16)
    return dict(w_ada=w_ada.astype(BF16), w1=w1, w2=w2, uq=uq, wkv=wkv, wuk_t=wuk_t, wuv=wuv,
                w_a=w_a.astype(BF16), w_b=w_b.astype(BF16), w_o=w_o.astype(BF16),
                w_pq_t=jnp.swapaxes(w_pq, 1, 2).astype(BF16), k1=k1.astype(BF16), k2=k2.astype(BF16),
                u=u.astype(BF16), v_t=jnp.swapaxes(v, 1, 2).astype(BF16))


def kernel(x_prompt, x_sample, c_prompt, c_sample, cache_ckv, cache_krope, state_ret, w_ada, b_ada, g_mix, g_ffn, w_in, g_q, w_uq, g_kv, w_ukv, w_a, w_b, w_o, w_pq, peer_k1, peer_k2, peer_u, peer_v, g_final):
    bp, s_len, d = x_prompt.shape
    bs, t_s, _ = x_sample.shape
    depth = w_in.shape[0]
    past = cache_ckv.shape[2]
    n_p, n_s = bp * s_len, bs * t_s
    n = n_p + n_s
    assert t_s == GROUP and s_len % GROUP == 0

    x = jnp.concatenate([x_prompt.reshape(n_p, d), x_sample.reshape(n_s, d)], axis=0)
    c_all = jnp.concatenate([c_prompt, c_sample], axis=0)
    gidx = jnp.concatenate([jnp.repeat(jnp.arange(bp), s_len // GROUP), bp + jnp.arange(bs)])
    pos = jnp.concatenate([jnp.tile(jnp.arange(s_len), bp), jnp.tile(past + jnp.arange(t_s), bs)])
    cos64, sa64, sb64, cos128, sin128 = _rope_tables(pos)
    ret_c = _ret_consts()
    w = _prep_weights(w_ada, w_in, w_uq, w_ukv, w_a, w_b, w_o, w_pq, peer_k1, peer_k2, peer_u, peer_v)
    zero_state = jnp.zeros((1, bp, RET_HEADS, RET_QK_DIM, RET_V_DIM), F32)
    silu_c = jax.nn.silu(c_all)
    gspec = lambda tm, tn: pl.BlockSpec((tm // GROUP, 1, tn), lambda i, j: (i, 0, j))
    tile = lambda tm, tn, off=0: pl.BlockSpec((tm, tn), lambda i, j: (i, j + off))
    rope_spec = lambda tm: pl.BlockSpec((tm, LANES), lambda i, j: (i, 0))

    outs = {k: [] for k in ("ckv_p", "kr_p", "st_p", "ckv_s", "kr_s", "st_s")}
    for l in range(depth):
        ada = matmul(silu_c, w["w_ada"], F32, wl=l, name="ada") + b_ada[l][None, :]
        cond = ada.reshape(bp + bs, 6, d)[gidx]
        cg = [cond[:, k, :][:, None, :] for k in range(6)]

        h = modnorm(x, g_mix[l], cg[1], cg[0])
        cqn, ckvn, krr, kvin = inproj1(h, w["w1"], l, g_q[l], g_kv[l], cos64, sa64, sb64)
        big = matmul(h, w["w2"], BF16, tm=1024, tn=1024, wl=l, name="inproj2")
        tmq = 512
        q_cat = matmul(cqn, w["uq"], BF16, tm=tmq, tn=1024, epilogue=_ep_qrope, wl=l,
                       extras=(cos64, sa64, sb64), extra_specs=[rope_spec(tmq)] * 3, name="q_up")
        kv = matmul(kvin, w["wkv"], BF16, tm=1024, tn=1024, wl=l, m=n_p, name="kv_up")
        o_a = attn_prompt(q_cat, kv, bp, s_len)
        o_a = attn_sample(q_cat, cache_ckv, cache_krope, kvin, w["wuk_t"], w["wuv"], l, o_a, n_p, t_s)
        o_b, st_p = retention(big, cos128, sin128, zero_state, 0, ret_c, 0, bp, s_len, min(8, s_len // CHUNK))
        o_b, st_s = retention(big, cos128, sin128, state_ret, l, ret_c, n_p, bs, t_s, 1, prev=o_b)
        tm, tn = 1024, 512
        ga_off = (2 * RET_HEADS * RET_QK_DIM + 2 * RET_HEADS * RET_V_DIM) // tn
        m1 = matmul(o_a, w["w_a"], F32, tm=tm, tn=tn, epilogue=_ep_gate_a, wl=l,
                    extras=(big,), extra_specs=[tile(tm, tn, ga_off)], name="proj_a")
        mixed = matmul(o_b, w["w_b"], BF16, tm=tm, tn=tn, epilogue=_ep_gate_b, wl=l,
                       extras=(big, m1), extra_specs=[tile(tm, tn, ga_off + d // tn), tile(tm, tn)], name="proj_b")
        x = matmul(mixed, w["w_o"], F32, tm=tm, tn=tn, epilogue=_ep_resid, wl=l,
                   extras=(x, cg[2]), extra_specs=[tile(tm, tn), gspec(tm, tn)], name="proj_o")

        outs["ckv_p"].append(ckvn[:n_p].reshape(bp, s_len, KV_LORA))
        outs["kr_p"].append(krr[:n_p, :ROPE_DIM].reshape(bp, s_len, ROPE_DIM))
        outs["st_p"].append(st_p)
        outs["ckv_s"].append(ckvn[n_p:].reshape(bs, t_s, KV_LORA))
        outs["kr_s"].append(krr[n_p:, :ROPE_DIM].reshape(bs, t_s, ROPE_DIM))
        outs["st_s"].append(st_s)

        h2t = modnorm(x, g_ffn[l], cg[4], cg[3]).T
        qt = matmul(w["w_pq_t"], h2t, F32, tm=2048, tn=512, xl=l, name="peer_q")
        s1, e1, s2, e2, tau = peer_route(qt, w["k1"], w["k2"], l)
        pt = peer_experts(h2t, w["u"], w["v_t"], l, s1, e1, s2, e2, tau)
        x = x + (pt.T.reshape(n // GROUP, GROUP, d) * cg[5]).reshape(n, d)

    y_p = final_rmsnorm(x, g_final, 0, n_p)
    y_s = final_rmsnorm(x, g_final, n_p, n_s)
    st = lambda k: jnp.stack(outs[k], axis=0)
    return (y_p.reshape(bp, s_len, d), y_s.reshape(bs, t_s, d),
            st("ckv_p"), st("kr_p"), st("st_p"), st("ckv_s"), st("kr_s"), st("st_s"))
```

```python
import functools

import numpy as np
import jax
import jax.numpy as jnp
from jax import lax
from jax.experimental import pallas as pl
from jax.experimental.pallas import tpu as pltpu

F32 = jnp.float32
BF16 = jnp.bfloat16

D_MODEL = 2048
CHUNK = 64
EPS = 1e-6
ROPE_THETA = 10000.0
MLA_HEADS = 16
Q_LORA = 512
KV_LORA = 512
NOPE_DIM = 128
ROPE_DIM = 64
V_DIM = 128
MLA_SCALE = (NOPE_DIM + ROPE_DIM) ** -0.5
QK_PAD = 256
KV_IN = KV_LORA + 128
RET_HEADS = 8
RET_QK_DIM = 128
RET_V_DIM = 256
PEER_HEADS = 8
N_KEYS = 128
PEER_HALF = 128
PEER_TOPK = 16
GROUP = 64
LANES = 128
SUBLANES = 8

VMEM_LIMIT = 60 * 1024 * 1024


def _cparams(*sem):
    return pltpu.CompilerParams(dimension_semantics=sem, vmem_limit_bytes=VMEM_LIMIT)


def _modnorm_kernel(x_ref, g_ref, sc_ref, sh_ref, o_ref, *, transposed):
    x = x_ref[...]
    tr, d = x.shape
    y = x * lax.rsqrt(jnp.mean(x * x, axis=-1, keepdims=True) + EPS) * g_ref[...]
    y = (y.reshape(tr // GROUP, GROUP, d) * (1.0 + sc_ref[...]) + sh_ref[...]).reshape(tr, d)
    if transposed:
        y = jnp.transpose(y)
    o_ref[...] = y.astype(o_ref.dtype)


def modnorm(x, g, scale_g, shift_g, tr=512, transposed=False):
    n, d = x.shape
    tr = min(tr, n)
    gpt = tr // GROUP
    return pl.pallas_call(
        functools.partial(_modnorm_kernel, transposed=transposed),
        out_shape=jax.ShapeDtypeStruct((d, n) if transposed else (n, d), BF16),
        grid=(n // tr,),
        in_specs=[pl.BlockSpec((tr, d), lambda i: (i, 0)),
                  pl.BlockSpec((1, d), lambda i: (0, 0)),
                  pl.BlockSpec((gpt, 1, d), lambda i: (i, 0, 0)),
                  pl.BlockSpec((gpt, 1, d), lambda i: (i, 0, 0))],
        out_specs=(pl.BlockSpec((d, tr), lambda i: (0, i)) if transposed
                   else pl.BlockSpec((tr, d), lambda i: (i, 0))),
        compiler_params=_cparams("parallel"),
        name="modnorm",
    )(x, g.reshape(1, d), scale_g, shift_g)


def _rmsnorm_kernel(x_ref, g_ref, o_ref):
    x = x_ref[...]
    o_ref[...] = x * lax.rsqrt(jnp.mean(x * x, axis=-1, keepdims=True) + EPS) * g_ref[...]


def final_rmsnorm(x, g, row0, nrows, tr=512):
    d = x.shape[1]
    tr = min(tr, nrows)
    rb = row0 // tr
    return pl.pallas_call(
        _rmsnorm_kernel,
        out_shape=jax.ShapeDtypeStruct((nrows, d), F32),
        grid=(nrows // tr,),
        in_specs=[pl.BlockSpec((tr, d), lambda i: (rb + i, 0)),
                  pl.BlockSpec((1, d), lambda i: (0, 0))],
        out_specs=pl.BlockSpec((tr, d), lambda i: (i, 0)),
        compiler_params=_cparams("parallel"),
        name="final_rmsnorm",
    )(x, g.reshape(1, d))


def _mm_kernel(x_ref, w_ref, *rest, epilogue):
    extras, o_ref = rest[:-1], rest[-1]
    acc = jnp.dot(x_ref[...].astype(BF16), w_ref[...].astype(BF16), preferred_element_type=F32)
    if epilogue is not None:
        acc = epilogue(acc, *[e[...] for e in extras])
    o_ref[...] = acc.astype(o_ref.dtype)


def matmul(x, w, out_dtype, tm=1024, tn=512, epilogue=None, extras=(), extra_specs=(), name="matmul",
           xl=None, wl=None, m=None):
    k = x.shape[-1]
    m = x.shape[-2] if m is None else m
    n = w.shape[-1]
    tm, tn = min(tm, m), min(tn, n)
    assert m % tm == 0 and n % tn == 0, (m, tm, n, tn)
    xspec = (pl.BlockSpec((tm, k), lambda i, j: (i, 0)) if xl is None
             else pl.BlockSpec((None, tm, k), lambda i, j: (xl, i, 0)))
    wspec = (pl.BlockSpec((k, tn), lambda i, j: (0, j)) if wl is None
             else pl.BlockSpec((None, k, tn), lambda i, j: (wl, 0, j)))
    return pl.pallas_call(
        functools.partial(_mm_kernel, epilogue=epilogue),
        out_shape=jax.ShapeDtypeStruct((m, n), out_dtype),
        grid=(m // tm, n // tn),
        in_specs=[xspec, wspec] + list(extra_specs),
        out_specs=pl.BlockSpec((tm, tn), lambda i, j: (i, j)),
        compiler_params=_cparams("parallel", "arbitrary"),
        name=name,
    )(x, w, *extras)


def _rope_half(x, cos, sa, sb):
    return x * cos + pltpu.roll(x, 96, 1) * sa + pltpu.roll(x, 32, 1) * sb


def _ep_qrope(acc, cos, sa, sb):
    parts = []
    for h in range(acc.shape[1] // QK_PAD):
        parts.append(acc[:, h * QK_PAD:h * QK_PAD + NOPE_DIM])
        parts.append(_rope_half(acc[:, h * QK_PAD + NOPE_DIM:(h + 1) * QK_PAD], cos, sa, sb))
    return jnp.concatenate(parts, axis=1)


def _ep_gate_a(acc, ga):
    return jax.nn.sigmoid(ga.astype(F32)) * acc


def _ep_gate_b(acc, gb, m1):
    return jax.nn.sigmoid(gb.astype(F32)) * acc + m1


def _ep_resid(acc, x, gate):
    tm, tn = acc.shape
    y = acc.reshape(tm // GROUP, GROUP, tn) * gate
    return x + y.reshape(tm, tn)


def _inproj1_kernel(h_ref, w_ref, gq_ref, gkv_ref, cos_ref, sa_ref, sb_ref,
                    cq_ref, ckv_ref, kr_ref, kvin_ref):
    acc = jnp.dot(h_ref[...], w_ref[...], preferred_element_type=F32)
    cq = acc[:, :Q_LORA]
    ckv = acc[:, Q_LORA:Q_LORA + KV_LORA]
    kr = acc[:, Q_LORA + KV_LORA:]
    cqn = cq * lax.rsqrt(jnp.mean(cq * cq, axis=-1, keepdims=True) + EPS) * gq_ref[...]
    ckvn = ckv * lax.rsqrt(jnp.mean(ckv * ckv, axis=-1, keepdims=True) + EPS) * gkv_ref[...]
    krr = _rope_half(kr, cos_ref[...], sa_ref[...], sb_ref[...])
    cq_ref[...] = cqn.astype(BF16)
    ckv_ref[...] = ckvn
    kr_ref[...] = krr
    kvin_ref[:, :KV_LORA] = ckvn.astype(BF16)
    kvin_ref[:, KV_LORA:] = krr.astype(BF16)


def inproj1(h, w1, l, g_q, g_kv, cos, sa, sb, tm=512):
    n, k = h.shape
    tm = min(tm, n)
    w1n = w1.shape[-1]
    row = lambda i: (i, 0)
    fix = lambda i: (0, 0)
    return pl.pallas_call(
        _inproj1_kernel,
        out_shape=(jax.ShapeDtypeStruct((n, Q_LORA), BF16),
                   jax.ShapeDtypeStruct((n, KV_LORA), F32),
                   jax.ShapeDtypeStruct((n, LANES), F32),
                   jax.ShapeDtypeStruct((n, KV_IN), BF16)),
        grid=(n // tm,),
        in_specs=[pl.BlockSpec((tm, k), row), pl.BlockSpec((None, k, w1n), lambda i: (l, 0, 0)),
                  pl.BlockSpec((1, Q_LORA), fix), pl.BlockSpec((1, KV_LORA), fix),
                  pl.BlockSpec((tm, LANES), row), pl.BlockSpec((tm, LANES), row), pl.BlockSpec((tm, LANES), row)],
        out_specs=(pl.BlockSpec((tm, Q_LORA), row), pl.BlockSpec((tm, KV_LORA), row),
                   pl.BlockSpec((tm, LANES), row), pl.BlockSpec((tm, KV_IN), row)),
        compiler_params=_cparams("parallel"),
        name="inproj1",
    )(h, w1, g_q.reshape(1, -1), g_kv.reshape(1, -1), cos, sa, sb)


def _attn_prompt_kernel(q_ref, k_ref, v_ref, o_ref, *, tq, tk, nh):
    qi = pl.program_id(2)
    nt = (((1,), (1,)), ((), ()))
    r = tk // tq
    qs = [q_ref[:, h * QK_PAD:(h + 1) * QK_PAD] for h in range(nh)]

    def scores(j, h):
        start = pl.multiple_of(j * tk, tk)
        k = k_ref[pl.ds(start, tk), h * QK_PAD:(h + 1) * QK_PAD]
        v = v_ref[pl.ds(start, tk), h * V_DIM:(h + 1) * V_DIM]
        return lax.dot_general(qs[h], k, nt, preferred_element_type=F32), v

    def update(carry, s, v):
        m, l, acc = carry
        m_new = jnp.maximum(m, jnp.max(s, axis=-1, keepdims=True))
        alpha = jnp.exp(m - m_new)
        p = jnp.exp(s - m_new)
        l = alpha * l + jnp.sum(p, axis=-1, keepdims=True)
        acc = alpha * acc + jnp.dot(p.astype(BF16), v, preferred_element_type=F32)
        return m_new, l, acc

    def body(j, carry):
        return tuple(update(carry[h], *scores(j, h)) for h in range(nh))

    init = tuple((jnp.full((tq, 1), -1e30, F32), jnp.zeros((tq, 1), F32), jnp.zeros((tq, V_DIM), F32))
                 for _ in range(nh))
    jd = qi // r
    carry = lax.fori_loop(0, jd, body, init)
    rows = (lax.broadcasted_iota(jnp.int32, (tq, tk), 0) + (qi % r) * tq) // CHUNK
    cols = lax.broadcasted_iota(jnp.int32, (tq, tk), 1) // CHUNK
    for h in range(nh):
        s, v = scores(jd, h)
        s = jnp.where(cols <= rows, s, -1e30)
        m, l, acc = update(carry[h], s, v)
        o_ref[:, h * V_DIM:(h + 1) * V_DIM] = (acc / l).astype(o_ref.dtype)


def attn_prompt(q_cat, kv, bp, s_len, tq=512, tk=1024, nh=2):
    tq = min(tq, s_len)
    tk = min(tk, s_len)
    nq = s_len // tq
    vcol0 = MLA_HEADS * QK_PAD // (V_DIM * nh)
    return pl.pallas_call(
        functools.partial(_attn_prompt_kernel, tq=tq, tk=tk, nh=nh),
        out_shape=jax.ShapeDtypeStruct((q_cat.shape[0], MLA_HEADS * V_DIM), BF16),
        grid=(bp, MLA_HEADS // nh, nq),
        in_specs=[pl.BlockSpec((tq, QK_PAD * nh), lambda b, h, i: (b * nq + i, h)),
                  pl.BlockSpec((s_len, QK_PAD * nh), lambda b, h, i: (b, h)),
                  pl.BlockSpec((s_len, V_DIM * nh), lambda b, h, i: (b, vcol0 + h))],
        out_specs=pl.BlockSpec((tq, V_DIM * nh), lambda b, h, i: (b * nq + i, h)),
        compiler_params=_cparams("parallel", "parallel", "arbitrary"),
        name="attn_prompt",
    )(q_cat, kv, kv)


def _attn_sample_kernel(q_ref, ckv_ref, kr_ref, new_ref, wuk_ref, wuv_ref, prev_ref, o_ref,
                        qall_ref, kall_ref, olat_ref, *, t):
    del prev_ref
    nt = (((1,), (1,)), ((), ()))
    past = ckv_ref.shape[0]
    for h in range(MLA_HEADS):
        qn = q_ref[:, h * QK_PAD:h * QK_PAD + NOPE_DIM]
        qlat = jnp.dot(qn, wuk_ref[h], preferred_element_type=F32)
        qall_ref[h * t:(h + 1) * t, :KV_LORA] = qlat.astype(BF16)
        qall_ref[h * t:(h + 1) * t, KV_LORA:] = q_ref[:, h * QK_PAD + NOPE_DIM:(h + 1) * QK_PAD]
    kall_ref[:, :KV_LORA] = ckv_ref[...].astype(BF16)
    kall_ref[:, KV_LORA:KV_LORA + ROPE_DIM] = kr_ref[...].astype(BF16)
    kall_ref[:, KV_LORA + ROPE_DIM:] = jnp.zeros((past, KV_IN - KV_LORA - ROPE_DIM), BF16)
    qall = qall_ref[...]
    knew = new_ref[...]
    s_past = lax.dot_general(qall, kall_ref[...], nt, preferred_element_type=F32)
    s_new = lax.dot_general(qall, knew, nt, preferred_element_type=F32)
    m = jnp.maximum(jnp.max(s_past, axis=-1, keepdims=True), jnp.max(s_new, axis=-1, keepdims=True))
    p_past = jnp.exp(s_past - m)
    p_new = jnp.exp(s_new - m)
    l = jnp.sum(p_past, axis=-1, keepdims=True) + jnp.sum(p_new, axis=-1, keepdims=True)
    olat = (jnp.dot(p_past.astype(BF16), kall_ref[:, :KV_LORA], preferred_element_type=F32)
            + jnp.dot(p_new.astype(BF16), knew[:, :KV_LORA], preferred_element_type=F32))
    olat_ref[...] = (olat / l).astype(BF16)
    for h in range(MLA_HEADS):
        o = jnp.dot(olat_ref[h * t:(h + 1) * t, :], wuv_ref[h], preferred_element_type=F32)
        o_ref[:, h * V_DIM:(h + 1) * V_DIM] = o.astype(o_ref.dtype)


def attn_sample(q_cat, ckv_past, kr_past, kvin, wuk_t, wuv, l, prev, row0, t):
    _, bs, past, _ = ckv_past.shape
    rb = row0 // t
    return pl.pallas_call(
        functools.partial(_attn_sample_kernel, t=t),
        out_shape=jax.ShapeDtypeStruct(prev.shape, prev.dtype),
        grid=(bs,),
        in_specs=[pl.BlockSpec((t, MLA_HEADS * QK_PAD), lambda b: (rb + b, 0)),
                  pl.BlockSpec((None, None, past, KV_LORA), lambda b: (l, b, 0, 0)),
                  pl.BlockSpec((None, None, past, ROPE_DIM), lambda b: (l, b, 0, 0)),
                  pl.BlockSpec((t, KV_IN), lambda b: (rb + b, 0)),
                  pl.BlockSpec((None, MLA_HEADS, NOPE_DIM, KV_LORA), lambda b: (l, 0, 0, 0)),
                  pl.BlockSpec((None, MLA_HEADS, KV_LORA, V_DIM), lambda b: (l, 0, 0, 0)),
                  pl.BlockSpec(memory_space=pl.ANY)],
        out_specs=pl.BlockSpec((t, MLA_HEADS * V_DIM), lambda b: (rb + b, 0)),
        scratch_shapes=[pltpu.VMEM((MLA_HEADS * t, KV_IN), BF16),
                        pltpu.VMEM((past, KV_IN), BF16),
                        pltpu.VMEM((MLA_HEADS * t, KV_LORA), BF16)],
        input_output_aliases={6: 0},
        compiler_params=_cparams("parallel"),
        name="attn_sample",
    )(q_cat, ckv_past, kr_past, kvin, wuk_t, wuv, prev)


def _ret_consts():
    lg = np.log1p(-(2.0 ** (-5.0 - np.arange(RET_HEADS, dtype=np.float64))))
    idx = np.arange(CHUNK, dtype=np.float64)
    diff = idx[:, None] - idx[None, :]
    dmask = np.where(diff[None] >= 0, np.exp(np.maximum(diff, 0.0)[None] * lg[:, None, None]), 0.0)
    qd = np.exp((idx + 1.0)[None, :] * lg[:, None])
    kd = np.exp((CHUNK - 1.0 - idx)[None, :] * lg[:, None])
    g = np.exp(CHUNK * lg)
    qd = np.broadcast_to(qd[:, :, None], (RET_HEADS, CHUNK, RET_QK_DIM))
    kd = np.broadcast_to(kd[:, :, None], (RET_HEADS, CHUNK, RET_QK_DIM))
    g = np.broadcast_to(g[:, None, None], (RET_HEADS, 1, RET_V_DIM))
    return (jnp.asarray(dmask, F32), jnp.asarray(qd, F32), jnp.asarray(kd, F32), jnp.asarray(g, F32))


def _retention_kernel(q_ref, k_ref, v_ref, g_ref, cos_ref, sin_ref, st0_ref, dm_ref, qd_ref, kd_ref,
                      gam_ref, *rest, cps):
    o_ref, st_ref, state = rest[-3:]
    step = pl.program_id(2)
    nt = (((1,), (1,)), ((), ()))

    @pl.when(step == 0)
    def _():
        state[...] = st0_ref[...]

    dm, qd, kd, gam = dm_ref[0], qd_ref[0], kd_ref[0], gam_ref[0]
    for c in range(cps):
        r = slice(c * CHUNK, (c + 1) * CHUNK)
        cos, sin = cos_ref[r, :], sin_ref[r, :]
        qf = q_ref[r, :].astype(F32)
        kf = k_ref[r, :].astype(F32)
        q = qf * cos + pltpu.roll(qf, 64, 1) * sin
        k = (kf * cos + pltpu.roll(kf, 64, 1) * sin) * (RET_QK_DIM ** -0.5)
        v = v_ref[r, :]
        st = state[...]
        inner = lax.dot_general(q.astype(BF16), k.astype(BF16), nt, preferred_element_type=F32) * dm
        o = (jnp.dot(inner.astype(BF16), v, preferred_element_type=F32)
             + jnp.dot((q * qd).astype(BF16), st.astype(BF16), preferred_element_type=F32))
        kt = jnp.transpose(k * kd).astype(BF16)
        state[...] = st * gam + jnp.dot(kt, v, preferred_element_type=F32)
        mu = jnp.mean(o, axis=-1, keepdims=True)
        oc = o - mu
        var = jnp.mean(oc * oc, axis=-1, keepdims=True)
        gate = g_ref[r, :].astype(F32)
        o_ref[r, :] = (gate * jax.nn.sigmoid(gate) * (oc * lax.rsqrt(var + EPS))).astype(o_ref.dtype)

    @pl.when(step == pl.num_programs(2) - 1)
    def _():
        st_ref[0, 0] = state[...]


def retention(big, cos, sin, state0, l, consts, row0, nseq, seq_len, cps, prev=None):
    r = cps * CHUNK
    steps = seq_len // r
    rb0 = row0 // r
    dmask, qd, kd, gam = consts
    rowblk = lambda s, h, t: rb0 + s * steps + t
    hconst = lambda s, h, t: (h, 0, 0)
    in_specs = [pl.BlockSpec((r, RET_QK_DIM), lambda s, h, t: (rowblk(s, h, t), h)),
                pl.BlockSpec((r, RET_QK_DIM), lambda s, h, t: (rowblk(s, h, t), RET_HEADS + h)),
                pl.BlockSpec((r, RET_V_DIM), lambda s, h, t: (rowblk(s, h, t), RET_HEADS + h)),
                pl.BlockSpec((r, RET_V_DIM), lambda s, h, t: (rowblk(s, h, t), 2 * RET_HEADS + h)),
                pl.BlockSpec((r, RET_QK_DIM), lambda s, h, t: (rowblk(s, h, t), 0)),
                pl.BlockSpec((r, RET_QK_DIM), lambda s, h, t: (rowblk(s, h, t), 0)),
                pl.BlockSpec((None, None, None, RET_QK_DIM, RET_V_DIM), lambda s, h, t: (l, s, h, 0, 0)),
                pl.BlockSpec((1, CHUNK, CHUNK), hconst),
                pl.BlockSpec((1, CHUNK, RET_QK_DIM), hconst),
                pl.BlockSpec((1, CHUNK, RET_QK_DIM), hconst),
                pl.BlockSpec((1, 1, RET_V_DIM), hconst)]
    args = [big, big, big, big, cos, sin, state0, dmask, qd, kd, gam]
    aliases = {}
    if prev is not None:
        in_specs.append(pl.BlockSpec(memory_space=pl.ANY))
        args.append(prev)
        aliases = {len(args) - 1: 0}
    return pl.pallas_call(
        functools.partial(_retention_kernel, cps=cps),
        out_shape=(jax.ShapeDtypeStruct((big.shape[0], RET_HEADS * RET_V_DIM), BF16),
                   jax.ShapeDtypeStruct((nseq, RET_HEADS, RET_QK_DIM, RET_V_DIM), F32)),
        grid=(nseq, RET_HEADS, steps),
        in_specs=in_specs,
        out_specs=(pl.BlockSpec((r, RET_V_DIM), lambda s, h, t: (rowblk(s, h, t), h)),
                   pl.BlockSpec((1, 1, RET_QK_DIM, RET_V_DIM), lambda s, h, t: (s, h, 0, 0))),
        scratch_shapes=[pltpu.VMEM((RET_QK_DIM, RET_V_DIM), F32)],
        input_output_aliases=aliases,
        compiler_params=_cparams("parallel", "parallel", "arbitrary"),
        name="retention",
    )(*args)


_CAND = [(a, b) for a in range(PEER_TOPK) for b in range(PEER_TOPK) if (a + 1) * (b + 1) <= PEER_TOPK]


def _top_distinct(s):
    vals, cnts = [], []
    for _ in range(PEER_TOPK):
        m = jnp.max(s, axis=0, keepdims=True)
        hit = s == m
        vals.append(m)
        cnts.append(jnp.sum(jnp.where(hit, 1.0, 0.0), axis=0, keepdims=True))
        s = jnp.where(hit, -jnp.inf, s)
    return vals, cnts


def _route_kernel(qt_ref, k1_ref, k2_ref, s1_ref, e1_ref, s2_ref, e2_ref, tau_ref, cand_ref, mult_ref):
    tt = qt_ref.shape[1]
    pad = cand_ref.shape[0] - len(_CAND)
    cand_ref[len(_CAND):, :] = jnp.full((pad, LANES), -jnp.inf, F32)
    mult_ref[len(_CAND):, :] = jnp.zeros((pad, LANES), F32)
    for h, c in [(h, c) for h in range(PEER_HEADS) for c in range(tt // LANES)]:
        lanes = slice(c * LANES, (c + 1) * LANES)
        q1 = qt_ref[h * 2 * PEER_HALF:h * 2 * PEER_HALF + PEER_HALF, lanes].astype(BF16)
        q2 = qt_ref[h * 2 * PEER_HALF + PEER_HALF:(h + 1) * 2 * PEER_HALF, lanes].astype(BF16)
        s1 = jnp.dot(k1_ref[...], q1, preferred_element_type=F32)
        s2 = jnp.dot(k2_ref[...], q2, preferred_element_type=F32)
        v1, c1 = _top_distinct(s1)
        v2, c2 = _top_distinct(s2)
        for r, (a, b) in enumerate(_CAND):
            cand_ref[r:r + 1, :] = v1[a] + v2[b]
            mult_ref[r:r + 1, :] = c1[a] * c2[b]
        cand = cand_ref[...]
        mult = mult_ref[...]
        rest = cand
        cum = jnp.zeros_like(v1[0])
        tau = v1[0] + v2[0]
        for _ in range(PEER_TOPK):
            m = jnp.max(rest, axis=0, keepdims=True)
            hit = rest == m
            tau = jnp.where(cum < PEER_TOPK, m, tau)
            cum = cum + jnp.sum(jnp.where(hit, mult, 0.0), axis=0, keepdims=True)
            rest = jnp.where(hit, -jnp.inf, rest)
        top = v1[0] + v2[0]
        z = jnp.sum(jnp.where(cand >= tau, mult * jnp.exp(cand - top), 0.0), axis=0, keepdims=True)
        s1_ref[h, :, lanes] = s1
        s2_ref[h, :, lanes] = s2
        e1_ref[h, :, lanes] = jnp.exp(s1 - v1[0])
        e2_ref[h, :, lanes] = jnp.exp(s2 - v2[0]) / z
        tau_ref[h, :, lanes] = tau


def peer_route(qt, k1, k2, l, tt=256):
    n = qt.shape[1]
    tt = min(tt, n)
    big = jax.ShapeDtypeStruct((PEER_HEADS, N_KEYS, n), F32)
    bspec = pl.BlockSpec((PEER_HEADS, N_KEYS, tt), lambda i: (0, 0, i))
    return pl.pallas_call(
        _route_kernel,
        out_shape=(big, big, big, big, jax.ShapeDtypeStruct((PEER_HEADS, 1, n), F32)),
        grid=(n // tt,),
        in_specs=[pl.BlockSpec((PEER_HEADS * 2 * PEER_HALF, tt), lambda i: (0, i)),
                  pl.BlockSpec((None, N_KEYS, PEER_HALF), lambda i: (l, 0, 0)),
                  pl.BlockSpec((None, N_KEYS, PEER_HALF), lambda i: (l, 0, 0))],
        out_specs=(bspec, bspec, bspec, bspec, pl.BlockSpec((PEER_HEADS, 1, tt), lambda i: (0, 0, i))),
        scratch_shapes=[pltpu.VMEM((-(-len(_CAND) // SUBLANES) * SUBLANES, LANES), F32)] * 2,
        compiler_params=_cparams("parallel"),
        name="peer_route",
    )(qt, k1, k2)


def _gate_rows(ii, s1_ref, e1_ref, s2_ref, e2_ref, tau_ref):
    w = None
    for h in range(PEER_HEADS):
        ssum = s2_ref[h] + s1_ref[h, ii:ii + 1, :]
        c = jnp.where(ssum >= tau_ref[h], e2_ref[h], 0.0) * e1_ref[h, ii:ii + 1, :]
        w = c if w is None else w + c
    return w


def _peer_expert_kernel(ht_ref, u_ref, vt_ref, s1c_ref, e1c_ref, s1n_ref, e1n_ref, s2_ref, e2_ref, tau_ref,
                        x_ref, gate_ref, o_ref, acc_ref, wg_ref, w_ref, *, ipt):
    e = pl.program_id(1)
    last = pl.num_programs(1) - 1

    def build(s1_ref, e1_ref, slot):
        for ii in range(ipt):
            w_ref[slot, ii * N_KEYS:(ii + 1) * N_KEYS, :] = _gate_rows(ii, s1_ref, e1_ref, s2_ref, e2_ref, tau_ref)

    @pl.when(e == 0)
    def _():
        acc_ref[...] = jnp.zeros_like(acc_ref)
        build(s1c_ref, e1c_ref, 0)

    build(s1n_ref, e1n_ref, (e + 1) % 2)
    act = jnp.dot(u_ref[...], ht_ref[...], preferred_element_type=F32)
    gelu = 0.5 * act * (1.0 + lax.erf(act * (2.0 ** -0.5)))
    wg_ref[...] = (w_ref[e % 2] * gelu).astype(BF16)
    acc_ref[...] += jnp.dot(vt_ref[...], wg_ref[...], preferred_element_type=F32)

    @pl.when(e == last)
    def _():
        tt, d = o_ref.shape
        y = jnp.transpose(acc_ref[...]).reshape(tt // GROUP, GROUP, d) * gate_ref[...]
        o_ref[...] = x_ref[...] + y.reshape(tt, d)


def peer_experts(ht, u, vt, l, s1, e1, s2, e2, tau, x, gate_g, tt=512, ipt=8):
    d, n = ht.shape
    ne = u.shape[1]
    tt = min(tt, n)
    te = ipt * N_KEYS
    nlast = ne // te - 1
    full = pl.BlockSpec((PEER_HEADS, N_KEYS, tt), lambda t, e: (0, 0, t))
    cur = pl.BlockSpec((PEER_HEADS, ipt, tt), lambda t, e: (0, e, t))
    nxt = pl.BlockSpec((PEER_HEADS, ipt, tt), lambda t, e: (0, jnp.minimum(e + 1, nlast), t))
    return pl.pallas_call(
        functools.partial(_peer_expert_kernel, ipt=ipt),
        out_shape=jax.ShapeDtypeStruct((n, d), F32),
        grid=(n // tt, ne // te),
        in_specs=[pl.BlockSpec((d, tt), lambda t, e: (0, t)),
                  pl.BlockSpec((None, te, d), lambda t, e: (l, e, 0)),
                  pl.BlockSpec((None, d, te), lambda t, e: (l, 0, e)),
                  cur, cur, nxt, nxt, full, full,
                  pl.BlockSpec((PEER_HEADS, 1, tt), lambda t, e: (0, 0, t)),
                  pl.BlockSpec((tt, d), lambda t, e: (t, 0), pipeline_mode=pl.Buffered(1)),
                  pl.BlockSpec((tt // GROUP, 1, d), lambda t, e: (t, 0, 0))],
        out_specs=pl.BlockSpec((tt, d), lambda t, e: (t, 0)),
        scratch_shapes=[pltpu.VMEM((d, tt), F32), pltpu.VMEM((te, tt), BF16), pltpu.VMEM((2, te, tt), F32)],
        compiler_params=_cparams("parallel", "arbitrary"),
        name="peer_experts",
    )(ht, u, vt, s1, e1, s1, e1, s2, e2, tau, x, gate_g)


def _rope_tables(pos):
    pos = pos.astype(F32)[:, None]
    half = ROPE_DIM // 2
    inv = ROPE_THETA ** (-jnp.arange(half, dtype=F32) * 2.0 / ROPE_DIM)
    ang = pos * inv[None, :]
    c, s = jnp.cos(ang), jnp.sin(ang)
    z32 = jnp.zeros_like(c)
    z64 = jnp.concatenate([z32, z32], axis=1)
    cos64 = jnp.concatenate([c, c, z64], axis=1)
    sa64 = jnp.concatenate([-s, z32, z64], axis=1)
    sb64 = jnp.concatenate([z32, s, z64], axis=1)
    half = RET_QK_DIM // 2
    inv = ROPE_THETA ** (-jnp.arange(half, dtype=F32) * 2.0 / RET_QK_DIM)
    ang = pos * inv[None, :]
    c, s = jnp.cos(ang), jnp.sin(ang)
    return cos64, sa64, sb64, jnp.concatenate([c, c], axis=1), jnp.concatenate([-s, s], axis=1)


def _prep_weights(w_ada, w_in, w_uq, w_ukv, w_a, w_b, w_o, w_pq, k1, k2, u, v):
    nl = w_in.shape[0]
    o1 = Q_LORA + KV_LORA + ROPE_DIM
    w1 = jnp.pad(w_in[:, :, :o1], ((0, 0), (0, 0), (0, LANES - ROPE_DIM))).astype(BF16)
    w2 = w_in[:, :, o1:].astype(BF16)
    uq = w_uq.reshape(nl, Q_LORA, MLA_HEADS, NOPE_DIM + ROPE_DIM) * MLA_SCALE
    uq = jnp.pad(uq, ((0, 0), (0, 0), (0, 0), (0, QK_PAD - NOPE_DIM - ROPE_DIM)))
    uq = uq.reshape(nl, Q_LORA, MLA_HEADS * QK_PAD).astype(BF16)
    wk = jnp.pad(w_ukv[..., :NOPE_DIM], ((0, 0), (0, KV_IN - KV_LORA), (0, 0), (0, QK_PAD - NOPE_DIM)))
    eye = jnp.zeros((KV_IN, QK_PAD), F32).at[KV_LORA + jnp.arange(ROPE_DIM), NOPE_DIM + jnp.arange(ROPE_DIM)].set(1.0)
    wk = (wk + eye[None, :, None, :]).reshape(nl, KV_IN, MLA_HEADS * QK_PAD)
    wv = jnp.pad(w_ukv[..., NOPE_DIM:], ((0, 0), (0, KV_IN - KV_LORA), (0, 0), (0, 0)))
    wv = wv.reshape(nl, KV_IN, MLA_HEADS * V_DIM)
    wkv = jnp.concatenate([wk, wv], axis=2).astype(BF16)
    wuk_t = jnp.transpose(w_ukv[..., :NOPE_DIM], (0, 2, 3, 1)).astype(BF16)
    wuv = jnp.transpose(w_ukv[..., NOPE_DIM:], (0, 2, 1, 3)).astype(BF16)
    return dict(w_ada=w_ada, w1=w1, w2=w2, uq=uq, wkv=wkv, wuk_t=wuk_t, wuv=wuv,
                w_a=w_a.astype(BF16), w_b=w_b.astype(BF16), w_o=w_o.astype(BF16),
                w_pq_t=jnp.swapaxes(w_pq, 1, 2).astype(BF16), k1=k1.astype(BF16), k2=k2.astype(BF16),
                u=u.astype(BF16), v_t=jnp.swapaxes(v, 1, 2).astype(BF16))


def kernel(x_prompt, x_sample, c_prompt, c_sample, cache_ckv, cache_krope, state_ret, w_ada, b_ada, g_mix, g_ffn, w_in, g_q, w_uq, g_kv, w_ukv, w_a, w_b, w_o, w_pq, peer_k1, peer_k2, peer_u, peer_v, g_final):
    bp, s_len, d = x_prompt.shape
    bs, t_s, _ = x_sample.shape
    depth = w_in.shape[0]
    past = cache_ckv.shape[2]
    n_p, n_s = bp * s_len, bs * t_s
    n = n_p + n_s
    assert t_s == GROUP and s_len % GROUP == 0

    x = jnp.concatenate([x_prompt.reshape(n_p, d), x_sample.reshape(n_s, d)], axis=0)
    c_all = jnp.concatenate([c_prompt, c_sample], axis=0)
    gidx = jnp.concatenate([jnp.repeat(jnp.arange(bp), s_len // GROUP), bp + jnp.arange(bs)])
    pos = jnp.concatenate([jnp.tile(jnp.arange(s_len), bp), jnp.tile(past + jnp.arange(t_s), bs)])
    cos64, sa64, sb64, cos128, sin128 = _rope_tables(pos)
    ret_c = _ret_consts()
    w = _prep_weights(w_ada, w_in, w_uq, w_ukv, w_a, w_b, w_o, w_pq, peer_k1, peer_k2, peer_u, peer_v)
    zero_state = jnp.zeros((1, bp, RET_HEADS, RET_QK_DIM, RET_V_DIM), F32)
    silu_c = jax.nn.silu(c_all)
    gspec = lambda tm, tn: pl.BlockSpec((tm // GROUP, 1, tn), lambda i, j: (i, 0, j))
    tile = lambda tm, tn, off=0: pl.BlockSpec((tm, tn), lambda i, j: (i, j + off))
    rope_spec = lambda tm: pl.BlockSpec((tm, LANES), lambda i, j: (i, 0))

    outs = {k: [] for k in ("ckv_p", "kr_p", "st_p", "ckv_s", "kr_s", "st_s")}
    for l in range(depth):
        ada = matmul(silu_c, w["w_ada"], F32, wl=l, name="ada") + b_ada[l][None, :]
        cond = ada.reshape(bp + bs, 6, d)[gidx]
        cg = [cond[:, k, :][:, None, :] for k in range(6)]

        h = modnorm(x, g_mix[l], cg[1], cg[0])
        cqn, ckvn, krr, kvin = inproj1(h, w["w1"], l, g_q[l], g_kv[l], cos64, sa64, sb64)
        big = matmul(h, w["w2"], BF16, tm=1024, tn=1024, wl=l, name="inproj2")
        tmq = 512
        q_cat = matmul(cqn, w["uq"], BF16, tm=tmq, tn=1024, epilogue=_ep_qrope, wl=l,
                       extras=(cos64, sa64, sb64), extra_specs=[rope_spec(tmq)] * 3, name="q_up")
        kv = matmul(kvin, w["wkv"], BF16, tm=1024, tn=1024, wl=l, m=n_p, name="kv_up")
        o_a = attn_prompt(q_cat, kv, bp, s_len)
        o_a = attn_sample(q_cat, cache_ckv, cache_krope, kvin, w["wuk_t"], w["wuv"], l, o_a, n_p, t_s)
        o_b, st_p = retention(big, cos128, sin128, zero_state, 0, ret_c, 0, bp, s_len, min(8, s_len // CHUNK))
        o_b, st_s = retention(big, cos128, sin128, state_ret, l, ret_c, n_p, bs, t_s, 1, prev=o_b)
        tm, tn = 1024, 512
        ga_off = (2 * RET_HEADS * RET_QK_DIM + 2 * RET_HEADS * RET_V_DIM) // tn
        m1 = matmul(o_a, w["w_a"], F32, tm=tm, tn=tn, epilogue=_ep_gate_a, wl=l,
                    extras=(big,), extra_specs=[tile(tm, tn, ga_off)], name="proj_a")
        mixed = matmul(o_b, w["w_b"], BF16, tm=tm, tn=tn, epilogue=_ep_gate_b, wl=l,
                       extras=(big, m1), extra_specs=[tile(tm, tn, ga_off + d // tn), tile(tm, tn)], name="proj_b")
        x = matmul(mixed, w["w_o"], F32, tm=tm, tn=tn, epilogue=_ep_resid, wl=l,
                   extras=(x, cg[2]), extra_specs=[tile(tm, tn), gspec(tm, tn)], name="proj_o")

        outs["ckv_p"].append(ckvn[:n_p].reshape(bp, s_len, KV_LORA))
        outs["kr_p"].append(krr[:n_p, :ROPE_DIM].reshape(bp, s_len, ROPE_DIM))
        outs["st_p"].append(st_p)
        outs["ckv_s"].append(ckvn[n_p:].reshape(bs, t_s, KV_LORA))
        outs["kr_s"].append(krr[n_p:, :ROPE_DIM].reshape(bs, t_s, ROPE_DIM))
        outs["st_s"].append(st_s)

        h2t = modnorm(x, g_ffn[l], cg[4], cg[3], transposed=True)
        qt = matmul(w["w_pq_t"], h2t, F32, tm=2048, tn=512, xl=l, name="peer_q")
        s1, e1, s2, e2, tau = peer_route(qt, w["k1"], w["k2"], l)
        x = peer_experts(h2t, w["u"], w["v_t"], l, s1, e1, s2, e2, tau, x, cg[5])

    y_p = final_rmsnorm(x, g_final, 0, n_p)
    y_s = final_rmsnorm(x, g_final, n_p, n_s)
    st = lambda k: jnp.stack(outs[k], axis=0)
    return (y_p.reshape(bp, s_len, d), y_s.reshape(bs, t_s, d),
            st("ckv_p"), st("kr_p"), st("st_p"), st("ckv_s"), st("kr_s"), st("st_s"))
```

```python
import functools

import numpy as np
import jax
import jax.numpy as jnp
from jax import lax
from jax.experimental import pallas as pl
from jax.experimental.pallas import tpu as pltpu

F32 = jnp.float32
BF16 = jnp.bfloat16

D_MODEL = 2048
CHUNK = 64
EPS = 1e-6
ROPE_THETA = 10000.0
MLA_HEADS = 16
Q_LORA = 512
KV_LORA = 512
NOPE_DIM = 128
ROPE_DIM = 64
V_DIM = 128
MLA_SCALE = (NOPE_DIM + ROPE_DIM) ** -0.5
QK_PAD = 256
KV_IN = KV_LORA + 128
RET_HEADS = 8
RET_QK_DIM = 128
RET_V_DIM = 256
PEER_HEADS = 8
N_KEYS = 128
PEER_HALF = 128
PEER_TOPK = 16
GROUP = 64
LANES = 128
SUBLANES = 8

VMEM_LIMIT = 60 * 1024 * 1024


def _cparams(*sem):
    return pltpu.CompilerParams(dimension_semantics=sem, vmem_limit_bytes=VMEM_LIMIT)


def _modnorm_kernel(x_ref, g_ref, sc_ref, sh_ref, o_ref, *, transposed):
    x = x_ref[...]
    tr, d = x.shape
    y = x * lax.rsqrt(jnp.mean(x * x, axis=-1, keepdims=True) + EPS) * g_ref[...]
    y = (y.reshape(tr // GROUP, GROUP, d) * (1.0 + sc_ref[...]) + sh_ref[...]).reshape(tr, d)
    if transposed:
        y = jnp.transpose(y)
    o_ref[...] = y.astype(o_ref.dtype)


def modnorm(x, g, scale_g, shift_g, tr=512, transposed=False):
    n, d = x.shape
    tr = min(tr, n)
    gpt = tr // GROUP
    return pl.pallas_call(
        functools.partial(_modnorm_kernel, transposed=transposed),
        out_shape=jax.ShapeDtypeStruct((d, n) if transposed else (n, d), BF16),
        grid=(n // tr,),
        in_specs=[pl.BlockSpec((tr, d), lambda i: (i, 0)),
                  pl.BlockSpec((1, d), lambda i: (0, 0)),
                  pl.BlockSpec((gpt, 1, d), lambda i: (i, 0, 0)),
                  pl.BlockSpec((gpt, 1, d), lambda i: (i, 0, 0))],
        out_specs=(pl.BlockSpec((d, tr), lambda i: (0, i)) if transposed
                   else pl.BlockSpec((tr, d), lambda i: (i, 0))),
        compiler_params=_cparams("parallel"),
        name="modnorm",
    )(x, g.reshape(1, d), scale_g, shift_g)


def _rmsnorm_kernel(x_ref, g_ref, o_ref):
    x = x_ref[...]
    o_ref[...] = x * lax.rsqrt(jnp.mean(x * x, axis=-1, keepdims=True) + EPS) * g_ref[...]


def final_rmsnorm(x, g, row0, nrows, tr=512):
    d = x.shape[1]
    tr = min(tr, nrows)
    rb = row0 // tr
    return pl.pallas_call(
        _rmsnorm_kernel,
        out_shape=jax.ShapeDtypeStruct((nrows, d), F32),
        grid=(nrows // tr,),
        in_specs=[pl.BlockSpec((tr, d), lambda i: (rb + i, 0)),
                  pl.BlockSpec((1, d), lambda i: (0, 0))],
        out_specs=pl.BlockSpec((tr, d), lambda i: (i, 0)),
        compiler_params=_cparams("parallel"),
        name="final_rmsnorm",
    )(x, g.reshape(1, d))


def _mm_kernel(x_ref, w_ref, *rest, epilogue):
    extras, o_ref = rest[:-1], rest[-1]
    acc = jnp.dot(x_ref[...].astype(BF16), w_ref[...].astype(BF16), preferred_element_type=F32)
    if epilogue is not None:
        acc = epilogue(acc, *[e[...] for e in extras])
    o_ref[...] = acc.astype(o_ref.dtype)


def matmul(x, w, out_dtype, tm=1024, tn=512, epilogue=None, extras=(), extra_specs=(), name="matmul",
           xl=None, wl=None, m=None):
    k = x.shape[-1]
    m = x.shape[-2] if m is None else m
    n = w.shape[-1]
    tm, tn = min(tm, m), min(tn, n)
    assert m % tm == 0 and n % tn == 0, (m, tm, n, tn)
    xspec = (pl.BlockSpec((tm, k), lambda i, j: (i, 0)) if xl is None
             else pl.BlockSpec((None, tm, k), lambda i, j: (xl, i, 0)))
    wspec = (pl.BlockSpec((k, tn), lambda i, j: (0, j)) if wl is None
             else pl.BlockSpec((None, k, tn), lambda i, j: (wl, 0, j)))
    return pl.pallas_call(
        functools.partial(_mm_kernel, epilogue=epilogue),
        out_shape=jax.ShapeDtypeStruct((m, n), out_dtype),
        grid=(m // tm, n // tn),
        in_specs=[xspec, wspec] + list(extra_specs),
        out_specs=pl.BlockSpec((tm, tn), lambda i, j: (i, j)),
        compiler_params=_cparams("parallel", "arbitrary"),
        name=name,
    )(x, w, *extras)


def _rope_half(x, cos, sa, sb):
    return x * cos + pltpu.roll(x, 96, 1) * sa + pltpu.roll(x, 32, 1) * sb


def _ep_qrope(acc, cos, sa, sb):
    parts = []
    for h in range(acc.shape[1] // QK_PAD):
        parts.append(acc[:, h * QK_PAD:h * QK_PAD + NOPE_DIM])
        parts.append(_rope_half(acc[:, h * QK_PAD + NOPE_DIM:(h + 1) * QK_PAD], cos, sa, sb))
    return jnp.concatenate(parts, axis=1)


def _proj_ab_kernel(oa_ref, ob_ref, wa_ref, wb_ref, ga_ref, gb_ref, o_ref):
    ya = jnp.dot(oa_ref[...], wa_ref[...], preferred_element_type=F32)
    yb = jnp.dot(ob_ref[...], wb_ref[...], preferred_element_type=F32)
    o_ref[...] = (jax.nn.sigmoid(ga_ref[...].astype(F32)) * ya
                  + jax.nn.sigmoid(gb_ref[...].astype(F32)) * yb).astype(o_ref.dtype)


def proj_ab(o_a, o_b, w_a, w_b, l, big, gate_col0, tm=1024, tn=512):
    m, k = o_a.shape
    n = w_a.shape[-1]
    tm = min(tm, m)
    goff = gate_col0 // tn
    xspec = pl.BlockSpec((tm, k), lambda i, j: (i, 0))
    wspec = pl.BlockSpec((None, k, tn), lambda i, j: (l, 0, j))
    return pl.pallas_call(
        _proj_ab_kernel,
        out_shape=jax.ShapeDtypeStruct((m, n), BF16),
        grid=(m // tm, n // tn),
        in_specs=[xspec, xspec, wspec, wspec,
                  pl.BlockSpec((tm, tn), lambda i, j: (i, goff + j)),
                  pl.BlockSpec((tm, tn), lambda i, j: (i, goff + n // tn + j))],
        out_specs=pl.BlockSpec((tm, tn), lambda i, j: (i, j)),
        compiler_params=_cparams("parallel", "arbitrary"),
        name="proj_ab",
    )(o_a, o_b, w_a, w_b, big, big)


def _ep_resid(acc, x, gate):
    tm, tn = acc.shape
    y = acc.reshape(tm // GROUP, GROUP, tn) * gate
    return x + y.reshape(tm, tn)


def _inproj1_kernel(h_ref, w_ref, gq_ref, gkv_ref, cos_ref, sa_ref, sb_ref,
                    cq_ref, ckv_ref, kr_ref, kvin_ref):
    acc = jnp.dot(h_ref[...], w_ref[...], preferred_element_type=F32)
    cq = acc[:, :Q_LORA]
    ckv = acc[:, Q_LORA:Q_LORA + KV_LORA]
    kr = acc[:, Q_LORA + KV_LORA:]
    cqn = cq * lax.rsqrt(jnp.mean(cq * cq, axis=-1, keepdims=True) + EPS) * gq_ref[...]
    ckvn = ckv * lax.rsqrt(jnp.mean(ckv * ckv, axis=-1, keepdims=True) + EPS) * gkv_ref[...]
    krr = _rope_half(kr, cos_ref[...], sa_ref[...], sb_ref[...])
    cq_ref[...] = cqn.astype(BF16)
    ckv_ref[...] = ckvn
    kr_ref[...] = krr
    kvin_ref[:, :KV_LORA] = ckvn.astype(BF16)
    kvin_ref[:, KV_LORA:] = krr.astype(BF16)


def inproj1(h, w1, l, g_q, g_kv, cos, sa, sb, tm=512):
    n, k = h.shape
    tm = min(tm, n)
    w1n = w1.shape[-1]
    row = lambda i: (i, 0)
    fix = lambda i: (0, 0)
    return pl.pallas_call(
        _inproj1_kernel,
        out_shape=(jax.ShapeDtypeStruct((n, Q_LORA), BF16),
                   jax.ShapeDtypeStruct((n, KV_LORA), F32),
                   jax.ShapeDtypeStruct((n, LANES), F32),
                   jax.ShapeDtypeStruct((n, KV_IN), BF16)),
        grid=(n // tm,),
        in_specs=[pl.BlockSpec((tm, k), row), pl.BlockSpec((None, k, w1n), lambda i: (l, 0, 0)),
                  pl.BlockSpec((1, Q_LORA), fix), pl.BlockSpec((1, KV_LORA), fix),
                  pl.BlockSpec((tm, LANES), row), pl.BlockSpec((tm, LANES), row), pl.BlockSpec((tm, LANES), row)],
        out_specs=(pl.BlockSpec((tm, Q_LORA), row), pl.BlockSpec((tm, KV_LORA), row),
                   pl.BlockSpec((tm, LANES), row), pl.BlockSpec((tm, KV_IN), row)),
        compiler_params=_cparams("parallel"),
        name="inproj1",
    )(h, w1, g_q.reshape(1, -1), g_kv.reshape(1, -1), cos, sa, sb)


def _attn_prompt_kernel(q_ref, k_ref, v_ref, o_ref, *, tq, tk, nh):
    qi = pl.program_id(2)
    nt = (((1,), (1,)), ((), ()))
    r = tk // tq
    qs = [q_ref[:, h * QK_PAD:(h + 1) * QK_PAD] for h in range(nh)]

    def scores(j, h):
        start = pl.multiple_of(j * tk, tk)
        k = k_ref[pl.ds(start, tk), h * QK_PAD:(h + 1) * QK_PAD]
        v = v_ref[pl.ds(start, tk), h * V_DIM:(h + 1) * V_DIM]
        return lax.dot_general(qs[h], k, nt, preferred_element_type=F32), v

    def update(carry, s, v):
        m, l, acc = carry
        m_new = jnp.maximum(m, jnp.max(s, axis=-1, keepdims=True))
        alpha = jnp.exp(m - m_new)
        p = jnp.exp(s - m_new)
        l = alpha * l + jnp.sum(p, axis=-1, keepdims=True)
        acc = alpha * acc + jnp.dot(p.astype(BF16), v, preferred_element_type=F32)
        return m_new, l, acc

    def body(j, carry):
        return tuple(update(carry[h], *scores(j, h)) for h in range(nh))

    init = tuple((jnp.full((tq, 1), -1e30, F32), jnp.zeros((tq, 1), F32), jnp.zeros((tq, V_DIM), F32))
                 for _ in range(nh))
    jd = qi // r
    carry = lax.fori_loop(0, jd, body, init)
    rows = (lax.broadcasted_iota(jnp.int32, (tq, tk), 0) + (qi % r) * tq) // CHUNK
    cols = lax.broadcasted_iota(jnp.int32, (tq, tk), 1) // CHUNK
    for h in range(nh):
        s, v = scores(jd, h)
        s = jnp.where(cols <= rows, s, -1e30)
        m, l, acc = update(carry[h], s, v)
        o_ref[:, h * V_DIM:(h + 1) * V_DIM] = (acc / l).astype(o_ref.dtype)


def attn_prompt(q_cat, kv, bp, s_len, tq=512, tk=1024, nh=2):
    tq = min(tq, s_len)
    tk = min(tk, s_len)
    nq = s_len // tq
    vcol0 = MLA_HEADS * QK_PAD // (V_DIM * nh)
    return pl.pallas_call(
        functools.partial(_attn_prompt_kernel, tq=tq, tk=tk, nh=nh),
        out_shape=jax.ShapeDtypeStruct((q_cat.shape[0], MLA_HEADS * V_DIM), BF16),
        grid=(bp, MLA_HEADS // nh, nq),
        in_specs=[pl.BlockSpec((tq, QK_PAD * nh), lambda b, h, i: (b * nq + i, h)),
                  pl.BlockSpec((s_len, QK_PAD * nh), lambda b, h, i: (b, h)),
                  pl.BlockSpec((s_len, V_DIM * nh), lambda b, h, i: (b, vcol0 + h))],
        out_specs=pl.BlockSpec((tq, V_DIM * nh), lambda b, h, i: (b * nq + i, h)),
        compiler_params=_cparams("parallel", "parallel", "arbitrary"),
        name="attn_prompt",
    )(q_cat, kv, kv)


def _attn_sample_kernel(q_ref, ckv_ref, kr_ref, new_ref, wuk_ref, wuv_ref, prev_ref, o_ref,
                        qall_ref, kall_ref, olat_ref, *, t):
    del prev_ref
    nt = (((1,), (1,)), ((), ()))
    past = ckv_ref.shape[0]
    for h in range(MLA_HEADS):
        qn = q_ref[:, h * QK_PAD:h * QK_PAD + NOPE_DIM]
        qlat = jnp.dot(qn, wuk_ref[h], preferred_element_type=F32)
        qall_ref[h * t:(h + 1) * t, :KV_LORA] = qlat.astype(BF16)
        qall_ref[h * t:(h + 1) * t, KV_LORA:] = q_ref[:, h * QK_PAD + NOPE_DIM:(h + 1) * QK_PAD]
    kall_ref[:, :KV_LORA] = ckv_ref[...].astype(BF16)
    kall_ref[:, KV_LORA:KV_LORA + ROPE_DIM] = kr_ref[...].astype(BF16)
    kall_ref[:, KV_LORA + ROPE_DIM:] = jnp.zeros((past, KV_IN - KV_LORA - ROPE_DIM), BF16)
    qall = qall_ref[...]
    knew = new_ref[...]
    s_past = lax.dot_general(qall, kall_ref[...], nt, preferred_element_type=F32)
    s_new = lax.dot_general(qall, knew, nt, preferred_element_type=F32)
    m = jnp.maximum(jnp.max(s_past, axis=-1, keepdims=True), jnp.max(s_new, axis=-1, keepdims=True))
    p_past = jnp.exp(s_past - m)
    p_new = jnp.exp(s_new - m)
    l = jnp.sum(p_past, axis=-1, keepdims=True) + jnp.sum(p_new, axis=-1, keepdims=True)
    olat = (jnp.dot(p_past.astype(BF16), kall_ref[:, :KV_LORA], preferred_element_type=F32)
            + jnp.dot(p_new.astype(BF16), knew[:, :KV_LORA], preferred_element_type=F32))
    olat_ref[...] = (olat / l).astype(BF16)
    for h in range(MLA_HEADS):
        o = jnp.dot(olat_ref[h * t:(h + 1) * t, :], wuv_ref[h], preferred_element_type=F32)
        o_ref[:, h * V_DIM:(h + 1) * V_DIM] = o.astype(o_ref.dtype)


def attn_sample(q_cat, ckv_past, kr_past, kvin, wuk_t, wuv, l, prev, row0, t):
    _, bs, past, _ = ckv_past.shape
    rb = row0 // t
    return pl.pallas_call(
        functools.partial(_attn_sample_kernel, t=t),
        out_shape=jax.ShapeDtypeStruct(prev.shape, prev.dtype),
        grid=(bs,),
        in_specs=[pl.BlockSpec((t, MLA_HEADS * QK_PAD), lambda b: (rb + b, 0)),
                  pl.BlockSpec((None, None, past, KV_LORA), lambda b: (l, b, 0, 0)),
                  pl.BlockSpec((None, None, past, ROPE_DIM), lambda b: (l, b, 0, 0)),
                  pl.BlockSpec((t, KV_IN), lambda b: (rb + b, 0)),
                  pl.BlockSpec((None, MLA_HEADS, NOPE_DIM, KV_LORA), lambda b: (l, 0, 0, 0)),
                  pl.BlockSpec((None, MLA_HEADS, KV_LORA, V_DIM), lambda b: (l, 0, 0, 0)),
                  pl.BlockSpec(memory_space=pl.ANY)],
        out_specs=pl.BlockSpec((t, MLA_HEADS * V_DIM), lambda b: (rb + b, 0)),
        scratch_shapes=[pltpu.VMEM((MLA_HEADS * t, KV_IN), BF16),
                        pltpu.VMEM((past, KV_IN), BF16),
                        pltpu.VMEM((MLA_HEADS * t, KV_LORA), BF16)],
        input_output_aliases={6: 0},
        compiler_params=_cparams("parallel"),
        name="attn_sample",
    )(q_cat, ckv_past, kr_past, kvin, wuk_t, wuv, prev)


def _ret_consts():
    lg = np.log1p(-(2.0 ** (-5.0 - np.arange(RET_HEADS, dtype=np.float64))))
    idx = np.arange(CHUNK, dtype=np.float64)
    diff = idx[:, None] - idx[None, :]
    dmask = np.where(diff[None] >= 0, np.exp(np.maximum(diff, 0.0)[None] * lg[:, None, None]), 0.0)
    qd = np.exp((idx + 1.0)[None, :] * lg[:, None])
    kd = np.exp((CHUNK - 1.0 - idx)[None, :] * lg[:, None])
    g = np.exp(CHUNK * lg)
    qd = np.broadcast_to(qd[:, :, None], (RET_HEADS, CHUNK, RET_QK_DIM))
    kd = np.broadcast_to(kd[:, :, None], (RET_HEADS, CHUNK, RET_QK_DIM))
    g = np.broadcast_to(g[:, None, None], (RET_HEADS, 1, RET_V_DIM))
    return (jnp.asarray(dmask, F32), jnp.asarray(qd, F32), jnp.asarray(kd, F32), jnp.asarray(g, F32))


def _retention_kernel(q_ref, k_ref, v_ref, g_ref, cos_ref, sin_ref, st0_ref, dm_ref, qd_ref, kd_ref,
                      gam_ref, *rest, cps, hps):
    o_ref, st_ref, state = rest[-3:]
    step = pl.program_id(2)
    nt = (((1,), (1,)), ((), ()))

    @pl.when(step == 0)
    def _():
        state[...] = st0_ref[...]

    for c in range(cps):
        r = slice(c * CHUNK, (c + 1) * CHUNK)
        cos, sin = cos_ref[r, :], sin_ref[r, :]
        for hh in range(hps):
            qk = slice(hh * RET_QK_DIM, (hh + 1) * RET_QK_DIM)
            vv = slice(hh * RET_V_DIM, (hh + 1) * RET_V_DIM)
            qf = q_ref[r, qk].astype(F32)
            kf = k_ref[r, qk].astype(F32)
            q = qf * cos + pltpu.roll(qf, 64, 1) * sin
            k = (kf * cos + pltpu.roll(kf, 64, 1) * sin) * (RET_QK_DIM ** -0.5)
            v = v_ref[r, vv]
            st = state[hh]
            inner = lax.dot_general(q.astype(BF16), k.astype(BF16), nt, preferred_element_type=F32) * dm_ref[hh]
            o = (jnp.dot(inner.astype(BF16), v, preferred_element_type=F32)
                 + jnp.dot((q * qd_ref[hh]).astype(BF16), st.astype(BF16), preferred_element_type=F32))
            kt = jnp.transpose(k * kd_ref[hh]).astype(BF16)
            state[hh] = st * gam_ref[hh] + jnp.dot(kt, v, preferred_element_type=F32)
            mu = jnp.mean(o, axis=-1, keepdims=True)
            oc = o - mu
            var = jnp.mean(oc * oc, axis=-1, keepdims=True)
            gate = g_ref[r, vv].astype(F32)
            o_ref[r, vv] = (gate * jax.nn.sigmoid(gate) * (oc * lax.rsqrt(var + EPS))).astype(o_ref.dtype)

    @pl.when(step == pl.num_programs(2) - 1)
    def _():
        st_ref[0] = state[...]


def retention(big, cos, sin, state0, l, consts, row0, nseq, seq_len, cps, hps, prev=None):
    r = cps * CHUNK
    steps = seq_len // r
    rb0 = row0 // r
    ng = RET_HEADS // hps
    dmask, qd, kd, gam = consts
    rowblk = lambda s, h, t: rb0 + s * steps + t
    hconst = lambda s, h, t: (h, 0, 0)
    in_specs = [pl.BlockSpec((r, hps * RET_QK_DIM), lambda s, h, t: (rowblk(s, h, t), h)),
                pl.BlockSpec((r, hps * RET_QK_DIM), lambda s, h, t: (rowblk(s, h, t), ng + h)),
                pl.BlockSpec((r, hps * RET_V_DIM), lambda s, h, t: (rowblk(s, h, t), ng + h)),
                pl.BlockSpec((r, hps * RET_V_DIM), lambda s, h, t: (rowblk(s, h, t), 2 * ng + h)),
                pl.BlockSpec((r, RET_QK_DIM), lambda s, h, t: (rowblk(s, h, t), 0)),
                pl.BlockSpec((r, RET_QK_DIM), lambda s, h, t: (rowblk(s, h, t), 0)),
                pl.BlockSpec((None, None, hps, RET_QK_DIM, RET_V_DIM), lambda s, h, t: (l, s, h, 0, 0)),
                pl.BlockSpec((hps, CHUNK, CHUNK), hconst),
                pl.BlockSpec((hps, CHUNK, RET_QK_DIM), hconst),
                pl.BlockSpec((hps, CHUNK, RET_QK_DIM), hconst),
                pl.BlockSpec((hps, 1, RET_V_DIM), hconst)]
    args = [big, big, big, big, cos, sin, state0, dmask, qd, kd, gam]
    aliases = {}
    if prev is not None:
        in_specs.append(pl.BlockSpec(memory_space=pl.ANY))
        args.append(prev)
        aliases = {len(args) - 1: 0}
    return pl.pallas_call(
        functools.partial(_retention_kernel, cps=cps, hps=hps),
        out_shape=(jax.ShapeDtypeStruct((big.shape[0], RET_HEADS * RET_V_DIM), BF16),
                   jax.ShapeDtypeStruct((nseq, RET_HEADS, RET_QK_DIM, RET_V_DIM), F32)),
        grid=(nseq, ng, steps),
        in_specs=in_specs,
        out_specs=(pl.BlockSpec((r, hps * RET_V_DIM), lambda s, h, t: (rowblk(s, h, t), h)),
                   pl.BlockSpec((1, hps, RET_QK_DIM, RET_V_DIM), lambda s, h, t: (s, h, 0, 0))),
        scratch_shapes=[pltpu.VMEM((hps, RET_QK_DIM, RET_V_DIM), F32)],
        input_output_aliases=aliases,
        compiler_params=_cparams("parallel", "parallel", "arbitrary"),
        name="retention",
    )(*args)


_CAND = [(a, b) for a in range(PEER_TOPK) for b in range(PEER_TOPK) if (a + 1) * (b + 1) <= PEER_TOPK]


def _top_distinct(s):
    vals, cnts = [], []
    for _ in range(PEER_TOPK):
        m = jnp.max(s, axis=0, keepdims=True)
        hit = s == m
        vals.append(m)
        cnts.append(jnp.sum(jnp.where(hit, 1.0, 0.0), axis=0, keepdims=True))
        s = jnp.where(hit, -jnp.inf, s)
    return vals, cnts


def _route_kernel(qt_ref, k1_ref, k2_ref, thr_ref, e1_ref, s2_ref, e2_ref, cand_ref, mult_ref):
    tt = qt_ref.shape[1]
    pad = cand_ref.shape[0] - len(_CAND)
    cand_ref[len(_CAND):, :] = jnp.full((pad, LANES), -jnp.inf, F32)
    mult_ref[len(_CAND):, :] = jnp.zeros((pad, LANES), F32)
    for h, c in [(h, c) for h in range(PEER_HEADS) for c in range(tt // LANES)]:
        lanes = slice(c * LANES, (c + 1) * LANES)
        q1 = qt_ref[h * 2 * PEER_HALF:h * 2 * PEER_HALF + PEER_HALF, lanes].astype(BF16)
        q2 = qt_ref[h * 2 * PEER_HALF + PEER_HALF:(h + 1) * 2 * PEER_HALF, lanes].astype(BF16)
        s1 = jnp.dot(k1_ref[...], q1, preferred_element_type=F32)
        s2 = jnp.dot(k2_ref[...], q2, preferred_element_type=F32)
        v1, c1 = _top_distinct(s1)
        v2, c2 = _top_distinct(s2)
        for r, (a, b) in enumerate(_CAND):
            cand_ref[r:r + 1, :] = v1[a] + v2[b]
            mult_ref[r:r + 1, :] = c1[a] * c2[b]
        cand = cand_ref[...]
        mult = mult_ref[...]
        rest = cand
        cum = jnp.zeros_like(v1[0])
        tau = v1[0] + v2[0]
        for _ in range(PEER_TOPK):
            m = jnp.max(rest, axis=0, keepdims=True)
            hit = rest == m
            tau = jnp.where(cum < PEER_TOPK, m, tau)
            cum = cum + jnp.sum(jnp.where(hit, mult, 0.0), axis=0, keepdims=True)
            rest = jnp.where(hit, -jnp.inf, rest)
        top = v1[0] + v2[0]
        z = jnp.sum(jnp.where(cand >= tau, mult * jnp.exp(cand - top), 0.0), axis=0, keepdims=True)
        thr = jnp.full_like(s1, jnp.inf)
        for a in range(PEER_TOPK):
            th_a = jnp.full_like(tau, jnp.inf)
            for b in range(PEER_TOPK // (a + 1)):
                th_a = jnp.where(v1[a] + v2[b] >= tau, v2[b], th_a)
            thr = jnp.where(s1 == v1[a], th_a, thr)
        thr_ref[h, :, lanes] = thr
        s2_ref[h, :, lanes] = s2
        e1_ref[h, :, lanes] = jnp.exp(s1 - v1[0])
        e2_ref[h, :, lanes] = jnp.exp(s2 - v2[0]) / z


def peer_route(qt, k1, k2, l, tt=256):
    n = qt.shape[1]
    tt = min(tt, n)
    big = jax.ShapeDtypeStruct((PEER_HEADS, N_KEYS, n), F32)
    bspec = pl.BlockSpec((PEER_HEADS, N_KEYS, tt), lambda i: (0, 0, i))
    return pl.pallas_call(
        _route_kernel,
        out_shape=(big, big, big, big),
        grid=(n // tt,),
        in_specs=[pl.BlockSpec((PEER_HEADS * 2 * PEER_HALF, tt), lambda i: (0, i)),
                  pl.BlockSpec((None, N_KEYS, PEER_HALF), lambda i: (l, 0, 0)),
                  pl.BlockSpec((None, N_KEYS, PEER_HALF), lambda i: (l, 0, 0))],
        out_specs=(bspec, bspec, bspec, bspec),
        scratch_shapes=[pltpu.VMEM((-(-len(_CAND) // SUBLANES) * SUBLANES, LANES), F32)] * 2,
        compiler_params=_cparams("parallel"),
        name="peer_route",
    )(qt, k1, k2)


def _gate_rows(ii, thr_ref, e1_ref, s2_ref, e2_ref):
    w = None
    for h in range(PEER_HEADS):
        c = jnp.where(s2_ref[h] >= thr_ref[h, ii:ii + 1, :], e2_ref[h], 0.0) * e1_ref[h, ii:ii + 1, :]
        w = c if w is None else w + c
    return w


def _peer_expert_kernel(ht_ref, u_ref, vt_ref, thc_ref, e1c_ref, thn_ref, e1n_ref, s2_ref, e2_ref,
                        x_ref, gate_ref, o_ref, acc_ref, wg_ref, w_ref, *, ipt):
    e = pl.program_id(1)
    last = pl.num_programs(1) - 1

    def build(thr_ref, e1_ref, slot):
        for ii in range(ipt):
            w_ref[slot, ii * N_KEYS:(ii + 1) * N_KEYS, :] = _gate_rows(ii, thr_ref, e1_ref, s2_ref, e2_ref)

    @pl.when(e == 0)
    def _():
        acc_ref[...] = jnp.zeros_like(acc_ref)
        build(thc_ref, e1c_ref, 0)

    build(thn_ref, e1n_ref, (e + 1) % 2)
    act = jnp.dot(u_ref[...], ht_ref[...], preferred_element_type=F32)
    gelu = 0.5 * act * (1.0 + lax.erf(act * (2.0 ** -0.5)))
    wg_ref[...] = (w_ref[e % 2] * gelu).astype(BF16)
    acc_ref[...] += jnp.dot(vt_ref[...], wg_ref[...], preferred_element_type=F32)

    @pl.when(e == last)
    def _():
        tt, d = o_ref.shape
        y = jnp.transpose(acc_ref[...]).reshape(tt // GROUP, GROUP, d) * gate_ref[...]
        o_ref[...] = x_ref[...] + y.reshape(tt, d)


def peer_experts(ht, u, vt, l, thr, e1, s2, e2, x, gate_g, tt=512, ipt=8):
    d, n = ht.shape
    ne = u.shape[1]
    tt = min(tt, n)
    te = ipt * N_KEYS
    nlast = ne // te - 1
    full = pl.BlockSpec((PEER_HEADS, N_KEYS, tt), lambda t, e: (0, 0, t))
    cur = pl.BlockSpec((PEER_HEADS, ipt, tt), lambda t, e: (0, e, t))
    nxt = pl.BlockSpec((PEER_HEADS, ipt, tt), lambda t, e: (0, jnp.minimum(e + 1, nlast), t))
    return pl.pallas_call(
        functools.partial(_peer_expert_kernel, ipt=ipt),
        out_shape=jax.ShapeDtypeStruct((n, d), F32),
        grid=(n // tt, ne // te),
        in_specs=[pl.BlockSpec((d, tt), lambda t, e: (0, t)),
                  pl.BlockSpec((None, te, d), lambda t, e: (l, e, 0)),
                  pl.BlockSpec((None, d, te), lambda t, e: (l, 0, e)),
                  cur, cur, nxt, nxt, full, full,
                  pl.BlockSpec((tt, d), lambda t, e: (t, 0), pipeline_mode=pl.Buffered(1)),
                  pl.BlockSpec((tt // GROUP, 1, d), lambda t, e: (t, 0, 0))],
        out_specs=pl.BlockSpec((tt, d), lambda t, e: (t, 0)),
        scratch_shapes=[pltpu.VMEM((d, tt), F32), pltpu.VMEM((te, tt), BF16), pltpu.VMEM((2, te, tt), F32)],
        compiler_params=_cparams("parallel", "arbitrary"),
        name="peer_experts",
    )(ht, u, vt, thr, e1, thr, e1, s2, e2, x, gate_g)


def _rope_tables(pos):
    pos = pos.astype(F32)[:, None]
    half = ROPE_DIM // 2
    inv = ROPE_THETA ** (-jnp.arange(half, dtype=F32) * 2.0 / ROPE_DIM)
    ang = pos * inv[None, :]
    c, s = jnp.cos(ang), jnp.sin(ang)
    z32 = jnp.zeros_like(c)
    z64 = jnp.concatenate([z32, z32], axis=1)
    cos64 = jnp.concatenate([c, c, z64], axis=1)
    sa64 = jnp.concatenate([-s, z32, z64], axis=1)
    sb64 = jnp.concatenate([z32, s, z64], axis=1)
    half = RET_QK_DIM // 2
    inv = ROPE_THETA ** (-jnp.arange(half, dtype=F32) * 2.0 / RET_QK_DIM)
    ang = pos * inv[None, :]
    c, s = jnp.cos(ang), jnp.sin(ang)
    return cos64, sa64, sb64, jnp.concatenate([c, c], axis=1), jnp.concatenate([-s, s], axis=1)


def _prep_weights(w_ada, w_in, w_uq, w_ukv, w_a, w_b, w_o, w_pq, k1, k2, u, v):
    nl = w_in.shape[0]
    o1 = Q_LORA + KV_LORA + ROPE_DIM
    w1 = jnp.pad(w_in[:, :, :o1], ((0, 0), (0, 0), (0, LANES - ROPE_DIM))).astype(BF16)
    w2 = w_in[:, :, o1:].astype(BF16)
    uq = w_uq.reshape(nl, Q_LORA, MLA_HEADS, NOPE_DIM + ROPE_DIM) * MLA_SCALE
    uq = jnp.pad(uq, ((0, 0), (0, 0), (0, 0), (0, QK_PAD - NOPE_DIM - ROPE_DIM)))
    uq = uq.reshape(nl, Q_LORA, MLA_HEADS * QK_PAD).astype(BF16)
    wk = jnp.pad(w_ukv[..., :NOPE_DIM], ((0, 0), (0, KV_IN - KV_LORA), (0, 0), (0, QK_PAD - NOPE_DIM)))
    eye = jnp.zeros((KV_IN, QK_PAD), F32).at[KV_LORA + jnp.arange(ROPE_DIM), NOPE_DIM + jnp.arange(ROPE_DIM)].set(1.0)
    wk = (wk + eye[None, :, None, :]).reshape(nl, KV_IN, MLA_HEADS * QK_PAD)
    wv = jnp.pad(w_ukv[..., NOPE_DIM:], ((0, 0), (0, KV_IN - KV_LORA), (0, 0), (0, 0)))
    wv = wv.reshape(nl, KV_IN, MLA_HEADS * V_DIM)
    wkv = jnp.concatenate([wk, wv], axis=2).astype(BF16)
    wuk_t = jnp.transpose(w_ukv[..., :NOPE_DIM], (0, 2, 3, 1)).astype(BF16)
    wuv = jnp.transpose(w_ukv[..., NOPE_DIM:], (0, 2, 1, 3)).astype(BF16)
    return dict(w_ada=w_ada, w1=w1, w2=w2, uq=uq, wkv=wkv, wuk_t=wuk_t, wuv=wuv,
                w_a=w_a.astype(BF16), w_b=w_b.astype(BF16), w_o=w_o.astype(BF16),
                w_pq_t=jnp.swapaxes(w_pq, 1, 2).astype(BF16), k1=k1.astype(BF16), k2=k2.astype(BF16),
                u=u.astype(BF16), v_t=jnp.swapaxes(v, 1, 2).astype(BF16))


def kernel(x_prompt, x_sample, c_prompt, c_sample, cache_ckv, cache_krope, state_ret, w_ada, b_ada, g_mix, g_ffn, w_in, g_q, w_uq, g_kv, w_ukv, w_a, w_b, w_o, w_pq, peer_k1, peer_k2, peer_u, peer_v, g_final):
    bp, s_len, d = x_prompt.shape
    bs, t_s, _ = x_sample.shape
    depth = w_in.shape[0]
    past = cache_ckv.shape[2]
    n_p, n_s = bp * s_len, bs * t_s
    n = n_p + n_s
    assert t_s == GROUP and s_len % GROUP == 0

    x = jnp.concatenate([x_prompt.reshape(n_p, d), x_sample.reshape(n_s, d)], axis=0)
    c_all = jnp.concatenate([c_prompt, c_sample], axis=0)
    gidx = jnp.concatenate([jnp.repeat(jnp.arange(bp), s_len // GROUP), bp + jnp.arange(bs)])
    pos = jnp.concatenate([jnp.tile(jnp.arange(s_len), bp), jnp.tile(past + jnp.arange(t_s), bs)])
    cos64, sa64, sb64, cos128, sin128 = _rope_tables(pos)
    ret_c = _ret_consts()
    w = _prep_weights(w_ada, w_in, w_uq, w_ukv, w_a, w_b, w_o, w_pq, peer_k1, peer_k2, peer_u, peer_v)
    zero_state = jnp.zeros((1, bp, RET_HEADS, RET_QK_DIM, RET_V_DIM), F32)
    silu_c = jax.nn.silu(c_all)
    gspec = lambda tm, tn: pl.BlockSpec((tm // GROUP, 1, tn), lambda i, j: (i, 0, j))
    tile = lambda tm, tn, off=0: pl.BlockSpec((tm, tn), lambda i, j: (i, j + off))
    rope_spec = lambda tm: pl.BlockSpec((tm, LANES), lambda i, j: (i, 0))

    outs = {k: [] for k in ("ckv_p", "kr_p", "st_p", "ckv_s", "kr_s", "st_s")}
    for l in range(depth):
        ada = matmul(silu_c, w["w_ada"], F32, wl=l, name="ada") + b_ada[l][None, :]
        cond = ada.reshape(bp + bs, 6, d)[gidx]
        cg = [cond[:, k, :][:, None, :] for k in range(6)]

        h = modnorm(x, g_mix[l], cg[1], cg[0])
        cqn, ckvn, krr, kvin = inproj1(h, w["w1"], l, g_q[l], g_kv[l], cos64, sa64, sb64)
        big = matmul(h, w["w2"], BF16, tm=1024, tn=1024, wl=l, name="inproj2")
        tmq = 512
        q_cat = matmul(cqn, w["uq"], BF16, tm=tmq, tn=1024, epilogue=_ep_qrope, wl=l,
                       extras=(cos64, sa64, sb64), extra_specs=[rope_spec(tmq)] * 3, name="q_up")
        kv = matmul(kvin, w["wkv"], BF16, tm=1024, tn=1024, wl=l, m=n_p, name="kv_up")
        o_a = attn_prompt(q_cat, kv, bp, s_len)
        o_a = attn_sample(q_cat, cache_ckv, cache_krope, kvin, w["wuk_t"], w["wuv"], l, o_a, n_p, t_s)
        o_b, st_p = retention(big, cos128, sin128, zero_state, 0, ret_c, 0, bp, s_len, min(8, s_len // CHUNK), 2)
        o_b, st_s = retention(big, cos128, sin128, state_ret, l, ret_c, n_p, bs, t_s, 1, RET_HEADS, prev=o_b)
        mixed = proj_ab(o_a, o_b, w["w_a"], w["w_b"], l, big, 2 * RET_HEADS * RET_QK_DIM + 2 * RET_HEADS * RET_V_DIM)
        tm, tn = 1024, 1024
        x = matmul(mixed, w["w_o"], F32, tm=tm, tn=tn, epilogue=_ep_resid, wl=l,
                   extras=(x, cg[2]), extra_specs=[tile(tm, tn), gspec(tm, tn)], name="proj_o")

        outs["ckv_p"].append(ckvn[:n_p].reshape(bp, s_len, KV_LORA))
        outs["kr_p"].append(krr[:n_p, :ROPE_DIM].reshape(bp, s_len, ROPE_DIM))
        outs["st_p"].append(st_p)
        outs["ckv_s"].append(ckvn[n_p:].reshape(bs, t_s, KV_LORA))
        outs["kr_s"].append(krr[n_p:, :ROPE_DIM].reshape(bs, t_s, ROPE_DIM))
        outs["st_s"].append(st_s)

        h2t = modnorm(x, g_ffn[l], cg[4], cg[3], transposed=True)
        qt = matmul(w["w_pq_t"], h2t, F32, tm=2048, tn=512, xl=l, name="peer_q")
        thr, e1, s2, e2 = peer_route(qt, w["k1"], w["k2"], l)
        x = peer_experts(h2t, w["u"], w["v_t"], l, thr, e1, s2, e2, x, cg[5])

    y_p = final_rmsnorm(x, g_final, 0, n_p)
    y_s = final_rmsnorm(x, g_final, n_p, n_s)
    st = lambda k: jnp.stack(outs[k], axis=0)
    return (y_p.reshape(bp, s_len, d), y_s.reshape(bs, t_s, d),
            st("ckv_p"), st("kr_p"), st("st_p"), st("ckv_s"), st("kr_s"), st("st_s"))
```

```python
import functools

import numpy as np
import jax
import jax.numpy as jnp
from jax import lax
from jax.experimental import pallas as pl
from jax.experimental.pallas import tpu as pltpu

F32 = jnp.float32
BF16 = jnp.bfloat16

D_MODEL = 2048
CHUNK = 64
EPS = 1e-6
ROPE_THETA = 10000.0
MLA_HEADS = 16
Q_LORA = 512
KV_LORA = 512
NOPE_DIM = 128
ROPE_DIM = 64
V_DIM = 128
MLA_SCALE = (NOPE_DIM + ROPE_DIM) ** -0.5
QK_PAD = 256
KV_IN = KV_LORA + 128
RET_HEADS = 8
RET_QK_DIM = 128
RET_V_DIM = 256
PEER_HEADS = 8
N_KEYS = 128
PEER_HALF = 128
PEER_TOPK = 16
GROUP = 64
LANES = 128
SUBLANES = 8

VMEM_LIMIT = 60 * 1024 * 1024


def _cparams(*sem):
    return pltpu.CompilerParams(dimension_semantics=sem, vmem_limit_bytes=VMEM_LIMIT)


def _modnorm_kernel(x_ref, g_ref, sc_ref, sh_ref, o_ref, *, transposed):
    x = x_ref[...]
    tr, d = x.shape
    y = x * lax.rsqrt(jnp.mean(x * x, axis=-1, keepdims=True) + EPS) * g_ref[...]
    y = (y.reshape(tr // GROUP, GROUP, d) * (1.0 + sc_ref[...]) + sh_ref[...]).reshape(tr, d)
    if transposed:
        y = jnp.transpose(y)
    o_ref[...] = y.astype(o_ref.dtype)


def modnorm(x, g, scale_g, shift_g, tr=512, transposed=False):
    n, d = x.shape
    tr = min(tr, n)
    gpt = tr // GROUP
    return pl.pallas_call(
        functools.partial(_modnorm_kernel, transposed=transposed),
        out_shape=jax.ShapeDtypeStruct((d, n) if transposed else (n, d), BF16),
        grid=(n // tr,),
        in_specs=[pl.BlockSpec((tr, d), lambda i: (i, 0)),
                  pl.BlockSpec((1, d), lambda i: (0, 0)),
                  pl.BlockSpec((gpt, 1, d), lambda i: (i, 0, 0)),
                  pl.BlockSpec((gpt, 1, d), lambda i: (i, 0, 0))],
        out_specs=(pl.BlockSpec((d, tr), lambda i: (0, i)) if transposed
                   else pl.BlockSpec((tr, d), lambda i: (i, 0))),
        compiler_params=_cparams("parallel"),
        name="modnorm",
    )(x, g.reshape(1, d), scale_g, shift_g)


def _rmsnorm_kernel(x_ref, g_ref, o_ref):
    x = x_ref[...]
    o_ref[...] = x * lax.rsqrt(jnp.mean(x * x, axis=-1, keepdims=True) + EPS) * g_ref[...]


def final_rmsnorm(x, g, row0, nrows, tr=512):
    d = x.shape[1]
    tr = min(tr, nrows)
    rb = row0 // tr
    return pl.pallas_call(
        _rmsnorm_kernel,
        out_shape=jax.ShapeDtypeStruct((nrows, d), F32),
        grid=(nrows // tr,),
        in_specs=[pl.BlockSpec((tr, d), lambda i: (rb + i, 0)),
                  pl.BlockSpec((1, d), lambda i: (0, 0))],
        out_specs=pl.BlockSpec((tr, d), lambda i: (i, 0)),
        compiler_params=_cparams("parallel"),
        name="final_rmsnorm",
    )(x, g.reshape(1, d))


def _mm_kernel(x_ref, w_ref, *rest, epilogue):
    extras, o_ref = rest[:-1], rest[-1]
    acc = jnp.dot(x_ref[...].astype(BF16), w_ref[...].astype(BF16), preferred_element_type=F32)
    if epilogue is not None:
        acc = epilogue(acc, *[e[...] for e in extras])
    o_ref[...] = acc.astype(o_ref.dtype)


def matmul(x, w, out_dtype, tm=1024, tn=512, epilogue=None, extras=(), extra_specs=(), name="matmul",
           xl=None, wl=None, m=None):
    k = x.shape[-1]
    m = x.shape[-2] if m is None else m
    n = w.shape[-1]
    tm, tn = min(tm, m), min(tn, n)
    assert m % tm == 0 and n % tn == 0, (m, tm, n, tn)
    xspec = (pl.BlockSpec((tm, k), lambda i, j: (i, 0)) if xl is None
             else pl.BlockSpec((None, tm, k), lambda i, j: (xl, i, 0)))
    wspec = (pl.BlockSpec((k, tn), lambda i, j: (0, j)) if wl is None
             else pl.BlockSpec((None, k, tn), lambda i, j: (wl, 0, j)))
    return pl.pallas_call(
        functools.partial(_mm_kernel, epilogue=epilogue),
        out_shape=jax.ShapeDtypeStruct((m, n), out_dtype),
        grid=(m // tm, n // tn),
        in_specs=[xspec, wspec] + list(extra_specs),
        out_specs=pl.BlockSpec((tm, tn), lambda i, j: (i, j)),
        compiler_params=_cparams("parallel", "arbitrary"),
        name=name,
    )(x, w, *extras)


def _rope_half(x, cos, sa, sb):
    return x * cos + pltpu.roll(x, 96, 1) * sa + pltpu.roll(x, 32, 1) * sb


def _ep_qrope(acc, cos, sa, sb):
    parts = []
    for h in range(acc.shape[1] // QK_PAD):
        parts.append(acc[:, h * QK_PAD:h * QK_PAD + NOPE_DIM])
        parts.append(_rope_half(acc[:, h * QK_PAD + NOPE_DIM:(h + 1) * QK_PAD], cos, sa, sb))
    return jnp.concatenate(parts, axis=1)


def _proj_ab_kernel(oa_ref, ob_ref, wa_ref, wb_ref, ga_ref, gb_ref, o_ref):
    ya = jnp.dot(oa_ref[...], wa_ref[...], preferred_element_type=F32)
    yb = jnp.dot(ob_ref[...], wb_ref[...], preferred_element_type=F32)
    o_ref[...] = (jax.nn.sigmoid(ga_ref[...].astype(F32)) * ya
                  + jax.nn.sigmoid(gb_ref[...].astype(F32)) * yb).astype(o_ref.dtype)


def proj_ab(o_a, o_b, w_a, w_b, l, big, gate_col0, tm=1024, tn=512):
    m, k = o_a.shape
    n = w_a.shape[-1]
    tm = min(tm, m)
    goff = gate_col0 // tn
    xspec = pl.BlockSpec((tm, k), lambda i, j: (i, 0))
    wspec = pl.BlockSpec((None, k, tn), lambda i, j: (l, 0, j))
    return pl.pallas_call(
        _proj_ab_kernel,
        out_shape=jax.ShapeDtypeStruct((m, n), BF16),
        grid=(m // tm, n // tn),
        in_specs=[xspec, xspec, wspec, wspec,
                  pl.BlockSpec((tm, tn), lambda i, j: (i, goff + j)),
                  pl.BlockSpec((tm, tn), lambda i, j: (i, goff + n // tn + j))],
        out_specs=pl.BlockSpec((tm, tn), lambda i, j: (i, j)),
        compiler_params=_cparams("parallel", "arbitrary"),
        name="proj_ab",
    )(o_a, o_b, w_a, w_b, big, big)


def _ep_resid(acc, x, gate):
    tm, tn = acc.shape
    y = acc.reshape(tm // GROUP, GROUP, tn) * gate
    return x + y.reshape(tm, tn)


def _inproj1_kernel(h_ref, w_ref, gq_ref, gkv_ref, cos_ref, sa_ref, sb_ref,
                    cq_ref, ckv_ref, kr_ref, kvin_ref):
    acc = jnp.dot(h_ref[...], w_ref[...], preferred_element_type=F32)
    cq = acc[:, :Q_LORA]
    ckv = acc[:, Q_LORA:Q_LORA + KV_LORA]
    kr = acc[:, Q_LORA + KV_LORA:]
    cqn = cq * lax.rsqrt(jnp.mean(cq * cq, axis=-1, keepdims=True) + EPS) * gq_ref[...]
    ckvn = ckv * lax.rsqrt(jnp.mean(ckv * ckv, axis=-1, keepdims=True) + EPS) * gkv_ref[...]
    krr = _rope_half(kr, cos_ref[...], sa_ref[...], sb_ref[...])
    cq_ref[...] = cqn.astype(BF16)
    ckv_ref[...] = ckvn
    kr_ref[...] = krr
    kvin_ref[:, :KV_LORA] = ckvn.astype(BF16)
    kvin_ref[:, KV_LORA:] = krr.astype(BF16)


def inproj1(h, w1, l, g_q, g_kv, cos, sa, sb, tm=512):
    n, k = h.shape
    tm = min(tm, n)
    w1n = w1.shape[-1]
    row = lambda i: (i, 0)
    fix = lambda i: (0, 0)
    return pl.pallas_call(
        _inproj1_kernel,
        out_shape=(jax.ShapeDtypeStruct((n, Q_LORA), BF16),
                   jax.ShapeDtypeStruct((n, KV_LORA), F32),
                   jax.ShapeDtypeStruct((n, LANES), F32),
                   jax.ShapeDtypeStruct((n, KV_IN), BF16)),
        grid=(n // tm,),
        in_specs=[pl.BlockSpec((tm, k), row), pl.BlockSpec((None, k, w1n), lambda i: (l, 0, 0)),
                  pl.BlockSpec((1, Q_LORA), fix), pl.BlockSpec((1, KV_LORA), fix),
                  pl.BlockSpec((tm, LANES), row), pl.BlockSpec((tm, LANES), row), pl.BlockSpec((tm, LANES), row)],
        out_specs=(pl.BlockSpec((tm, Q_LORA), row), pl.BlockSpec((tm, KV_LORA), row),
                   pl.BlockSpec((tm, LANES), row), pl.BlockSpec((tm, KV_IN), row)),
        compiler_params=_cparams("parallel"),
        name="inproj1",
    )(h, w1, g_q.reshape(1, -1), g_kv.reshape(1, -1), cos, sa, sb)


def _attn_prompt_kernel(q_ref, k_ref, v_ref, o_ref, *, tq, tk, nh):
    qi = pl.program_id(2)
    nt = (((1,), (1,)), ((), ()))
    qs = [q_ref[:, h * QK_PAD:(h + 1) * QK_PAD] for h in range(nh)]

    def scores(start, size, h):
        k = k_ref[pl.ds(start, size), h * QK_PAD:(h + 1) * QK_PAD]
        v = v_ref[pl.ds(start, size), h * V_DIM:(h + 1) * V_DIM]
        return lax.dot_general(qs[h], k, nt, preferred_element_type=F32), v

    def update(carry, s, v):
        m, l, acc = carry
        m_new = jnp.maximum(m, jnp.max(s, axis=-1, keepdims=True))
        alpha = jnp.exp(m - m_new)
        p = jnp.exp(s - m_new)
        l = alpha * l + jnp.sum(p, axis=-1, keepdims=True)
        acc = alpha * acc + jnp.dot(p.astype(BF16), v, preferred_element_type=F32)
        return m_new, l, acc

    def big_body(j, carry):
        start = pl.multiple_of(j * tk, tk)
        return tuple(update(carry[h], *scores(start, tk, h)) for h in range(nh))

    q0 = qi * tq
    nbig = q0 // tk

    def small_body(i, carry):
        start = pl.multiple_of(nbig * tk + i * tq, tq)
        return tuple(update(carry[h], *scores(start, tq, h)) for h in range(nh))

    init = tuple((jnp.full((tq, 1), -1e30, F32), jnp.zeros((tq, 1), F32), jnp.zeros((tq, V_DIM), F32))
                 for _ in range(nh))
    carry = lax.fori_loop(0, nbig, big_body, init)
    carry = lax.fori_loop(0, (q0 - nbig * tk) // tq, small_body, carry)
    rows = lax.broadcasted_iota(jnp.int32, (tq, tq), 0) // CHUNK
    cols = lax.broadcasted_iota(jnp.int32, (tq, tq), 1) // CHUNK
    for h in range(nh):
        s, v = scores(pl.multiple_of(q0, tq), tq, h)
        s = jnp.where(cols <= rows, s, -1e30)
        m, l, acc = update(carry[h], s, v)
        o_ref[:, h * V_DIM:(h + 1) * V_DIM] = (acc / l).astype(o_ref.dtype)


def attn_prompt(q_cat, kv, bp, s_len, tq=512, tk=1024, nh=2):
    tq = min(tq, s_len)
    tk = min(tk, s_len)
    nq = s_len // tq
    vcol0 = MLA_HEADS * QK_PAD // (V_DIM * nh)
    return pl.pallas_call(
        functools.partial(_attn_prompt_kernel, tq=tq, tk=tk, nh=nh),
        out_shape=jax.ShapeDtypeStruct((q_cat.shape[0], MLA_HEADS * V_DIM), BF16),
        grid=(bp, MLA_HEADS // nh, nq),
        in_specs=[pl.BlockSpec((tq, QK_PAD * nh), lambda b, h, i: (b * nq + i, h)),
                  pl.BlockSpec((s_len, QK_PAD * nh), lambda b, h, i: (b, h)),
                  pl.BlockSpec((s_len, V_DIM * nh), lambda b, h, i: (b, vcol0 + h))],
        out_specs=pl.BlockSpec((tq, V_DIM * nh), lambda b, h, i: (b * nq + i, h)),
        compiler_params=_cparams("parallel", "parallel", "arbitrary"),
        name="attn_prompt",
    )(q_cat, kv, kv)


def _attn_sample_kernel(q_ref, ckv_ref, kr_ref, new_ref, wuk_ref, wuv_ref, prev_ref, o_ref,
                        qall_ref, kall_ref, olat_ref, *, t):
    del prev_ref
    nt = (((1,), (1,)), ((), ()))
    past = ckv_ref.shape[0]
    for h in range(MLA_HEADS):
        qn = q_ref[:, h * QK_PAD:h * QK_PAD + NOPE_DIM]
        qlat = jnp.dot(qn, wuk_ref[h], preferred_element_type=F32)
        qall_ref[h * t:(h + 1) * t, :KV_LORA] = qlat.astype(BF16)
        qall_ref[h * t:(h + 1) * t, KV_LORA:] = q_ref[:, h * QK_PAD + NOPE_DIM:(h + 1) * QK_PAD]
    kall_ref[:, :KV_LORA] = ckv_ref[...].astype(BF16)
    kall_ref[:, KV_LORA:KV_LORA + ROPE_DIM] = kr_ref[...].astype(BF16)
    kall_ref[:, KV_LORA + ROPE_DIM:] = jnp.zeros((past, KV_IN - KV_LORA - ROPE_DIM), BF16)
    qall = qall_ref[...]
    knew = new_ref[...]
    s_past = lax.dot_general(qall, kall_ref[...], nt, preferred_element_type=F32)
    s_new = lax.dot_general(qall, knew, nt, preferred_element_type=F32)
    m = jnp.maximum(jnp.max(s_past, axis=-1, keepdims=True), jnp.max(s_new, axis=-1, keepdims=True))
    p_past = jnp.exp(s_past - m)
    p_new = jnp.exp(s_new - m)
    l = jnp.sum(p_past, axis=-1, keepdims=True) + jnp.sum(p_new, axis=-1, keepdims=True)
    olat = (jnp.dot(p_past.astype(BF16), kall_ref[:, :KV_LORA], preferred_element_type=F32)
            + jnp.dot(p_new.astype(BF16), knew[:, :KV_LORA], preferred_element_type=F32))
    olat_ref[...] = (olat / l).astype(BF16)
    for h in range(MLA_HEADS):
        o = jnp.dot(olat_ref[h * t:(h + 1) * t, :], wuv_ref[h], preferred_element_type=F32)
        o_ref[:, h * V_DIM:(h + 1) * V_DIM] = o.astype(o_ref.dtype)


def attn_sample(q_cat, ckv_past, kr_past, kvin, wuk_t, wuv, l, prev, row0, t):
    _, bs, past, _ = ckv_past.shape
    rb = row0 // t
    return pl.pallas_call(
        functools.partial(_attn_sample_kernel, t=t),
        out_shape=jax.ShapeDtypeStruct(prev.shape, prev.dtype),
        grid=(bs,),
        in_specs=[pl.BlockSpec((t, MLA_HEADS * QK_PAD), lambda b: (rb + b, 0)),
                  pl.BlockSpec((None, None, past, KV_LORA), lambda b: (l, b, 0, 0)),
                  pl.BlockSpec((None, None, past, ROPE_DIM), lambda b: (l, b, 0, 0)),
                  pl.BlockSpec((t, KV_IN), lambda b: (rb + b, 0)),
                  pl.BlockSpec((None, MLA_HEADS, NOPE_DIM, KV_LORA), lambda b: (l, 0, 0, 0)),
                  pl.BlockSpec((None, MLA_HEADS, KV_LORA, V_DIM), lambda b: (l, 0, 0, 0)),
                  pl.BlockSpec(memory_space=pl.ANY)],
        out_specs=pl.BlockSpec((t, MLA_HEADS * V_DIM), lambda b: (rb + b, 0)),
        scratch_shapes=[pltpu.VMEM((MLA_HEADS * t, KV_IN), BF16),
                        pltpu.VMEM((past, KV_IN), BF16),
                        pltpu.VMEM((MLA_HEADS * t, KV_LORA), BF16)],
        input_output_aliases={6: 0},
        compiler_params=_cparams("parallel"),
        name="attn_sample",
    )(q_cat, ckv_past, kr_past, kvin, wuk_t, wuv, prev)


def _ret_consts():
    lg = np.log1p(-(2.0 ** (-5.0 - np.arange(RET_HEADS, dtype=np.float64))))
    idx = np.arange(CHUNK, dtype=np.float64)
    diff = idx[:, None] - idx[None, :]
    dmask = np.where(diff[None] >= 0, np.exp(np.maximum(diff, 0.0)[None] * lg[:, None, None]), 0.0)
    qd = np.exp((idx + 1.0)[None, :] * lg[:, None])
    kd = np.exp((CHUNK - 1.0 - idx)[None, :] * lg[:, None])
    g = np.exp(CHUNK * lg)
    qd = np.broadcast_to(qd[:, :, None], (RET_HEADS, CHUNK, RET_QK_DIM))
    kd = np.broadcast_to(kd[:, :, None], (RET_HEADS, CHUNK, RET_QK_DIM))
    g = np.broadcast_to(g[:, None, None], (RET_HEADS, 1, RET_V_DIM))
    return (jnp.asarray(dmask, F32), jnp.asarray(qd, F32), jnp.asarray(kd, F32), jnp.asarray(g, F32))


def _retention_kernel(q_ref, k_ref, v_ref, g_ref, cos_ref, sin_ref, st0_ref, dm_ref, qd_ref, kd_ref,
                      gam_ref, *rest, cps, hps):
    o_ref, st_ref, state = rest[-3:]
    step = pl.program_id(2)
    nt = (((1,), (1,)), ((), ()))

    @pl.when(step == 0)
    def _():
        state[...] = st0_ref[...]

    for c in range(cps):
        r = slice(c * CHUNK, (c + 1) * CHUNK)
        cos, sin = cos_ref[r, :], sin_ref[r, :]
        for hh in range(hps):
            qk = slice(hh * RET_QK_DIM, (hh + 1) * RET_QK_DIM)
            vv = slice(hh * RET_V_DIM, (hh + 1) * RET_V_DIM)
            qf = q_ref[r, qk].astype(F32)
            kf = k_ref[r, qk].astype(F32)
            q = qf * cos + pltpu.roll(qf, 64, 1) * sin
            k = (kf * cos + pltpu.roll(kf, 64, 1) * sin) * (RET_QK_DIM ** -0.5)
            v = v_ref[r, vv]
            st = state[hh]
            inner = lax.dot_general(q.astype(BF16), k.astype(BF16), nt, preferred_element_type=F32) * dm_ref[hh]
            o = (jnp.dot(inner.astype(BF16), v, preferred_element_type=F32)
                 + jnp.dot((q * qd_ref[hh]).astype(BF16), st.astype(BF16), preferred_element_type=F32))
            kt = jnp.transpose(k * kd_ref[hh]).astype(BF16)
            state[hh] = st * gam_ref[hh] + jnp.dot(kt, v, preferred_element_type=F32)
            mu = jnp.mean(o, axis=-1, keepdims=True)
            oc = o - mu
            var = jnp.mean(oc * oc, axis=-1, keepdims=True)
            gate = g_ref[r, vv].astype(F32)
            o_ref[r, vv] = (gate * jax.nn.sigmoid(gate) * (oc * lax.rsqrt(var + EPS))).astype(o_ref.dtype)

    @pl.when(step == pl.num_programs(2) - 1)
    def _():
        st_ref[0] = state[...]


def retention(big, cos, sin, state0, l, consts, row0, nseq, seq_len, cps, hps, prev=None):
    r = cps * CHUNK
    steps = seq_len // r
    rb0 = row0 // r
    ng = RET_HEADS // hps
    dmask, qd, kd, gam = consts
    rowblk = lambda s, h, t: rb0 + s * steps + t
    hconst = lambda s, h, t: (h, 0, 0)
    in_specs = [pl.BlockSpec((r, hps * RET_QK_DIM), lambda s, h, t: (rowblk(s, h, t), h)),
                pl.BlockSpec((r, hps * RET_QK_DIM), lambda s, h, t: (rowblk(s, h, t), ng + h)),
                pl.BlockSpec((r, hps * RET_V_DIM), lambda s, h, t: (rowblk(s, h, t), ng + h)),
                pl.BlockSpec((r, hps * RET_V_DIM), lambda s, h, t: (rowblk(s, h, t), 2 * ng + h)),
                pl.BlockSpec((r, RET_QK_DIM), lambda s, h, t: (rowblk(s, h, t), 0)),
                pl.BlockSpec((r, RET_QK_DIM), lambda s, h, t: (rowblk(s, h, t), 0)),
                pl.BlockSpec((None, None, hps, RET_QK_DIM, RET_V_DIM), lambda s, h, t: (l, s, h, 0, 0)),
                pl.BlockSpec((hps, CHUNK, CHUNK), hconst),
                pl.BlockSpec((hps, CHUNK, RET_QK_DIM), hconst),
                pl.BlockSpec((hps, CHUNK, RET_QK_DIM), hconst),
                pl.BlockSpec((hps, 1, RET_V_DIM), hconst)]
    args = [big, big, big, big, cos, sin, state0, dmask, qd, kd, gam]
    aliases = {}
    if prev is not None:
        in_specs.append(pl.BlockSpec(memory_space=pl.ANY))
        args.append(prev)
        aliases = {len(args) - 1: 0}
    return pl.pallas_call(
        functools.partial(_retention_kernel, cps=cps, hps=hps),
        out_shape=(jax.ShapeDtypeStruct((big.shape[0], RET_HEADS * RET_V_DIM), BF16),
                   jax.ShapeDtypeStruct((nseq, RET_HEADS, RET_QK_DIM, RET_V_DIM), F32)),
        grid=(nseq, ng, steps),
        in_specs=in_specs,
        out_specs=(pl.BlockSpec((r, hps * RET_V_DIM), lambda s, h, t: (rowblk(s, h, t), h)),
                   pl.BlockSpec((1, hps, RET_QK_DIM, RET_V_DIM), lambda s, h, t: (s, h, 0, 0))),
        scratch_shapes=[pltpu.VMEM((hps, RET_QK_DIM, RET_V_DIM), F32)],
        input_output_aliases=aliases,
        compiler_params=_cparams("parallel", "parallel", "arbitrary"),
        name="retention",
    )(*args)


_CAND = [(a, b) for a in range(PEER_TOPK) for b in range(PEER_TOPK) if (a + 1) * (b + 1) <= PEER_TOPK]


def _top_distinct(s):
    vals = []
    for _ in range(PEER_TOPK):
        m = jnp.max(s, axis=0, keepdims=True)
        vals.append(m)
        s = jnp.where(s == m, -jnp.inf, s)
    return vals


def _count_eq(s, v):
    return jnp.sum(jnp.where(s == v, 1.0, 0.0), axis=0, keepdims=True)


def _route_kernel(qt_ref, k1_ref, k2_ref, thr_ref, e1_ref, s2_ref, e2_ref, cand_ref, mult_ref, cnt_ref):
    tt = qt_ref.shape[1]
    pad = cand_ref.shape[0] - len(_CAND)
    cand_ref[len(_CAND):, :] = jnp.full((pad, LANES), -jnp.inf, F32)
    mult_ref[len(_CAND):, :] = jnp.zeros((pad, LANES), F32)
    for h, c in [(h, c) for h in range(PEER_HEADS) for c in range(tt // LANES)]:
        lanes = slice(c * LANES, (c + 1) * LANES)
        q1 = qt_ref[h * 2 * PEER_HALF:h * 2 * PEER_HALF + PEER_HALF, lanes].astype(BF16)
        q2 = qt_ref[h * 2 * PEER_HALF + PEER_HALF:(h + 1) * 2 * PEER_HALF, lanes].astype(BF16)
        s1 = jnp.dot(k1_ref[...], q1, preferred_element_type=F32)
        s2 = jnp.dot(k2_ref[...], q2, preferred_element_type=F32)
        v1 = _top_distinct(s1)
        v2 = _top_distinct(s2)
        cnt_ref[...] = jnp.ones(cnt_ref.shape, F32)
        n_top = jnp.maximum(jnp.sum(jnp.where(s1 >= v1[-1], 1.0, 0.0), axis=0, keepdims=True),
                            jnp.sum(jnp.where(s2 >= v2[-1], 1.0, 0.0), axis=0, keepdims=True))

        @pl.when(jnp.max(n_top) > PEER_TOPK)
        def _():
            for a in range(PEER_TOPK):
                cnt_ref[a:a + 1, :] = _count_eq(s1, v1[a])
                cnt_ref[PEER_TOPK + a:PEER_TOPK + a + 1, :] = _count_eq(s2, v2[a])

        for r, (a, b) in enumerate(_CAND):
            cand_ref[r:r + 1, :] = v1[a] + v2[b]
            mult_ref[r:r + 1, :] = cnt_ref[a:a + 1, :] * cnt_ref[PEER_TOPK + b:PEER_TOPK + b + 1, :]
        cand = cand_ref[...]
        mult = mult_ref[...]
        rest = cand
        cum = jnp.zeros_like(v1[0])
        tau = v1[0] + v2[0]
        for _ in range(PEER_TOPK):
            m = jnp.max(rest, axis=0, keepdims=True)
            hit = rest == m
            tau = jnp.where(cum < PEER_TOPK, m, tau)
            cum = cum + jnp.sum(jnp.where(hit, mult, 0.0), axis=0, keepdims=True)
            rest = jnp.where(hit, -jnp.inf, rest)
        top = v1[0] + v2[0]
        z = jnp.sum(jnp.where(cand >= tau, mult * jnp.exp(cand - top), 0.0), axis=0, keepdims=True)
        thr = jnp.full_like(s1, jnp.inf)
        for a in range(PEER_TOPK):
            th_a = jnp.full_like(tau, jnp.inf)
            for b in range(PEER_TOPK // (a + 1)):
                th_a = jnp.where(v1[a] + v2[b] >= tau, v2[b], th_a)
            thr = jnp.where(s1 == v1[a], th_a, thr)
        thr_ref[h, :, lanes] = thr
        s2_ref[h, :, lanes] = s2
        e1_ref[h, :, lanes] = 0.5 * jnp.exp(s1 - v1[0])
        e2_ref[h, :, lanes] = jnp.exp(s2 - v2[0]) / z


def peer_route(qt, k1, k2, l, tt=256):
    n = qt.shape[1]
    tt = min(tt, n)
    big = jax.ShapeDtypeStruct((PEER_HEADS, N_KEYS, n), F32)
    bspec = pl.BlockSpec((PEER_HEADS, N_KEYS, tt), lambda i: (0, 0, i))
    return pl.pallas_call(
        _route_kernel,
        out_shape=(big, big, big, big),
        grid=(n // tt,),
        in_specs=[pl.BlockSpec((PEER_HEADS * 2 * PEER_HALF, tt), lambda i: (0, i)),
                  pl.BlockSpec((None, N_KEYS, PEER_HALF), lambda i: (l, 0, 0)),
                  pl.BlockSpec((None, N_KEYS, PEER_HALF), lambda i: (l, 0, 0))],
        out_specs=(bspec, bspec, bspec, bspec),
        scratch_shapes=[pltpu.VMEM((-(-len(_CAND) // SUBLANES) * SUBLANES, LANES), F32)] * 2
        + [pltpu.VMEM((2 * PEER_TOPK, LANES), F32)],
        compiler_params=_cparams("parallel"),
        name="peer_route",
    )(qt, k1, k2)


def _gate_rows(ii, thr_ref, e1_ref, s2_ref, e2_ref):
    w = None
    for h in range(PEER_HEADS):
        c = jnp.where(s2_ref[h] >= thr_ref[h, ii:ii + 1, :], e2_ref[h], 0.0) * e1_ref[h, ii:ii + 1, :]
        w = c if w is None else w + c
    return w


def _peer_expert_kernel(ht_ref, u_ref, vt_ref, thc_ref, e1c_ref, thn_ref, e1n_ref, s2_ref, e2_ref,
                        x_ref, gate_ref, o_ref, acc_ref, wg_ref, w_ref, *, ipt):
    e = pl.program_id(1)
    last = pl.num_programs(1) - 1

    def build(thr_ref, e1_ref, slot):
        for ii in range(ipt):
            w_ref[slot, ii * N_KEYS:(ii + 1) * N_KEYS, :] = _gate_rows(ii, thr_ref, e1_ref, s2_ref, e2_ref)

    @pl.when(e == 0)
    def _():
        acc_ref[...] = jnp.zeros_like(acc_ref)
        build(thc_ref, e1c_ref, 0)

    build(thn_ref, e1n_ref, (e + 1) % 2)
    act = jnp.dot(u_ref[...], ht_ref[...], preferred_element_type=F32)
    gelu2 = act * (1.0 + lax.erf(act * (2.0 ** -0.5)))
    wg_ref[...] = (w_ref[e % 2] * gelu2).astype(BF16)
    acc_ref[...] += jnp.dot(vt_ref[...], wg_ref[...], preferred_element_type=F32)

    @pl.when(e == last)
    def _():
        tt, d = o_ref.shape
        y = jnp.transpose(acc_ref[...]).reshape(tt // GROUP, GROUP, d) * gate_ref[...]
        o_ref[...] = x_ref[...] + y.reshape(tt, d)


def peer_experts(ht, u, vt, l, thr, e1, s2, e2, x, gate_g, tt=512, ipt=8):
    d, n = ht.shape
    ne = u.shape[1]
    tt = min(tt, n)
    te = ipt * N_KEYS
    nlast = ne // te - 1
    full = pl.BlockSpec((PEER_HEADS, N_KEYS, tt), lambda t, e: (0, 0, t))
    cur = pl.BlockSpec((PEER_HEADS, ipt, tt), lambda t, e: (0, e, t))
    nxt = pl.BlockSpec((PEER_HEADS, ipt, tt), lambda t, e: (0, jnp.minimum(e + 1, nlast), t))
    return pl.pallas_call(
        functools.partial(_peer_expert_kernel, ipt=ipt),
        out_shape=jax.ShapeDtypeStruct((n, d), F32),
        grid=(n // tt, ne // te),
        in_specs=[pl.BlockSpec((d, tt), lambda t, e: (0, t)),
                  pl.BlockSpec((None, te, d), lambda t, e: (l, e, 0)),
                  pl.BlockSpec((None, d, te), lambda t, e: (l, 0, e)),
                  cur, cur, nxt, nxt, full, full,
                  pl.BlockSpec((tt, d), lambda t, e: (t, 0), pipeline_mode=pl.Buffered(1)),
                  pl.BlockSpec((tt // GROUP, 1, d), lambda t, e: (t, 0, 0))],
        out_specs=pl.BlockSpec((tt, d), lambda t, e: (t, 0)),
        scratch_shapes=[pltpu.VMEM((d, tt), F32), pltpu.VMEM((te, tt), BF16), pltpu.VMEM((2, te, tt), F32)],
        compiler_params=_cparams("parallel", "arbitrary"),
        name="peer_experts",
    )(ht, u, vt, thr, e1, thr, e1, s2, e2, x, gate_g)


def _rope_tables(pos):
    pos = pos.astype(F32)[:, None]
    half = ROPE_DIM // 2
    inv = ROPE_THETA ** (-jnp.arange(half, dtype=F32) * 2.0 / ROPE_DIM)
    ang = pos * inv[None, :]
    c, s = jnp.cos(ang), jnp.sin(ang)
    z32 = jnp.zeros_like(c)
    z64 = jnp.concatenate([z32, z32], axis=1)
    cos64 = jnp.concatenate([c, c, z64], axis=1)
    sa64 = jnp.concatenate([-s, z32, z64], axis=1)
    sb64 = jnp.concatenate([z32, s, z64], axis=1)
    half = RET_QK_DIM // 2
    inv = ROPE_THETA ** (-jnp.arange(half, dtype=F32) * 2.0 / RET_QK_DIM)
    ang = pos * inv[None, :]
    c, s = jnp.cos(ang), jnp.sin(ang)
    return cos64, sa64, sb64, jnp.concatenate([c, c], axis=1), jnp.concatenate([-s, s], axis=1)


def _prep_weights(w_ada, w_in, w_uq, w_ukv, w_a, w_b, w_o, w_pq, k1, k2, u, v):
    nl = w_in.shape[0]
    o1 = Q_LORA + KV_LORA + ROPE_DIM
    w1 = jnp.pad(w_in[:, :, :o1], ((0, 0), (0, 0), (0, LANES - ROPE_DIM))).astype(BF16)
    w2 = w_in[:, :, o1:].astype(BF16)
    uq = w_uq.reshape(nl, Q_LORA, MLA_HEADS, NOPE_DIM + ROPE_DIM) * MLA_SCALE
    uq = jnp.pad(uq, ((0, 0), (0, 0), (0, 0), (0, QK_PAD - NOPE_DIM - ROPE_DIM)))
    uq = uq.reshape(nl, Q_LORA, MLA_HEADS * QK_PAD).astype(BF16)
    wk = jnp.pad(w_ukv[..., :NOPE_DIM], ((0, 0), (0, KV_IN - KV_LORA), (0, 0), (0, QK_PAD - NOPE_DIM)))
    eye = jnp.zeros((KV_IN, QK_PAD), F32).at[KV_LORA + jnp.arange(ROPE_DIM), NOPE_DIM + jnp.arange(ROPE_DIM)].set(1.0)
    wk = (wk + eye[None, :, None, :]).reshape(nl, KV_IN, MLA_HEADS * QK_PAD)
    wv = jnp.pad(w_ukv[..., NOPE_DIM:], ((0, 0), (0, KV_IN - KV_LORA), (0, 0), (0, 0)))
    wv = wv.reshape(nl, KV_IN, MLA_HEADS * V_DIM)
    wkv = jnp.concatenate([wk, wv], axis=2).astype(BF16)
    wuk_t = jnp.transpose(w_ukv[..., :NOPE_DIM], (0, 2, 3, 1)).astype(BF16)
    wuv = jnp.transpose(w_ukv[..., NOPE_DIM:], (0, 2, 1, 3)).astype(BF16)
    return dict(w_ada=w_ada, w1=w1, w2=w2, uq=uq, wkv=wkv, wuk_t=wuk_t, wuv=wuv,
                w_a=w_a.astype(BF16), w_b=w_b.astype(BF16), w_o=w_o.astype(BF16),
                w_pq_t=jnp.swapaxes(w_pq, 1, 2).astype(BF16), k1=k1.astype(BF16), k2=k2.astype(BF16),
                u=u.astype(BF16), v_t=jnp.swapaxes(v, 1, 2).astype(BF16))


def kernel(x_prompt, x_sample, c_prompt, c_sample, cache_ckv, cache_krope, state_ret, w_ada, b_ada, g_mix, g_ffn, w_in, g_q, w_uq, g_kv, w_ukv, w_a, w_b, w_o, w_pq, peer_k1, peer_k2, peer_u, peer_v, g_final):
    bp, s_len, d = x_prompt.shape
    bs, t_s, _ = x_sample.shape
    depth = w_in.shape[0]
    past = cache_ckv.shape[2]
    n_p, n_s = bp * s_len, bs * t_s
    n = n_p + n_s
    assert t_s == GROUP and s_len % GROUP == 0

    x = jnp.concatenate([x_prompt.reshape(n_p, d), x_sample.reshape(n_s, d)], axis=0)
    c_all = jnp.concatenate([c_prompt, c_sample], axis=0)
    gidx = jnp.concatenate([jnp.repeat(jnp.arange(bp), s_len // GROUP), bp + jnp.arange(bs)])
    pos = jnp.concatenate([jnp.tile(jnp.arange(s_len), bp), jnp.tile(past + jnp.arange(t_s), bs)])
    cos64, sa64, sb64, cos128, sin128 = _rope_tables(pos)
    ret_c = _ret_consts()
    w = _prep_weights(w_ada, w_in, w_uq, w_ukv, w_a, w_b, w_o, w_pq, peer_k1, peer_k2, peer_u, peer_v)
    zero_state = jnp.zeros((1, bp, RET_HEADS, RET_QK_DIM, RET_V_DIM), F32)
    silu_c = jax.nn.silu(c_all)
    gspec = lambda tm, tn: pl.BlockSpec((tm // GROUP, 1, tn), lambda i, j: (i, 0, j))
    tile = lambda tm, tn, off=0: pl.BlockSpec((tm, tn), lambda i, j: (i, j + off))
    rope_spec = lambda tm: pl.BlockSpec((tm, LANES), lambda i, j: (i, 0))

    outs = {k: [] for k in ("ckv_p", "kr_p", "st_p", "ckv_s", "kr_s", "st_s")}
    for l in range(depth):
        ada = matmul(silu_c, w["w_ada"], F32, wl=l, name="ada") + b_ada[l][None, :]
        cond = ada.reshape(bp + bs, 6, d)[gidx]
        cg = [cond[:, k, :][:, None, :] for k in range(6)]

        h = modnorm(x, g_mix[l], cg[1], cg[0])
        cqn, ckvn, krr, kvin = inproj1(h, w["w1"], l, g_q[l], g_kv[l], cos64, sa64, sb64)
        big = matmul(h, w["w2"], BF16, tm=1024, tn=1024, wl=l, name="inproj2")
        tmq = 1024
        q_cat = matmul(cqn, w["uq"], BF16, tm=tmq, tn=2048, epilogue=_ep_qrope, wl=l,
                       extras=(cos64, sa64, sb64), extra_specs=[rope_spec(tmq)] * 3, name="q_up")
        kv = matmul(kvin, w["wkv"], BF16, tm=1024, tn=2048, wl=l, m=n_p, name="kv_up")
        o_a = attn_prompt(q_cat, kv, bp, s_len)
        o_a = attn_sample(q_cat, cache_ckv, cache_krope, kvin, w["wuk_t"], w["wuv"], l, o_a, n_p, t_s)
        o_b, st_p = retention(big, cos128, sin128, zero_state, 0, ret_c, 0, bp, s_len, min(8, s_len // CHUNK), 2)
        o_b, st_s = retention(big, cos128, sin128, state_ret, l, ret_c, n_p, bs, t_s, 1, RET_HEADS, prev=o_b)
        mixed = proj_ab(o_a, o_b, w["w_a"], w["w_b"], l, big, 2 * RET_HEADS * RET_QK_DIM + 2 * RET_HEADS * RET_V_DIM)
        tm, tn = 1024, 1024
        x = matmul(mixed, w["w_o"], F32, tm=tm, tn=tn, epilogue=_ep_resid, wl=l,
                   extras=(x, cg[2]), extra_specs=[tile(tm, tn), gspec(tm, tn)], name="proj_o")

        outs["ckv_p"].append(ckvn[:n_p].reshape(bp, s_len, KV_LORA))
        outs["kr_p"].append(krr[:n_p, :ROPE_DIM].reshape(bp, s_len, ROPE_DIM))
        outs["st_p"].append(st_p)
        outs["ckv_s"].append(ckvn[n_p:].reshape(bs, t_s, KV_LORA))
        outs["kr_s"].append(krr[n_p:, :ROPE_DIM].reshape(bs, t_s, ROPE_DIM))
        outs["st_s"].append(st_s)

        h2t = modnorm(x, g_ffn[l], cg[4], cg[3], transposed=True)
        qt = matmul(w["w_pq_t"], h2t, F32, tm=2048, tn=512, xl=l, name="peer_q")
        thr, e1, s2, e2 = peer_route(qt, w["k1"], w["k2"], l)
        x = peer_experts(h2t, w["u"], w["v_t"], l, thr, e1, s2, e2, x, cg[5])

    y_p = final_rmsnorm(x, g_final, 0, n_p)
    y_s = final_rmsnorm(x, g_final, n_p, n_s)
    st = lambda k: jnp.stack(outs[k], axis=0)
    return (y_p.reshape(bp, s_len, d), y_s.reshape(bs, t_s, d),
            st("ckv_p"), st("kr_p"), st("st_p"), st("ckv_s"), st("kr_s"), st("st_s"))
```

```python
import functools

import numpy as np
import jax
import jax.numpy as jnp
from jax import lax
from jax.experimental import pallas as pl
from jax.experimental.pallas import tpu as pltpu

F32 = jnp.float32
BF16 = jnp.bfloat16

D_MODEL = 2048
CHUNK = 64
EPS = 1e-6
ROPE_THETA = 10000.0
MLA_HEADS = 16
Q_LORA = 512
KV_LORA = 512
NOPE_DIM = 128
ROPE_DIM = 64
V_DIM = 128
MLA_SCALE = (NOPE_DIM + ROPE_DIM) ** -0.5
LOG2E = 1.4426950408889634
QK_PAD = 256
KV_IN = KV_LORA + 128
RET_HEADS = 8
RET_QK_DIM = 128
RET_V_DIM = 256
PEER_HEADS = 8
N_KEYS = 128
PEER_HALF = 128
PEER_TOPK = 16
GROUP = 64
LANES = 128
SUBLANES = 8

VMEM_LIMIT = 60 * 1024 * 1024


def _cparams(*sem):
    return pltpu.CompilerParams(dimension_semantics=sem, vmem_limit_bytes=VMEM_LIMIT)


def _modnorm_kernel(x_ref, g_ref, sc_ref, sh_ref, o_ref, *, transposed):
    x = x_ref[...]
    tr, d = x.shape
    y = x * lax.rsqrt(jnp.mean(x * x, axis=-1, keepdims=True) + EPS) * g_ref[...]
    y = (y.reshape(tr // GROUP, GROUP, d) * (1.0 + sc_ref[...]) + sh_ref[...]).reshape(tr, d)
    if transposed:
        y = jnp.transpose(y)
    o_ref[...] = y.astype(o_ref.dtype)


def modnorm(x, g, scale_g, shift_g, tr=512, transposed=False):
    n, d = x.shape
    tr = min(tr, n)
    gpt = tr // GROUP
    return pl.pallas_call(
        functools.partial(_modnorm_kernel, transposed=transposed),
        out_shape=jax.ShapeDtypeStruct((d, n) if transposed else (n, d), BF16),
        grid=(n // tr,),
        in_specs=[pl.BlockSpec((tr, d), lambda i: (i, 0)),
                  pl.BlockSpec((1, d), lambda i: (0, 0)),
                  pl.BlockSpec((gpt, 1, d), lambda i: (i, 0, 0)),
                  pl.BlockSpec((gpt, 1, d), lambda i: (i, 0, 0))],
        out_specs=(pl.BlockSpec((d, tr), lambda i: (0, i)) if transposed
                   else pl.BlockSpec((tr, d), lambda i: (i, 0))),
        compiler_params=_cparams("parallel"),
        name="modnorm",
    )(x, g.reshape(1, d), scale_g, shift_g)


def _rmsnorm_kernel(x_ref, g_ref, o_ref):
    x = x_ref[...]
    o_ref[...] = x * lax.rsqrt(jnp.mean(x * x, axis=-1, keepdims=True) + EPS) * g_ref[...]


def final_rmsnorm(x, g, row0, nrows, tr=512):
    d = x.shape[1]
    tr = min(tr, nrows)
    rb = row0 // tr
    return pl.pallas_call(
        _rmsnorm_kernel,
        out_shape=jax.ShapeDtypeStruct((nrows, d), F32),
        grid=(nrows // tr,),
        in_specs=[pl.BlockSpec((tr, d), lambda i: (rb + i, 0)),
                  pl.BlockSpec((1, d), lambda i: (0, 0))],
        out_specs=pl.BlockSpec((tr, d), lambda i: (i, 0)),
        compiler_params=_cparams("parallel"),
        name="final_rmsnorm",
    )(x, g.reshape(1, d))


def _mm_kernel(x_ref, w_ref, *rest, epilogue):
    extras, o_ref = rest[:-1], rest[-1]
    acc = jnp.dot(x_ref[...].astype(BF16), w_ref[...].astype(BF16), preferred_element_type=F32)
    if epilogue is not None:
        acc = epilogue(acc, *[e[...] for e in extras])
    o_ref[...] = acc.astype(o_ref.dtype)


def matmul(x, w, out_dtype, tm=1024, tn=512, epilogue=None, extras=(), extra_specs=(), name="matmul",
           xl=None, wl=None, m=None):
    k = x.shape[-1]
    m = x.shape[-2] if m is None else m
    n = w.shape[-1]
    tm, tn = min(tm, m), min(tn, n)
    assert m % tm == 0 and n % tn == 0, (m, tm, n, tn)
    xspec = (pl.BlockSpec((tm, k), lambda i, j: (i, 0)) if xl is None
             else pl.BlockSpec((None, tm, k), lambda i, j: (xl, i, 0)))
    wspec = (pl.BlockSpec((k, tn), lambda i, j: (0, j)) if wl is None
             else pl.BlockSpec((None, k, tn), lambda i, j: (wl, 0, j)))
    return pl.pallas_call(
        functools.partial(_mm_kernel, epilogue=epilogue),
        out_shape=jax.ShapeDtypeStruct((m, n), out_dtype),
        grid=(m // tm, n // tn),
        in_specs=[xspec, wspec] + list(extra_specs),
        out_specs=pl.BlockSpec((tm, tn), lambda i, j: (i, j)),
        compiler_params=_cparams("parallel", "arbitrary"),
        name=name,
    )(x, w, *extras)


def _rope_half(x, cos, sa, sb):
    return x * cos + pltpu.roll(x, 96, 1) * sa + pltpu.roll(x, 32, 1) * sb


def _ep_qrope(acc, cos, sa, sb):
    parts = []
    for h in range(acc.shape[1] // QK_PAD):
        parts.append(acc[:, h * QK_PAD:h * QK_PAD + NOPE_DIM])
        parts.append(_rope_half(acc[:, h * QK_PAD + NOPE_DIM:(h + 1) * QK_PAD], cos, sa, sb))
    return jnp.concatenate(parts, axis=1)


def _proj_ab_kernel(oa_ref, ob_ref, wa_ref, wb_ref, ga_ref, gb_ref, o_ref):
    ya = jnp.dot(oa_ref[...], wa_ref[...], preferred_element_type=F32)
    yb = jnp.dot(ob_ref[...], wb_ref[...], preferred_element_type=F32)
    o_ref[...] = (jax.nn.sigmoid(ga_ref[...].astype(F32)) * ya
                  + jax.nn.sigmoid(gb_ref[...].astype(F32)) * yb).astype(o_ref.dtype)


def proj_ab(o_a, o_b, w_a, w_b, l, big, gate_col0, tm=1024, tn=512):
    m, k = o_a.shape
    n = w_a.shape[-1]
    tm = min(tm, m)
    goff = gate_col0 // tn
    xspec = pl.BlockSpec((tm, k), lambda i, j: (i, 0))
    wspec = pl.BlockSpec((None, k, tn), lambda i, j: (l, 0, j))
    return pl.pallas_call(
        _proj_ab_kernel,
        out_shape=jax.ShapeDtypeStruct((m, n), BF16),
        grid=(m // tm, n // tn),
        in_specs=[xspec, xspec, wspec, wspec,
                  pl.BlockSpec((tm, tn), lambda i, j: (i, goff + j)),
                  pl.BlockSpec((tm, tn), lambda i, j: (i, goff + n // tn + j))],
        out_specs=pl.BlockSpec((tm, tn), lambda i, j: (i, j)),
        compiler_params=_cparams("parallel", "arbitrary"),
        name="proj_ab",
    )(o_a, o_b, w_a, w_b, big, big)


def _ep_resid(acc, x, gate):
    tm, tn = acc.shape
    y = acc.reshape(tm // GROUP, GROUP, tn) * gate
    return x + y.reshape(tm, tn)


def _inproj1_kernel(h_ref, w_ref, gq_ref, gkv_ref, cos_ref, sa_ref, sb_ref,
                    cq_ref, ckv_ref, kr_ref, kvin_ref):
    acc = jnp.dot(h_ref[...], w_ref[...], preferred_element_type=F32)
    cq = acc[:, :Q_LORA]
    ckv = acc[:, Q_LORA:Q_LORA + KV_LORA]
    kr = acc[:, Q_LORA + KV_LORA:]
    cqn = cq * lax.rsqrt(jnp.mean(cq * cq, axis=-1, keepdims=True) + EPS) * gq_ref[...]
    ckvn = ckv * lax.rsqrt(jnp.mean(ckv * ckv, axis=-1, keepdims=True) + EPS) * gkv_ref[...]
    krr = _rope_half(kr, cos_ref[...], sa_ref[...], sb_ref[...])
    cq_ref[...] = cqn.astype(BF16)
    ckv_ref[...] = ckvn
    kr_ref[...] = krr
    kvin_ref[:, :KV_LORA] = ckvn.astype(BF16)
    kvin_ref[:, KV_LORA:] = krr.astype(BF16)


def inproj1(h, w1, l, g_q, g_kv, cos, sa, sb, tm=512):
    n, k = h.shape
    tm = min(tm, n)
    w1n = w1.shape[-1]
    row = lambda i: (i, 0)
    fix = lambda i: (0, 0)
    return pl.pallas_call(
        _inproj1_kernel,
        out_shape=(jax.ShapeDtypeStruct((n, Q_LORA), BF16),
                   jax.ShapeDtypeStruct((n, KV_LORA), F32),
                   jax.ShapeDtypeStruct((n, LANES), F32),
                   jax.ShapeDtypeStruct((n, KV_IN), BF16)),
        grid=(n // tm,),
        in_specs=[pl.BlockSpec((tm, k), row), pl.BlockSpec((None, k, w1n), lambda i: (l, 0, 0)),
                  pl.BlockSpec((1, Q_LORA), fix), pl.BlockSpec((1, KV_LORA), fix),
                  pl.BlockSpec((tm, LANES), row), pl.BlockSpec((tm, LANES), row), pl.BlockSpec((tm, LANES), row)],
        out_specs=(pl.BlockSpec((tm, Q_LORA), row), pl.BlockSpec((tm, KV_LORA), row),
                   pl.BlockSpec((tm, LANES), row), pl.BlockSpec((tm, KV_IN), row)),
        compiler_params=_cparams("parallel"),
        name="inproj1",
    )(h, w1, g_q.reshape(1, -1), g_kv.reshape(1, -1), cos, sa, sb)


def _attn_prompt_kernel(q_ref, k_ref, v_ref, o_ref, *, tq, tk, nh):
    qi = pl.program_id(2)
    nt = (((1,), (1,)), ((), ()))
    qs = [q_ref[:, h * QK_PAD:(h + 1) * QK_PAD] for h in range(nh)]

    def scores(start, size, h):
        k = k_ref[pl.ds(start, size), h * QK_PAD:(h + 1) * QK_PAD]
        v = v_ref[pl.ds(start, size), h * V_DIM:(h + 1) * V_DIM]
        return lax.dot_general(qs[h], k, nt, preferred_element_type=F32), v

    def update(carry, s, v):
        m, l, acc = carry
        m_new = jnp.maximum(m, jnp.max(s, axis=-1, keepdims=True))
        alpha = jnp.exp2(m - m_new)
        p = jnp.exp2(s - m_new)
        l = alpha * l + jnp.sum(p, axis=-1, keepdims=True)
        acc = alpha * acc + jnp.dot(p.astype(BF16), v, preferred_element_type=F32)
        return m_new, l, acc

    def big_body(j, carry):
        start = pl.multiple_of(j * tk, tk)
        return tuple(update(carry[h], *scores(start, tk, h)) for h in range(nh))

    q0 = qi * tq
    nbig = q0 // tk

    def small_body(i, carry):
        start = pl.multiple_of(nbig * tk + i * tq, tq)
        return tuple(update(carry[h], *scores(start, tq, h)) for h in range(nh))

    init = tuple((jnp.full((tq, 1), -1e30, F32), jnp.zeros((tq, 1), F32), jnp.zeros((tq, V_DIM), F32))
                 for _ in range(nh))
    carry = lax.fori_loop(0, nbig, big_body, init)
    carry = lax.fori_loop(0, (q0 - nbig * tk) // tq, small_body, carry)
    rows = lax.broadcasted_iota(jnp.int32, (tq, tq), 0) // CHUNK
    cols = lax.broadcasted_iota(jnp.int32, (tq, tq), 1) // CHUNK
    for h in range(nh):
        s, v = scores(pl.multiple_of(q0, tq), tq, h)
        s = jnp.where(cols <= rows, s, -1e30)
        m, l, acc = update(carry[h], s, v)
        o_ref[:, h * V_DIM:(h + 1) * V_DIM] = (acc / l).astype(o_ref.dtype)


def attn_prompt(q_cat, kv, bp, s_len, tq=512, tk=1024, nh=2):
    tq = min(tq, s_len)
    tk = min(tk, s_len)
    nq = s_len // tq
    vcol0 = MLA_HEADS * QK_PAD // (V_DIM * nh)
    return pl.pallas_call(
        functools.partial(_attn_prompt_kernel, tq=tq, tk=tk, nh=nh),
        out_shape=jax.ShapeDtypeStruct((q_cat.shape[0], MLA_HEADS * V_DIM), BF16),
        grid=(bp, MLA_HEADS // nh, nq),
        in_specs=[pl.BlockSpec((tq, QK_PAD * nh), lambda b, h, i: (b * nq + i, h)),
                  pl.BlockSpec((s_len, QK_PAD * nh), lambda b, h, i: (b, h)),
                  pl.BlockSpec((s_len, V_DIM * nh), lambda b, h, i: (b, vcol0 + h))],
        out_specs=pl.BlockSpec((tq, V_DIM * nh), lambda b, h, i: (b * nq + i, h)),
        compiler_params=_cparams("parallel", "parallel", "arbitrary"),
        name="attn_prompt",
    )(q_cat, kv, kv)


def _attn_sample_kernel(q_ref, ckv_ref, kr_ref, new_ref, wuk_ref, wuv_ref, prev_ref, o_ref,
                        qall_ref, kall_ref, olat_ref, *, t):
    del prev_ref
    nt = (((1,), (1,)), ((), ()))
    past = ckv_ref.shape[0]
    for h in range(MLA_HEADS):
        qn = q_ref[:, h * QK_PAD:h * QK_PAD + NOPE_DIM]
        qlat = jnp.dot(qn, wuk_ref[h], preferred_element_type=F32)
        qall_ref[h * t:(h + 1) * t, :KV_LORA] = qlat.astype(BF16)
        qall_ref[h * t:(h + 1) * t, KV_LORA:] = q_ref[:, h * QK_PAD + NOPE_DIM:(h + 1) * QK_PAD]
    kall_ref[:, :KV_LORA] = ckv_ref[...].astype(BF16)
    kall_ref[:, KV_LORA:KV_LORA + ROPE_DIM] = kr_ref[...].astype(BF16)
    kall_ref[:, KV_LORA + ROPE_DIM:] = jnp.zeros((past, KV_IN - KV_LORA - ROPE_DIM), BF16)
    qall = qall_ref[...]
    knew = new_ref[...]
    s_past = lax.dot_general(qall, kall_ref[...], nt, preferred_element_type=F32)
    s_new = lax.dot_general(qall, knew, nt, preferred_element_type=F32)
    m = jnp.maximum(jnp.max(s_past, axis=-1, keepdims=True), jnp.max(s_new, axis=-1, keepdims=True))
    p_past = jnp.exp2(s_past - m)
    p_new = jnp.exp2(s_new - m)
    l = jnp.sum(p_past, axis=-1, keepdims=True) + jnp.sum(p_new, axis=-1, keepdims=True)
    olat = (jnp.dot(p_past.astype(BF16), kall_ref[:, :KV_LORA], preferred_element_type=F32)
            + jnp.dot(p_new.astype(BF16), knew[:, :KV_LORA], preferred_element_type=F32))
    olat_ref[...] = (olat / l).astype(BF16)
    for h in range(MLA_HEADS):
        o = jnp.dot(olat_ref[h * t:(h + 1) * t, :], wuv_ref[h], preferred_element_type=F32)
        o_ref[:, h * V_DIM:(h + 1) * V_DIM] = o.astype(o_ref.dtype)


def attn_sample(q_cat, ckv_past, kr_past, kvin, wuk_t, wuv, l, prev, row0, t):
    _, bs, past, _ = ckv_past.shape
    rb = row0 // t
    return pl.pallas_call(
        functools.partial(_attn_sample_kernel, t=t),
        out_shape=jax.ShapeDtypeStruct(prev.shape, prev.dtype),
        grid=(bs,),
        in_specs=[pl.BlockSpec((t, MLA_HEADS * QK_PAD), lambda b: (rb + b, 0)),
                  pl.BlockSpec((None, None, past, KV_LORA), lambda b: (l, b, 0, 0)),
                  pl.BlockSpec((None, None, past, ROPE_DIM), lambda b: (l, b, 0, 0)),
                  pl.BlockSpec((t, KV_IN), lambda b: (rb + b, 0)),
                  pl.BlockSpec((None, MLA_HEADS, NOPE_DIM, KV_LORA), lambda b: (l, 0, 0, 0)),
                  pl.BlockSpec((None, MLA_HEADS, KV_LORA, V_DIM), lambda b: (l, 0, 0, 0)),
                  pl.BlockSpec(memory_space=pl.ANY)],
        out_specs=pl.BlockSpec((t, MLA_HEADS * V_DIM), lambda b: (rb + b, 0)),
        scratch_shapes=[pltpu.VMEM((MLA_HEADS * t, KV_IN), BF16),
                        pltpu.VMEM((past, KV_IN), BF16),
                        pltpu.VMEM((MLA_HEADS * t, KV_LORA), BF16)],
        input_output_aliases={6: 0},
        compiler_params=_cparams("parallel"),
        name="attn_sample",
    )(q_cat, ckv_past, kr_past, kvin, wuk_t, wuv, prev)


def _ret_consts():
    lg = np.log1p(-(2.0 ** (-5.0 - np.arange(RET_HEADS, dtype=np.float64))))
    idx = np.arange(CHUNK, dtype=np.float64)
    diff = idx[:, None] - idx[None, :]
    dmask = np.where(diff[None] >= 0, np.exp(np.maximum(diff, 0.0)[None] * lg[:, None, None]), 0.0)
    qd = np.exp((idx + 1.0)[None, :] * lg[:, None])
    kd = np.exp((CHUNK - 1.0 - idx)[None, :] * lg[:, None])
    g = np.exp(CHUNK * lg)
    qd = np.broadcast_to(qd[:, :, None], (RET_HEADS, CHUNK, RET_QK_DIM))
    kd = np.broadcast_to(kd[:, :, None], (RET_HEADS, CHUNK, RET_QK_DIM))
    g = np.broadcast_to(g[:, None, None], (RET_HEADS, 1, RET_V_DIM))
    return (jnp.asarray(dmask, F32), jnp.asarray(qd, F32), jnp.asarray(kd, F32), jnp.asarray(g, F32))


def _retention_kernel(q_ref, k_ref, v_ref, g_ref, cos_ref, sin_ref, st0_ref, dm_ref, qd_ref, kd_ref,
                      gam_ref, *rest, cps, hps):
    o_ref, st_ref, state = rest[-3:]
    step = pl.program_id(2)
    nt = (((1,), (1,)), ((), ()))

    @pl.when(step == 0)
    def _():
        state[...] = st0_ref[...]

    for c in range(cps):
        r = slice(c * CHUNK, (c + 1) * CHUNK)
        cos, sin = cos_ref[r, :], sin_ref[r, :]
        for hh in range(hps):
            qk = slice(hh * RET_QK_DIM, (hh + 1) * RET_QK_DIM)
            vv = slice(hh * RET_V_DIM, (hh + 1) * RET_V_DIM)
            qf = q_ref[r, qk].astype(F32)
            kf = k_ref[r, qk].astype(F32)
            q = qf * cos + pltpu.roll(qf, 64, 1) * sin
            k = (kf * cos + pltpu.roll(kf, 64, 1) * sin) * (RET_QK_DIM ** -0.5)
            v = v_ref[r, vv]
            st = state[hh]
            inner = lax.dot_general(q.astype(BF16), k.astype(BF16), nt, preferred_element_type=F32) * dm_ref[hh]
            o = (jnp.dot(inner.astype(BF16), v, preferred_element_type=F32)
                 + jnp.dot((q * qd_ref[hh]).astype(BF16), st.astype(BF16), preferred_element_type=F32))
            kt = jnp.transpose(k * kd_ref[hh]).astype(BF16)
            state[hh] = st * gam_ref[hh] + jnp.dot(kt, v, preferred_element_type=F32)
            mu = jnp.mean(o, axis=-1, keepdims=True)
            oc = o - mu
            var = jnp.mean(oc * oc, axis=-1, keepdims=True)
            gate = g_ref[r, vv].astype(F32)
            o_ref[r, vv] = (gate * jax.nn.sigmoid(gate) * (oc * lax.rsqrt(var + EPS))).astype(o_ref.dtype)

    @pl.when(step == pl.num_programs(2) - 1)
    def _():
        st_ref[0] = state[...]


def retention(big, cos, sin, state0, l, consts, row0, nseq, seq_len, cps, hps, prev=None):
    r = cps * CHUNK
    steps = seq_len // r
    rb0 = row0 // r
    ng = RET_HEADS // hps
    dmask, qd, kd, gam = consts
    rowblk = lambda s, h, t: rb0 + s * steps + t
    hconst = lambda s, h, t: (h, 0, 0)
    in_specs = [pl.BlockSpec((r, hps * RET_QK_DIM), lambda s, h, t: (rowblk(s, h, t), h)),
                pl.BlockSpec((r, hps * RET_QK_DIM), lambda s, h, t: (rowblk(s, h, t), ng + h)),
                pl.BlockSpec((r, hps * RET_V_DIM), lambda s, h, t: (rowblk(s, h, t), ng + h)),
                pl.BlockSpec((r, hps * RET_V_DIM), lambda s, h, t: (rowblk(s, h, t), 2 * ng + h)),
                pl.BlockSpec((r, RET_QK_DIM), lambda s, h, t: (rowblk(s, h, t), 0)),
                pl.BlockSpec((r, RET_QK_DIM), lambda s, h, t: (rowblk(s, h, t), 0)),
                pl.BlockSpec((None, None, hps, RET_QK_DIM, RET_V_DIM), lambda s, h, t: (l, s, h, 0, 0)),
                pl.BlockSpec((hps, CHUNK, CHUNK), hconst),
                pl.BlockSpec((hps, CHUNK, RET_QK_DIM), hconst),
                pl.BlockSpec((hps, CHUNK, RET_QK_DIM), hconst),
                pl.BlockSpec((hps, 1, RET_V_DIM), hconst)]
    args = [big, big, big, big, cos, sin, state0, dmask, qd, kd, gam]
    aliases = {}
    if prev is not None:
        in_specs.append(pl.BlockSpec(memory_space=pl.ANY))
        args.append(prev)
        aliases = {len(args) - 1: 0}
    return pl.pallas_call(
        functools.partial(_retention_kernel, cps=cps, hps=hps),
        out_shape=(jax.ShapeDtypeStruct((big.shape[0], RET_HEADS * RET_V_DIM), BF16),
                   jax.ShapeDtypeStruct((nseq, RET_HEADS, RET_QK_DIM, RET_V_DIM), F32)),
        grid=(nseq, ng, steps),
        in_specs=in_specs,
        out_specs=(pl.BlockSpec((r, hps * RET_V_DIM), lambda s, h, t: (rowblk(s, h, t), h)),
                   pl.BlockSpec((1, hps, RET_QK_DIM, RET_V_DIM), lambda s, h, t: (s, h, 0, 0))),
        scratch_shapes=[pltpu.VMEM((hps, RET_QK_DIM, RET_V_DIM), F32)],
        input_output_aliases=aliases,
        compiler_params=_cparams("parallel", "parallel", "arbitrary"),
        name="retention",
    )(*args)


_CAND = [(a, b) for a in range(PEER_TOPK) for b in range(PEER_TOPK) if (a + 1) * (b + 1) <= PEER_TOPK]


def _top_distinct(s):
    vals, cnts = [], []
    for _ in range(PEER_TOPK):
        m = jnp.max(s, axis=0, keepdims=True)
        hit = s == m
        vals.append(m)
        cnts.append(jnp.sum(jnp.where(hit, 1.0, 0.0), axis=0, keepdims=True))
        s = jnp.where(hit, -jnp.inf, s)
    return vals, cnts


def _route_kernel(qt_ref, k1_ref, k2_ref, thr_ref, e1_ref, s2_ref, e2_ref, cand_ref, mult_ref):
    tt = qt_ref.shape[1]
    pad = cand_ref.shape[0] - len(_CAND)
    cand_ref[len(_CAND):, :] = jnp.full((pad, LANES), -jnp.inf, F32)
    mult_ref[len(_CAND):, :] = jnp.zeros((pad, LANES), F32)
    for h, c in [(h, c) for h in range(PEER_HEADS) for c in range(tt // LANES)]:
        lanes = slice(c * LANES, (c + 1) * LANES)
        q1 = qt_ref[h * 2 * PEER_HALF:h * 2 * PEER_HALF + PEER_HALF, lanes].astype(BF16)
        q2 = qt_ref[h * 2 * PEER_HALF + PEER_HALF:(h + 1) * 2 * PEER_HALF, lanes].astype(BF16)
        s1 = jnp.dot(k1_ref[...], q1, preferred_element_type=F32)
        s2 = jnp.dot(k2_ref[...], q2, preferred_element_type=F32)
        v1, c1 = _top_distinct(s1)
        v2, c2 = _top_distinct(s2)
        for r, (a, b) in enumerate(_CAND):
            cand_ref[r:r + 1, :] = v1[a] + v2[b]
            mult_ref[r:r + 1, :] = c1[a] * c2[b]
        cand = cand_ref[...]
        mult = mult_ref[...]
        rest = cand
        cum = jnp.zeros_like(v1[0])
        tau = v1[0] + v2[0]
        for _ in range(PEER_TOPK):
            m = jnp.max(rest, axis=0, keepdims=True)
            hit = rest == m
            tau = jnp.where(cum < PEER_TOPK, m, tau)
            cum = cum + jnp.sum(jnp.where(hit, mult, 0.0), axis=0, keepdims=True)
            rest = jnp.where(hit, -jnp.inf, rest)
        top = v1[0] + v2[0]
        z = jnp.sum(jnp.where(cand >= tau, mult * jnp.exp(cand - top), 0.0), axis=0, keepdims=True)
        thr = jnp.full_like(s1, jnp.inf)
        for a in range(PEER_TOPK):
            th_a = jnp.full_like(tau, jnp.inf)
            for b in range(PEER_TOPK // (a + 1)):
                th_a = jnp.where(v1[a] + v2[b] >= tau, v2[b], th_a)
            thr = jnp.where(s1 == v1[a], th_a, thr)
        thr_ref[h, :, lanes] = thr
        s2_ref[h, :, lanes] = s2
        e1_ref[h, :, lanes] = 0.5 * jnp.exp(s1 - v1[0])
        e2_ref[h, :, lanes] = jnp.exp(s2 - v2[0]) / z


def peer_route(qt, k1, k2, l, tt=256):
    n = qt.shape[1]
    tt = min(tt, n)
    big = jax.ShapeDtypeStruct((PEER_HEADS, N_KEYS, n), F32)
    bspec = pl.BlockSpec((PEER_HEADS, N_KEYS, tt), lambda i: (0, 0, i))
    return pl.pallas_call(
        _route_kernel,
        out_shape=(big, big, big, big),
        grid=(n // tt,),
        in_specs=[pl.BlockSpec((PEER_HEADS * 2 * PEER_HALF, tt), lambda i: (0, i)),
                  pl.BlockSpec((None, N_KEYS, PEER_HALF), lambda i: (l, 0, 0)),
                  pl.BlockSpec((None, N_KEYS, PEER_HALF), lambda i: (l, 0, 0))],
        out_specs=(bspec, bspec, bspec, bspec),
        scratch_shapes=[pltpu.VMEM((-(-len(_CAND) // SUBLANES) * SUBLANES, LANES), F32)] * 2,
        compiler_params=_cparams("parallel"),
        name="peer_route",
    )(qt, k1, k2)


def _gate_rows(ii, thr_ref, e1_ref, s2_ref, e2_ref):
    w = None
    for h in range(PEER_HEADS):
        c = jnp.where(s2_ref[h] >= thr_ref[h, ii:ii + 1, :], e2_ref[h], 0.0) * e1_ref[h, ii:ii + 1, :]
        w = c if w is None else w + c
    return w


def _peer_expert_kernel(ht_ref, u_ref, vt_ref, thc_ref, e1c_ref, thn_ref, e1n_ref, s2_ref, e2_ref,
                        x_ref, gate_ref, o_ref, acc_ref, wg_ref, w_ref, *, ipt):
    e = pl.program_id(1)
    last = pl.num_programs(1) - 1

    def build(thr_ref, e1_ref, slot):
        for ii in range(ipt):
            w_ref[slot, ii * N_KEYS:(ii + 1) * N_KEYS, :] = _gate_rows(ii, thr_ref, e1_ref, s2_ref, e2_ref)

    @pl.when(e == 0)
    def _():
        acc_ref[...] = jnp.zeros_like(acc_ref)
        build(thc_ref, e1c_ref, 0)

    build(thn_ref, e1n_ref, (e + 1) % 2)
    act = jnp.dot(u_ref[...], ht_ref[...], preferred_element_type=F32)
    gelu2 = act * (1.0 + lax.erf(act * (2.0 ** -0.5)))
    wg_ref[...] = (w_ref[e % 2] * gelu2).astype(BF16)
    acc_ref[...] += jnp.dot(vt_ref[...], wg_ref[...], preferred_element_type=F32)

    @pl.when(e == last)
    def _():
        tt, d = o_ref.shape
        y = jnp.transpose(acc_ref[...]).reshape(tt // GROUP, GROUP, d) * gate_ref[...]
        o_ref[...] = x_ref[...] + y.reshape(tt, d)


def peer_experts(ht, u, vt, l, thr, e1, s2, e2, x, gate_g, tt=512, ipt=8):
    d, n = ht.shape
    ne = u.shape[1]
    tt = min(tt, n)
    te = ipt * N_KEYS
    nlast = ne // te - 1
    full = pl.BlockSpec((PEER_HEADS, N_KEYS, tt), lambda t, e: (0, 0, t))
    cur = pl.BlockSpec((PEER_HEADS, ipt, tt), lambda t, e: (0, e, t))
    nxt = pl.BlockSpec((PEER_HEADS, ipt, tt), lambda t, e: (0, jnp.minimum(e + 1, nlast), t))
    return pl.pallas_call(
        functools.partial(_peer_expert_kernel, ipt=ipt),
        out_shape=jax.ShapeDtypeStruct((n, d), F32),
        grid=(n // tt, ne // te),
        in_specs=[pl.BlockSpec((d, tt), lambda t, e: (0, t)),
                  pl.BlockSpec((None, te, d), lambda t, e: (l, e, 0)),
                  pl.BlockSpec((None, d, te), lambda t, e: (l, 0, e)),
                  cur, cur, nxt, nxt, full, full,
                  pl.BlockSpec((tt, d), lambda t, e: (t, 0), pipeline_mode=pl.Buffered(1)),
                  pl.BlockSpec((tt // GROUP, 1, d), lambda t, e: (t, 0, 0))],
        out_specs=pl.BlockSpec((tt, d), lambda t, e: (t, 0)),
        scratch_shapes=[pltpu.VMEM((d, tt), F32), pltpu.VMEM((te, tt), BF16), pltpu.VMEM((2, te, tt), F32)],
        compiler_params=_cparams("parallel", "arbitrary"),
        name="peer_experts",
    )(ht, u, vt, thr, e1, thr, e1, s2, e2, x, gate_g)


def _rope_tables(pos):
    pos = pos.astype(F32)[:, None]
    half = ROPE_DIM // 2
    inv = ROPE_THETA ** (-jnp.arange(half, dtype=F32) * 2.0 / ROPE_DIM)
    ang = pos * inv[None, :]
    c, s = jnp.cos(ang), jnp.sin(ang)
    z32 = jnp.zeros_like(c)
    z64 = jnp.concatenate([z32, z32], axis=1)
    cos64 = jnp.concatenate([c, c, z64], axis=1)
    sa64 = jnp.concatenate([-s, z32, z64], axis=1)
    sb64 = jnp.concatenate([z32, s, z64], axis=1)
    half = RET_QK_DIM // 2
    inv = ROPE_THETA ** (-jnp.arange(half, dtype=F32) * 2.0 / RET_QK_DIM)
    ang = pos * inv[None, :]
    c, s = jnp.cos(ang), jnp.sin(ang)
    return cos64, sa64, sb64, jnp.concatenate([c, c], axis=1), jnp.concatenate([-s, s], axis=1)


def _prep_weights(w_ada, w_in, w_uq, w_ukv, w_a, w_b, w_o, w_pq, k1, k2, u, v):
    nl = w_in.shape[0]
    o1 = Q_LORA + KV_LORA + ROPE_DIM
    w1 = jnp.pad(w_in[:, :, :o1], ((0, 0), (0, 0), (0, LANES - ROPE_DIM))).astype(BF16)
    w2 = w_in[:, :, o1:].astype(BF16)
    uq = w_uq.reshape(nl, Q_LORA, MLA_HEADS, NOPE_DIM + ROPE_DIM) * (MLA_SCALE * LOG2E)
    uq = jnp.pad(uq, ((0, 0), (0, 0), (0, 0), (0, QK_PAD - NOPE_DIM - ROPE_DIM)))
    uq = uq.reshape(nl, Q_LORA, MLA_HEADS * QK_PAD).astype(BF16)
    wk = jnp.pad(w_ukv[..., :NOPE_DIM], ((0, 0), (0, KV_IN - KV_LORA), (0, 0), (0, QK_PAD - NOPE_DIM)))
    eye = jnp.zeros((KV_IN, QK_PAD), F32).at[KV_LORA + jnp.arange(ROPE_DIM), NOPE_DIM + jnp.arange(ROPE_DIM)].set(1.0)
    wk = (wk + eye[None, :, None, :]).reshape(nl, KV_IN, MLA_HEADS * QK_PAD)
    wv = jnp.pad(w_ukv[..., NOPE_DIM:], ((0, 0), (0, KV_IN - KV_LORA), (0, 0), (0, 0)))
    wv = wv.reshape(nl, KV_IN, MLA_HEADS * V_DIM)
    wkv = jnp.concatenate([wk, wv], axis=2).astype(BF16)
    wuk_t = jnp.transpose(w_ukv[..., :NOPE_DIM], (0, 2, 3, 1)).astype(BF16)
    wuv = jnp.transpose(w_ukv[..., NOPE_DIM:], (0, 2, 1, 3)).astype(BF16)
    return dict(w_ada=w_ada, w1=w1, w2=w2, uq=uq, wkv=wkv, wuk_t=wuk_t, wuv=wuv,
                w_a=w_a.astype(BF16), w_b=w_b.astype(BF16), w_o=w_o.astype(BF16),
                w_pq_t=jnp.swapaxes(w_pq, 1, 2).astype(BF16), k1=k1.astype(BF16), k2=k2.astype(BF16),
                u=u.astype(BF16), v_t=jnp.swapaxes(v, 1, 2).astype(BF16))


def kernel(x_prompt, x_sample, c_prompt, c_sample, cache_ckv, cache_krope, state_ret, w_ada, b_ada, g_mix, g_ffn, w_in, g_q, w_uq, g_kv, w_ukv, w_a, w_b, w_o, w_pq, peer_k1, peer_k2, peer_u, peer_v, g_final):
    bp, s_len, d = x_prompt.shape
    bs, t_s, _ = x_sample.shape
    depth = w_in.shape[0]
    past = cache_ckv.shape[2]
    n_p, n_s = bp * s_len, bs * t_s
    n = n_p + n_s
    assert t_s == GROUP and s_len % GROUP == 0

    x = jnp.concatenate([x_prompt.reshape(n_p, d), x_sample.reshape(n_s, d)], axis=0)
    c_all = jnp.concatenate([c_prompt, c_sample], axis=0)
    gidx = jnp.concatenate([jnp.repeat(jnp.arange(bp), s_len // GROUP), bp + jnp.arange(bs)])
    pos = jnp.concatenate([jnp.tile(jnp.arange(s_len), bp), jnp.tile(past + jnp.arange(t_s), bs)])
    cos64, sa64, sb64, cos128, sin128 = _rope_tables(pos)
    ret_c = _ret_consts()
    w = _prep_weights(w_ada, w_in, w_uq, w_ukv, w_a, w_b, w_o, w_pq, peer_k1, peer_k2, peer_u, peer_v)
    zero_state = jnp.zeros((1, bp, RET_HEADS, RET_QK_DIM, RET_V_DIM), F32)
    silu_c = jax.nn.silu(c_all)
    gspec = lambda tm, tn: pl.BlockSpec((tm // GROUP, 1, tn), lambda i, j: (i, 0, j))
    tile = lambda tm, tn, off=0: pl.BlockSpec((tm, tn), lambda i, j: (i, j + off))
    rope_spec = lambda tm: pl.BlockSpec((tm, LANES), lambda i, j: (i, 0))

    outs = {k: [] for k in ("ckv_p", "kr_p", "st_p", "ckv_s", "kr_s", "st_s")}
    for l in range(depth):
        ada = matmul(silu_c, w["w_ada"], F32, wl=l, name="ada") + b_ada[l][None, :]
        cond = ada.reshape(bp + bs, 6, d)[gidx]
        cg = [cond[:, k, :][:, None, :] for k in range(6)]

        h = modnorm(x, g_mix[l], cg[1], cg[0])
        cqn, ckvn, krr, kvin = inproj1(h, w["w1"], l, g_q[l], g_kv[l], cos64, sa64, sb64)
        big = matmul(h, w["w2"], BF16, tm=1024, tn=1024, wl=l, name="inproj2")
        tmq = 1024
        q_cat = matmul(cqn, w["uq"], BF16, tm=tmq, tn=2048, epilogue=_ep_qrope, wl=l,
                       extras=(cos64, sa64, sb64), extra_specs=[rope_spec(tmq)] * 3, name="q_up")
        kv = matmul(kvin, w["wkv"], BF16, tm=1024, tn=2048, wl=l, m=n_p, name="kv_up")
        o_a = attn_prompt(q_cat, kv, bp, s_len)
        o_a = attn_sample(q_cat, cache_ckv, cache_krope, kvin, w["wuk_t"], w["wuv"], l, o_a, n_p, t_s)
        o_b, st_p = retention(big, cos128, sin128, zero_state, 0, ret_c, 0, bp, s_len, min(8, s_len // CHUNK), 2)
        o_b, st_s = retention(big, cos128, sin128, state_ret, l, ret_c, n_p, bs, t_s, 1, RET_HEADS, prev=o_b)
        mixed = proj_ab(o_a, o_b, w["w_a"], w["w_b"], l, big, 2 * RET_HEADS * RET_QK_DIM + 2 * RET_HEADS * RET_V_DIM)
        tm, tn = 1024, 1024
        x = matmul(mixed, w["w_o"], F32, tm=tm, tn=tn, epilogue=_ep_resid, wl=l,
                   extras=(x, cg[2]), extra_specs=[tile(tm, tn), gspec(tm, tn)], name="proj_o")

        outs["ckv_p"].append(ckvn[:n_p].reshape(bp, s_len, KV_LORA))
        outs["kr_p"].append(krr[:n_p, :ROPE_DIM].reshape(bp, s_len, ROPE_DIM))
        outs["st_p"].append(st_p)
        outs["ckv_s"].append(ckvn[n_p:].reshape(bs, t_s, KV_LORA))
        outs["kr_s"].append(krr[n_p:, :ROPE_DIM].reshape(bs, t_s, ROPE_DIM))
        outs["st_s"].append(st_s)

        h2t = modnorm(x, g_ffn[l], cg[4], cg[3], transposed=True)
        qt = matmul(w["w_pq_t"], h2t, F32, tm=2048, tn=512, xl=l, name="peer_q")
        thr, e1, s2, e2 = peer_route(qt, w["k1"], w["k2"], l)
        x = peer_experts(h2t, w["u"], w["v_t"], l, thr, e1, s2, e2, x, cg[5])

    y_p = final_rmsnorm(x, g_final, 0, n_p)
    y_s = final_rmsnorm(x, g_final, n_p, n_s)
    st = lambda k: jnp.stack(outs[k], axis=0)
    return (y_p.reshape(bp, s_len, d), y_s.reshape(bs, t_s, d),
            st("ckv_p"), st("kr_p"), st("st_p"), st("ckv_s"), st("kr_s"), st("st_s"))
```

```python
import functools

import numpy as np
import jax
import jax.numpy as jnp
from jax import lax
from jax.experimental import pallas as pl
from jax.experimental.pallas import tpu as pltpu

F32 = jnp.float32
BF16 = jnp.bfloat16

D_MODEL = 2048
CHUNK = 64
EPS = 1e-6
ROPE_THETA = 10000.0
MLA_HEADS = 16
Q_LORA = 512
KV_LORA = 512
NOPE_DIM = 128
ROPE_DIM = 64
V_DIM = 128
MLA_SCALE = (NOPE_DIM + ROPE_DIM) ** -0.5
LOG2E = 1.4426950408889634
QK_PAD = 256
KV_IN = KV_LORA + 128
RET_HEADS = 8
RET_QK_DIM = 128
RET_V_DIM = 256
PEER_HEADS = 8
N_KEYS = 128
PEER_HALF = 128
PEER_TOPK = 16
GROUP = 64
LANES = 128
SUBLANES = 8

VMEM_LIMIT = 60 * 1024 * 1024


def _cparams(*sem):
    return pltpu.CompilerParams(dimension_semantics=sem, vmem_limit_bytes=VMEM_LIMIT)


def _modnorm_kernel(x_ref, g_ref, sc_ref, sh_ref, o_ref, *, transposed):
    x = x_ref[...]
    tr, d = x.shape
    y = x * lax.rsqrt(jnp.mean(x * x, axis=-1, keepdims=True) + EPS) * g_ref[...]
    y = (y.reshape(tr // GROUP, GROUP, d) * (1.0 + sc_ref[...]) + sh_ref[...]).reshape(tr, d)
    if transposed:
        y = jnp.transpose(y)
    o_ref[...] = y.astype(o_ref.dtype)


def modnorm(x, g, scale_g, shift_g, tr=512, transposed=False):
    n, d = x.shape
    tr = min(tr, n)
    gpt = tr // GROUP
    return pl.pallas_call(
        functools.partial(_modnorm_kernel, transposed=transposed),
        out_shape=jax.ShapeDtypeStruct((d, n) if transposed else (n, d), BF16),
        grid=(n // tr,),
        in_specs=[pl.BlockSpec((tr, d), lambda i: (i, 0)),
                  pl.BlockSpec((1, d), lambda i: (0, 0)),
                  pl.BlockSpec((gpt, 1, d), lambda i: (i, 0, 0)),
                  pl.BlockSpec((gpt, 1, d), lambda i: (i, 0, 0))],
        out_specs=(pl.BlockSpec((d, tr), lambda i: (0, i)) if transposed
                   else pl.BlockSpec((tr, d), lambda i: (i, 0))),
        compiler_params=_cparams("parallel"),
        name="modnorm",
    )(x, g.reshape(1, d), scale_g, shift_g)


def _rmsnorm_kernel(x_ref, g_ref, o_ref):
    x = x_ref[...]
    o_ref[...] = x * lax.rsqrt(jnp.mean(x * x, axis=-1, keepdims=True) + EPS) * g_ref[...]


def final_rmsnorm(x, g, row0, nrows, tr=512):
    d = x.shape[1]
    tr = min(tr, nrows)
    rb = row0 // tr
    return pl.pallas_call(
        _rmsnorm_kernel,
        out_shape=jax.ShapeDtypeStruct((nrows, d), F32),
        grid=(nrows // tr,),
        in_specs=[pl.BlockSpec((tr, d), lambda i: (rb + i, 0)),
                  pl.BlockSpec((1, d), lambda i: (0, 0))],
        out_specs=pl.BlockSpec((tr, d), lambda i: (i, 0)),
        compiler_params=_cparams("parallel"),
        name="final_rmsnorm",
    )(x, g.reshape(1, d))


def _mm_kernel(x_ref, w_ref, *rest, epilogue):
    extras, o_ref = rest[:-1], rest[-1]
    acc = jnp.dot(x_ref[...].astype(BF16), w_ref[...].astype(BF16), preferred_element_type=F32)
    if epilogue is not None:
        acc = epilogue(acc, *[e[...] for e in extras])
    o_ref[...] = acc.astype(o_ref.dtype)


def matmul(x, w, out_dtype, tm=1024, tn=512, epilogue=None, extras=(), extra_specs=(), name="matmul",
           xl=None, wl=None, m=None):
    k = x.shape[-1]
    m = x.shape[-2] if m is None else m
    n = w.shape[-1]
    tm, tn = min(tm, m), min(tn, n)
    assert m % tm == 0 and n % tn == 0, (m, tm, n, tn)
    xspec = (pl.BlockSpec((tm, k), lambda i, j: (i, 0)) if xl is None
             else pl.BlockSpec((None, tm, k), lambda i, j: (xl, i, 0)))
    wspec = (pl.BlockSpec((k, tn), lambda i, j: (0, j)) if wl is None
             else pl.BlockSpec((None, k, tn), lambda i, j: (wl, 0, j)))
    return pl.pallas_call(
        functools.partial(_mm_kernel, epilogue=epilogue),
        out_shape=jax.ShapeDtypeStruct((m, n), out_dtype),
        grid=(m // tm, n // tn),
        in_specs=[xspec, wspec] + list(extra_specs),
        out_specs=pl.BlockSpec((tm, tn), lambda i, j: (i, j)),
        compiler_params=_cparams("parallel", "arbitrary"),
        name=name,
    )(x, w, *extras)


def _rope_half(x, cos, sa, sb):
    return x * cos + pltpu.roll(x, 96, 1) * sa + pltpu.roll(x, 32, 1) * sb


def _ep_qrope(acc, cos, sa, sb):
    parts = []
    for h in range(acc.shape[1] // QK_PAD):
        parts.append(acc[:, h * QK_PAD:h * QK_PAD + NOPE_DIM])
        parts.append(_rope_half(acc[:, h * QK_PAD + NOPE_DIM:(h + 1) * QK_PAD], cos, sa, sb))
    return jnp.concatenate(parts, axis=1)


def _proj_ab_kernel(oa_ref, ob_ref, wa_ref, wb_ref, ga_ref, gb_ref, o_ref):
    ya = jnp.dot(oa_ref[...], wa_ref[...], preferred_element_type=F32)
    yb = jnp.dot(ob_ref[...], wb_ref[...], preferred_element_type=F32)
    o_ref[...] = (jax.nn.sigmoid(ga_ref[...].astype(F32)) * ya
                  + jax.nn.sigmoid(gb_ref[...].astype(F32)) * yb).astype(o_ref.dtype)


def proj_ab(o_a, o_b, w_a, w_b, l, big, gate_col0, tm=1024, tn=512):
    m, k = o_a.shape
    n = w_a.shape[-1]
    tm = min(tm, m)
    goff = gate_col0 // tn
    xspec = pl.BlockSpec((tm, k), lambda i, j: (i, 0))
    wspec = pl.BlockSpec((None, k, tn), lambda i, j: (l, 0, j))
    return pl.pallas_call(
        _proj_ab_kernel,
        out_shape=jax.ShapeDtypeStruct((m, n), BF16),
        grid=(m // tm, n // tn),
        in_specs=[xspec, xspec, wspec, wspec,
                  pl.BlockSpec((tm, tn), lambda i, j: (i, goff + j)),
                  pl.BlockSpec((tm, tn), lambda i, j: (i, goff + n // tn + j))],
        out_specs=pl.BlockSpec((tm, tn), lambda i, j: (i, j)),
        compiler_params=_cparams("parallel", "arbitrary"),
        name="proj_ab",
    )(o_a, o_b, w_a, w_b, big, big)


def _ep_resid(acc, x, gate):
    tm, tn = acc.shape
    y = acc.reshape(tm // GROUP, GROUP, tn) * gate
    return x + y.reshape(tm, tn)


def _inproj1_kernel(h_ref, w_ref, gq_ref, gkv_ref, cos_ref, sa_ref, sb_ref,
                    cq_ref, ckv_ref, kr_ref, kvin_ref):
    acc = jnp.dot(h_ref[...], w_ref[...], preferred_element_type=F32)
    cq = acc[:, :Q_LORA]
    ckv = acc[:, Q_LORA:Q_LORA + KV_LORA]
    kr = acc[:, Q_LORA + KV_LORA:]
    cqn = cq * lax.rsqrt(jnp.mean(cq * cq, axis=-1, keepdims=True) + EPS) * gq_ref[...]
    ckvn = ckv * lax.rsqrt(jnp.mean(ckv * ckv, axis=-1, keepdims=True) + EPS) * gkv_ref[...]
    krr = _rope_half(kr, cos_ref[...], sa_ref[...], sb_ref[...])
    cq_ref[...] = cqn.astype(BF16)
    ckv_ref[...] = ckvn
    kr_ref[...] = krr
    kvin_ref[:, :KV_LORA] = ckvn.astype(BF16)
    kvin_ref[:, KV_LORA:] = krr.astype(BF16)


def inproj1(h, w1, l, g_q, g_kv, cos, sa, sb, tm=512):
    n, k = h.shape
    tm = min(tm, n)
    w1n = w1.shape[-1]
    row = lambda i: (i, 0)
    fix = lambda i: (0, 0)
    return pl.pallas_call(
        _inproj1_kernel,
        out_shape=(jax.ShapeDtypeStruct((n, Q_LORA), BF16),
                   jax.ShapeDtypeStruct((n, KV_LORA), F32),
                   jax.ShapeDtypeStruct((n, LANES), F32),
                   jax.ShapeDtypeStruct((n, KV_IN), BF16)),
        grid=(n // tm,),
        in_specs=[pl.BlockSpec((tm, k), row), pl.BlockSpec((None, k, w1n), lambda i: (l, 0, 0)),
                  pl.BlockSpec((1, Q_LORA), fix), pl.BlockSpec((1, KV_LORA), fix),
                  pl.BlockSpec((tm, LANES), row), pl.BlockSpec((tm, LANES), row), pl.BlockSpec((tm, LANES), row)],
        out_specs=(pl.BlockSpec((tm, Q_LORA), row), pl.BlockSpec((tm, KV_LORA), row),
                   pl.BlockSpec((tm, LANES), row), pl.BlockSpec((tm, KV_IN), row)),
        compiler_params=_cparams("parallel"),
        name="inproj1",
    )(h, w1, g_q.reshape(1, -1), g_kv.reshape(1, -1), cos, sa, sb)


def _attn_prompt_kernel(q_ref, k_ref, v_ref, o_ref, *, tq, tk, nh):
    qi = pl.program_id(2)
    nt = (((1,), (1,)), ((), ()))
    qs = [q_ref[:, h * QK_PAD:(h + 1) * QK_PAD] for h in range(nh)]

    def scores(start, size, h):
        k = k_ref[pl.ds(start, size), h * QK_PAD:(h + 1) * QK_PAD]
        v = v_ref[pl.ds(start, size), h * V_DIM:(h + 1) * V_DIM]
        return lax.dot_general(qs[h], k, nt, preferred_element_type=F32), v

    def update(carry, s, v):
        m, l, acc = carry
        m_new = jnp.maximum(m, jnp.max(s, axis=-1, keepdims=True))
        alpha = jnp.exp2(m - m_new)
        p = jnp.exp2(s - m_new)
        l = alpha * l + jnp.sum(p, axis=-1, keepdims=True)
        acc = alpha * acc + jnp.dot(p.astype(BF16), v, preferred_element_type=F32)
        return m_new, l, acc

    def big_body(j, carry):
        start = pl.multiple_of(j * tk, tk)
        return tuple(update(carry[h], *scores(start, tk, h)) for h in range(nh))

    q0 = qi * tq
    nbig = q0 // tk

    def small_body(i, carry):
        start = pl.multiple_of(nbig * tk + i * tq, tq)
        return tuple(update(carry[h], *scores(start, tq, h)) for h in range(nh))

    init = tuple((jnp.full((tq, 1), -1e30, F32), jnp.zeros((tq, 1), F32), jnp.zeros((tq, V_DIM), F32))
                 for _ in range(nh))
    carry = lax.fori_loop(0, nbig, big_body, init)
    carry = lax.fori_loop(0, (q0 - nbig * tk) // tq, small_body, carry)
    rows = lax.broadcasted_iota(jnp.int32, (tq, tq), 0) // CHUNK
    cols = lax.broadcasted_iota(jnp.int32, (tq, tq), 1) // CHUNK
    for h in range(nh):
        s, v = scores(pl.multiple_of(q0, tq), tq, h)
        s = jnp.where(cols <= rows, s, -1e30)
        m, l, acc = update(carry[h], s, v)
        o_ref[:, h * V_DIM:(h + 1) * V_DIM] = (acc / l).astype(o_ref.dtype)


def attn_prompt(q_cat, kv, bp, s_len, tq=512, tk=1024, nh=4):
    tq = min(tq, s_len)
    tk = min(tk, s_len)
    nq = s_len // tq
    vcol0 = MLA_HEADS * QK_PAD // (V_DIM * nh)
    return pl.pallas_call(
        functools.partial(_attn_prompt_kernel, tq=tq, tk=tk, nh=nh),
        out_shape=jax.ShapeDtypeStruct((q_cat.shape[0], MLA_HEADS * V_DIM), BF16),
        grid=(bp, MLA_HEADS // nh, nq),
        in_specs=[pl.BlockSpec((tq, QK_PAD * nh), lambda b, h, i: (b * nq + i, h)),
                  pl.BlockSpec((s_len, QK_PAD * nh), lambda b, h, i: (b, h)),
                  pl.BlockSpec((s_len, V_DIM * nh), lambda b, h, i: (b, vcol0 + h))],
        out_specs=pl.BlockSpec((tq, V_DIM * nh), lambda b, h, i: (b * nq + i, h)),
        compiler_params=_cparams("parallel", "parallel", "arbitrary"),
        name="attn_prompt",
    )(q_cat, kv, kv)


def _attn_sample_kernel(q_ref, ckv_ref, kr_ref, new_ref, wuk_ref, wuv_ref, prev_ref, o_ref,
                        qall_ref, kall_ref, olat_ref, *, t):
    del prev_ref
    nt = (((1,), (1,)), ((), ()))
    past = ckv_ref.shape[0]
    for h in range(MLA_HEADS):
        qn = q_ref[:, h * QK_PAD:h * QK_PAD + NOPE_DIM]
        qlat = jnp.dot(qn, wuk_ref[h], preferred_element_type=F32)
        qall_ref[h * t:(h + 1) * t, :KV_LORA] = qlat.astype(BF16)
        qall_ref[h * t:(h + 1) * t, KV_LORA:] = q_ref[:, h * QK_PAD + NOPE_DIM:(h + 1) * QK_PAD]
    kall_ref[:, :KV_LORA] = ckv_ref[...].astype(BF16)
    kall_ref[:, KV_LORA:KV_LORA + ROPE_DIM] = kr_ref[...].astype(BF16)
    kall_ref[:, KV_LORA + ROPE_DIM:] = jnp.zeros((past, KV_IN - KV_LORA - ROPE_DIM), BF16)
    knew = new_ref[...]
    half = MLA_HEADS * t // 2
    for r0 in (0, half):
        qall = qall_ref[r0:r0 + half, :]
        s_past = lax.dot_general(qall, kall_ref[...], nt, preferred_element_type=F32)
        s_new = lax.dot_general(qall, knew, nt, preferred_element_type=F32)
        m = jnp.maximum(jnp.max(s_past, axis=-1, keepdims=True), jnp.max(s_new, axis=-1, keepdims=True))
        p_past = jnp.exp2(s_past - m)
        p_new = jnp.exp2(s_new - m)
        l = jnp.sum(p_past, axis=-1, keepdims=True) + jnp.sum(p_new, axis=-1, keepdims=True)
        olat = (jnp.dot(p_past.astype(BF16), kall_ref[:, :KV_LORA], preferred_element_type=F32)
                + jnp.dot(p_new.astype(BF16), knew[:, :KV_LORA], preferred_element_type=F32))
        olat_ref[r0:r0 + half, :] = (olat / l).astype(BF16)
    for h in range(MLA_HEADS):
        o = jnp.dot(olat_ref[h * t:(h + 1) * t, :], wuv_ref[h], preferred_element_type=F32)
        o_ref[:, h * V_DIM:(h + 1) * V_DIM] = o.astype(o_ref.dtype)


def attn_sample(q_cat, ckv_past, kr_past, kvin, wuk_t, wuv, l, prev, row0, t):
    _, bs, past, _ = ckv_past.shape
    rb = row0 // t
    return pl.pallas_call(
        functools.partial(_attn_sample_kernel, t=t),
        out_shape=jax.ShapeDtypeStruct(prev.shape, prev.dtype),
        grid=(bs,),
        in_specs=[pl.BlockSpec((t, MLA_HEADS * QK_PAD), lambda b: (rb + b, 0)),
                  pl.BlockSpec((None, None, past, KV_LORA), lambda b: (l, b, 0, 0)),
                  pl.BlockSpec((None, None, past, ROPE_DIM), lambda b: (l, b, 0, 0)),
                  pl.BlockSpec((t, KV_IN), lambda b: (rb + b, 0)),
                  pl.BlockSpec((None, MLA_HEADS, NOPE_DIM, KV_LORA), lambda b: (l, 0, 0, 0)),
                  pl.BlockSpec((None, MLA_HEADS, KV_LORA, V_DIM), lambda b: (l, 0, 0, 0)),
                  pl.BlockSpec(memory_space=pl.ANY)],
        out_specs=pl.BlockSpec((t, MLA_HEADS * V_DIM), lambda b: (rb + b, 0)),
        scratch_shapes=[pltpu.VMEM((MLA_HEADS * t, KV_IN), BF16),
                        pltpu.VMEM((past, KV_IN), BF16),
                        pltpu.VMEM((MLA_HEADS * t, KV_LORA), BF16)],
        input_output_aliases={6: 0},
        compiler_params=_cparams("parallel"),
        name="attn_sample",
    )(q_cat, ckv_past, kr_past, kvin, wuk_t, wuv, prev)


def _ret_consts():
    lg = np.log1p(-(2.0 ** (-5.0 - np.arange(RET_HEADS, dtype=np.float64))))
    idx = np.arange(CHUNK, dtype=np.float64)
    diff = idx[:, None] - idx[None, :]
    dmask = np.where(diff[None] >= 0, np.exp(np.maximum(diff, 0.0)[None] * lg[:, None, None]), 0.0)
    qd = np.exp((idx + 1.0)[None, :] * lg[:, None])
    kd = np.exp((CHUNK - 1.0 - idx)[None, :] * lg[:, None])
    g = np.exp(CHUNK * lg)
    qd = np.broadcast_to(qd[:, :, None], (RET_HEADS, CHUNK, RET_QK_DIM))
    kd = np.broadcast_to(kd[:, :, None], (RET_HEADS, CHUNK, RET_QK_DIM))
    g = np.broadcast_to(g[:, None, None], (RET_HEADS, 1, RET_V_DIM))
    return (jnp.asarray(dmask, F32), jnp.asarray(qd, F32), jnp.asarray(kd, F32), jnp.asarray(g, F32))


def _retention_kernel(q_ref, k_ref, v_ref, g_ref, cos_ref, sin_ref, st0_ref, dm_ref, qd_ref, kd_ref,
                      gam_ref, *rest, cps, hps):
    o_ref, st_ref, state = rest[-3:]
    step = pl.program_id(2)
    nt = (((1,), (1,)), ((), ()))

    @pl.when(step == 0)
    def _():
        state[...] = st0_ref[...]

    for c in range(cps):
        r = slice(c * CHUNK, (c + 1) * CHUNK)
        cos, sin = cos_ref[r, :], sin_ref[r, :]
        for hh in range(hps):
            qk = slice(hh * RET_QK_DIM, (hh + 1) * RET_QK_DIM)
            vv = slice(hh * RET_V_DIM, (hh + 1) * RET_V_DIM)
            qf = q_ref[r, qk].astype(F32)
            kf = k_ref[r, qk].astype(F32)
            q = qf * cos + pltpu.roll(qf, 64, 1) * sin
            k = (kf * cos + pltpu.roll(kf, 64, 1) * sin) * (RET_QK_DIM ** -0.5)
            v = v_ref[r, vv]
            st = state[hh]
            inner = lax.dot_general(q.astype(BF16), k.astype(BF16), nt, preferred_element_type=F32) * dm_ref[hh]
            o = (jnp.dot(inner.astype(BF16), v, preferred_element_type=F32)
                 + jnp.dot((q * qd_ref[hh]).astype(BF16), st.astype(BF16), preferred_element_type=F32))
            kt = jnp.transpose(k * kd_ref[hh]).astype(BF16)
            state[hh] = st * gam_ref[hh] + jnp.dot(kt, v, preferred_element_type=F32)
            mu = jnp.mean(o, axis=-1, keepdims=True)
            oc = o - mu
            var = jnp.mean(oc * oc, axis=-1, keepdims=True)
            gate = g_ref[r, vv].astype(F32)
            o_ref[r, vv] = (gate * jax.nn.sigmoid(gate) * (oc * lax.rsqrt(var + EPS))).astype(o_ref.dtype)

    @pl.when(step == pl.num_programs(2) - 1)
    def _():
        st_ref[0] = state[...]


def retention(big, cos, sin, state0, l, consts, row0, nseq, seq_len, cps, hps, prev=None):
    r = cps * CHUNK
    steps = seq_len // r
    rb0 = row0 // r
    ng = RET_HEADS // hps
    dmask, qd, kd, gam = consts
    rowblk = lambda s, h, t: rb0 + s * steps + t
    hconst = lambda s, h, t: (h, 0, 0)
    in_specs = [pl.BlockSpec((r, hps * RET_QK_DIM), lambda s, h, t: (rowblk(s, h, t), h)),
                pl.BlockSpec((r, hps * RET_QK_DIM), lambda s, h, t: (rowblk(s, h, t), ng + h)),
                pl.BlockSpec((r, hps * RET_V_DIM), lambda s, h, t: (rowblk(s, h, t), ng + h)),
                pl.BlockSpec((r, hps * RET_V_DIM), lambda s, h, t: (rowblk(s, h, t), 2 * ng + h)),
                pl.BlockSpec((r, RET_QK_DIM), lambda s, h, t: (rowblk(s, h, t), 0)),
                pl.BlockSpec((r, RET_QK_DIM), lambda s, h, t: (rowblk(s, h, t), 0)),
                pl.BlockSpec((None, None, hps, RET_QK_DIM, RET_V_DIM), lambda s, h, t: (l, s, h, 0, 0)),
                pl.BlockSpec((hps, CHUNK, CHUNK), hconst),
                pl.BlockSpec((hps, CHUNK, RET_QK_DIM), hconst),
                pl.BlockSpec((hps, CHUNK, RET_QK_DIM), hconst),
                pl.BlockSpec((hps, 1, RET_V_DIM), hconst)]
    args = [big, big, big, big, cos, sin, state0, dmask, qd, kd, gam]
    aliases = {}
    if prev is not None:
        in_specs.append(pl.BlockSpec(memory_space=pl.ANY))
        args.append(prev)
        aliases = {len(args) - 1: 0}
    return pl.pallas_call(
        functools.partial(_retention_kernel, cps=cps, hps=hps),
        out_shape=(jax.ShapeDtypeStruct((big.shape[0], RET_HEADS * RET_V_DIM), BF16),
                   jax.ShapeDtypeStruct((nseq, RET_HEADS, RET_QK_DIM, RET_V_DIM), F32)),
        grid=(nseq, ng, steps),
        in_specs=in_specs,
        out_specs=(pl.BlockSpec((r, hps * RET_V_DIM), lambda s, h, t: (rowblk(s, h, t), h)),
                   pl.BlockSpec((1, hps, RET_QK_DIM, RET_V_DIM), lambda s, h, t: (s, h, 0, 0))),
        scratch_shapes=[pltpu.VMEM((hps, RET_QK_DIM, RET_V_DIM), F32)],
        input_output_aliases=aliases,
        compiler_params=_cparams("parallel", "parallel", "arbitrary"),
        name="retention",
    )(*args)


_CAND = [(a, b) for a in range(PEER_TOPK) for b in range(PEER_TOPK) if (a + 1) * (b + 1) <= PEER_TOPK]


def _top_distinct(s):
    vals, cnts = [], []
    for _ in range(PEER_TOPK):
        m = jnp.max(s, axis=0, keepdims=True)
        hit = s == m
        vals.append(m)
        cnts.append(jnp.sum(jnp.where(hit, 1.0, 0.0), axis=0, keepdims=True))
        s = jnp.where(hit, -jnp.inf, s)
    return vals, cnts


def _route_kernel(qt_ref, k1_ref, k2_ref, thr_ref, e1_ref, s2_ref, e2_ref, cand_ref, mult_ref):
    tt = qt_ref.shape[1]
    pad = cand_ref.shape[0] - len(_CAND)
    cand_ref[len(_CAND):, :] = jnp.full((pad, LANES), -jnp.inf, F32)
    mult_ref[len(_CAND):, :] = jnp.zeros((pad, LANES), F32)
    for h, c in [(h, c) for h in range(PEER_HEADS) for c in range(tt // LANES)]:
        lanes = slice(c * LANES, (c + 1) * LANES)
        q1 = qt_ref[h * 2 * PEER_HALF:h * 2 * PEER_HALF + PEER_HALF, lanes].astype(BF16)
        q2 = qt_ref[h * 2 * PEER_HALF + PEER_HALF:(h + 1) * 2 * PEER_HALF, lanes].astype(BF16)
        s1 = jnp.dot(k1_ref[...], q1, preferred_element_type=F32)
        s2 = jnp.dot(k2_ref[...], q2, preferred_element_type=F32)
        v1, c1 = _top_distinct(s1)
        v2, c2 = _top_distinct(s2)
        for r, (a, b) in enumerate(_CAND):
            cand_ref[r:r + 1, :] = v1[a] + v2[b]
            mult_ref[r:r + 1, :] = c1[a] * c2[b]
        cand = cand_ref[...]
        mult = mult_ref[...]
        rest = cand
        cum = jnp.zeros_like(v1[0])
        tau = v1[0] + v2[0]
        for _ in range(PEER_TOPK):
            m = jnp.max(rest, axis=0, keepdims=True)
            hit = rest == m
            tau = jnp.where(cum < PEER_TOPK, m, tau)
            cum = cum + jnp.sum(jnp.where(hit, mult, 0.0), axis=0, keepdims=True)
            rest = jnp.where(hit, -jnp.inf, rest)
        top = v1[0] + v2[0]
        z = jnp.sum(jnp.where(cand >= tau, mult * jnp.exp(cand - top), 0.0), axis=0, keepdims=True)
        thr = jnp.full_like(s1, jnp.inf)
        for a in range(PEER_TOPK):
            th_a = jnp.full_like(tau, jnp.inf)
            for b in range(PEER_TOPK // (a + 1)):
                th_a = jnp.where(v1[a] + v2[b] >= tau, v2[b], th_a)
            thr = jnp.where(s1 == v1[a], th_a, thr)
        thr_ref[h, :, lanes] = thr
        s2_ref[h, :, lanes] = s2
        e1_ref[h, :, lanes] = 0.5 * jnp.exp(s1 - v1[0])
        e2_ref[h, :, lanes] = jnp.exp(s2 - v2[0]) / z


def peer_route(qt, k1, k2, l, tt=256):
    n = qt.shape[1]
    tt = min(tt, n)
    big = jax.ShapeDtypeStruct((PEER_HEADS, N_KEYS, n), F32)
    bspec = pl.BlockSpec((PEER_HEADS, N_KEYS, tt), lambda i: (0, 0, i))
    return pl.pallas_call(
        _route_kernel,
        out_shape=(big, big, big, big),
        grid=(n // tt,),
        in_specs=[pl.BlockSpec((PEER_HEADS * 2 * PEER_HALF, tt), lambda i: (0, i)),
                  pl.BlockSpec((None, N_KEYS, PEER_HALF), lambda i: (l, 0, 0)),
                  pl.BlockSpec((None, N_KEYS, PEER_HALF), lambda i: (l, 0, 0))],
        out_specs=(bspec, bspec, bspec, bspec),
        scratch_shapes=[pltpu.VMEM((-(-len(_CAND) // SUBLANES) * SUBLANES, LANES), F32)] * 2,
        compiler_params=_cparams("parallel"),
        name="peer_route",
    )(qt, k1, k2)


def _gate_rows(ii, thr_ref, e1_ref, s2_ref, e2_ref):
    w = None
    for h in range(PEER_HEADS):
        c = jnp.where(s2_ref[h] >= thr_ref[h, ii:ii + 1, :], e2_ref[h], 0.0) * e1_ref[h, ii:ii + 1, :]
        w = c if w is None else w + c
    return w


def _peer_expert_kernel(ht_ref, u_ref, vt_ref, thc_ref, e1c_ref, thn_ref, e1n_ref, s2_ref, e2_ref,
                        x_ref, gate_ref, o_ref, acc_ref, wg_ref, w_ref, *, ipt):
    e = pl.program_id(1)
    last = pl.num_programs(1) - 1

    def build(thr_ref, e1_ref, slot):
        for ii in range(ipt):
            w_ref[slot, ii * N_KEYS:(ii + 1) * N_KEYS, :] = _gate_rows(ii, thr_ref, e1_ref, s2_ref, e2_ref)

    @pl.when(e == 0)
    def _():
        acc_ref[...] = jnp.zeros_like(acc_ref)
        build(thc_ref, e1c_ref, 0)

    build(thn_ref, e1n_ref, (e + 1) % 2)
    act = jnp.dot(u_ref[...], ht_ref[...], preferred_element_type=F32)
    gelu2 = act * (1.0 + lax.erf(act * (2.0 ** -0.5)))
    wg_ref[...] = (w_ref[e % 2] * gelu2).astype(BF16)
    acc_ref[...] += jnp.dot(vt_ref[...], wg_ref[...], preferred_element_type=F32)

    @pl.when(e == last)
    def _():
        tt, d = o_ref.shape
        y = jnp.transpose(acc_ref[...]).reshape(tt // GROUP, GROUP, d) * gate_ref[...]
        o_ref[...] = x_ref[...] + y.reshape(tt, d)


def peer_experts(ht, u, vt, l, thr, e1, s2, e2, x, gate_g, tt=512, ipt=8):
    d, n = ht.shape
    ne = u.shape[1]
    tt = min(tt, n)
    te = ipt * N_KEYS
    nlast = ne // te - 1
    full = pl.BlockSpec((PEER_HEADS, N_KEYS, tt), lambda t, e: (0, 0, t))
    cur = pl.BlockSpec((PEER_HEADS, ipt, tt), lambda t, e: (0, e, t))
    nxt = pl.BlockSpec((PEER_HEADS, ipt, tt), lambda t, e: (0, jnp.minimum(e + 1, nlast), t))
    return pl.pallas_call(
        functools.partial(_peer_expert_kernel, ipt=ipt),
        out_shape=jax.ShapeDtypeStruct((n, d), F32),
        grid=(n // tt, ne // te),
        in_specs=[pl.BlockSpec((d, tt), lambda t, e: (0, t)),
                  pl.BlockSpec((None, te, d), lambda t, e: (l, e, 0)),
                  pl.BlockSpec((None, d, te), lambda t, e: (l, 0, e)),
                  cur, cur, nxt, nxt, full, full,
                  pl.BlockSpec((tt, d), lambda t, e: (t, 0), pipeline_mode=pl.Buffered(1)),
                  pl.BlockSpec((tt // GROUP, 1, d), lambda t, e: (t, 0, 0))],
        out_specs=pl.BlockSpec((tt, d), lambda t, e: (t, 0)),
        scratch_shapes=[pltpu.VMEM((d, tt), F32), pltpu.VMEM((te, tt), BF16), pltpu.VMEM((2, te, tt), F32)],
        compiler_params=_cparams("parallel", "arbitrary"),
        name="peer_experts",
    )(ht, u, vt, thr, e1, thr, e1, s2, e2, x, gate_g)


def _rope_tables(pos):
    pos = pos.astype(F32)[:, None]
    half = ROPE_DIM // 2
    inv = ROPE_THETA ** (-jnp.arange(half, dtype=F32) * 2.0 / ROPE_DIM)
    ang = pos * inv[None, :]
    c, s = jnp.cos(ang), jnp.sin(ang)
    z32 = jnp.zeros_like(c)
    z64 = jnp.concatenate([z32, z32], axis=1)
    cos64 = jnp.concatenate([c, c, z64], axis=1)
    sa64 = jnp.concatenate([-s, z32, z64], axis=1)
    sb64 = jnp.concatenate([z32, s, z64], axis=1)
    half = RET_QK_DIM // 2
    inv = ROPE_THETA ** (-jnp.arange(half, dtype=F32) * 2.0 / RET_QK_DIM)
    ang = pos * inv[None, :]
    c, s = jnp.cos(ang), jnp.sin(ang)
    return cos64, sa64, sb64, jnp.concatenate([c, c], axis=1), jnp.concatenate([-s, s], axis=1)


def _prep_weights(w_ada, w_in, w_uq, w_ukv, w_a, w_b, w_o, w_pq, k1, k2, u, v):
    nl = w_in.shape[0]
    o1 = Q_LORA + KV_LORA + ROPE_DIM
    w1 = jnp.pad(w_in[:, :, :o1], ((0, 0), (0, 0), (0, LANES - ROPE_DIM))).astype(BF16)
    w2 = w_in[:, :, o1:].astype(BF16)
    uq = w_uq.reshape(nl, Q_LORA, MLA_HEADS, NOPE_DIM + ROPE_DIM) * (MLA_SCALE * LOG2E)
    uq = jnp.pad(uq, ((0, 0), (0, 0), (0, 0), (0, QK_PAD - NOPE_DIM - ROPE_DIM)))
    uq = uq.reshape(nl, Q_LORA, MLA_HEADS * QK_PAD).astype(BF16)
    wk = jnp.pad(w_ukv[..., :NOPE_DIM], ((0, 0), (0, KV_IN - KV_LORA), (0, 0), (0, QK_PAD - NOPE_DIM)))
    eye = jnp.zeros((KV_IN, QK_PAD), F32).at[KV_LORA + jnp.arange(ROPE_DIM), NOPE_DIM + jnp.arange(ROPE_DIM)].set(1.0)
    wk = (wk + eye[None, :, None, :]).reshape(nl, KV_IN, MLA_HEADS * QK_PAD)
    wv = jnp.pad(w_ukv[..., NOPE_DIM:], ((0, 0), (0, KV_IN - KV_LORA), (0, 0), (0, 0)))
    wv = wv.reshape(nl, KV_IN, MLA_HEADS * V_DIM)
    wkv = jnp.concatenate([wk, wv], axis=2).astype(BF16)
    wuk_t = jnp.transpose(w_ukv[..., :NOPE_DIM], (0, 2, 3, 1)).astype(BF16)
    wuv = jnp.transpose(w_ukv[..., NOPE_DIM:], (0, 2, 1, 3)).astype(BF16)
    return dict(w_ada=w_ada, w1=w1, w2=w2, uq=uq, wkv=wkv, wuk_t=wuk_t, wuv=wuv,
                w_a=w_a.astype(BF16), w_b=w_b.astype(BF16), w_o=w_o.astype(BF16),
                w_pq_t=jnp.swapaxes(w_pq, 1, 2).astype(BF16), k1=k1.astype(BF16), k2=k2.astype(BF16),
                u=u.astype(BF16), v_t=jnp.swapaxes(v, 1, 2).astype(BF16))


def kernel(x_prompt, x_sample, c_prompt, c_sample, cache_ckv, cache_krope, state_ret, w_ada, b_ada, g_mix, g_ffn, w_in, g_q, w_uq, g_kv, w_ukv, w_a, w_b, w_o, w_pq, peer_k1, peer_k2, peer_u, peer_v, g_final):
    bp, s_len, d = x_prompt.shape
    bs, t_s, _ = x_sample.shape
    depth = w_in.shape[0]
    past = cache_ckv.shape[2]
    n_p, n_s = bp * s_len, bs * t_s
    n = n_p + n_s
    assert t_s == GROUP and s_len % GROUP == 0

    x = jnp.concatenate([x_prompt.reshape(n_p, d), x_sample.reshape(n_s, d)], axis=0)
    c_all = jnp.concatenate([c_prompt, c_sample], axis=0)
    gidx = jnp.concatenate([jnp.repeat(jnp.arange(bp), s_len // GROUP), bp + jnp.arange(bs)])
    pos = jnp.concatenate([jnp.tile(jnp.arange(s_len), bp), jnp.tile(past + jnp.arange(t_s), bs)])
    cos64, sa64, sb64, cos128, sin128 = _rope_tables(pos)
    ret_c = _ret_consts()
    w = _prep_weights(w_ada, w_in, w_uq, w_ukv, w_a, w_b, w_o, w_pq, peer_k1, peer_k2, peer_u, peer_v)
    zero_state = jnp.zeros((1, bp, RET_HEADS, RET_QK_DIM, RET_V_DIM), F32)
    silu_c = jax.nn.silu(c_all)
    gspec = lambda tm, tn: pl.BlockSpec((tm // GROUP, 1, tn), lambda i, j: (i, 0, j))
    tile = lambda tm, tn, off=0: pl.BlockSpec((tm, tn), lambda i, j: (i, j + off))
    rope_spec = lambda tm: pl.BlockSpec((tm, LANES), lambda i, j: (i, 0))

    outs = {k: [] for k in ("ckv_p", "kr_p", "st_p", "ckv_s", "kr_s", "st_s")}
    for l in range(depth):
        ada = matmul(silu_c, w["w_ada"], F32, wl=l, name="ada") + b_ada[l][None, :]
        cond = ada.reshape(bp + bs, 6, d)[gidx]
        cg = [cond[:, k, :][:, None, :] for k in range(6)]

        h = modnorm(x, g_mix[l], cg[1], cg[0])
        cqn, ckvn, krr, kvin = inproj1(h, w["w1"], l, g_q[l], g_kv[l], cos64, sa64, sb64)
        big = matmul(h, w["w2"], BF16, tm=1024, tn=1024, wl=l, name="inproj2")
        tmq = 1024
        q_cat = matmul(cqn, w["uq"], BF16, tm=tmq, tn=2048, epilogue=_ep_qrope, wl=l,
                       extras=(cos64, sa64, sb64), extra_specs=[rope_spec(tmq)] * 3, name="q_up")
        kv = matmul(kvin, w["wkv"], BF16, tm=1024, tn=2048, wl=l, m=n_p, name="kv_up")
        o_a = attn_prompt(q_cat, kv, bp, s_len)
        o_a = attn_sample(q_cat, cache_ckv, cache_krope, kvin, w["wuk_t"], w["wuv"], l, o_a, n_p, t_s)
        o_b, st_p = retention(big, cos128, sin128, zero_state, 0, ret_c, 0, bp, s_len, min(8, s_len // CHUNK), 4)
        o_b, st_s = retention(big, cos128, sin128, state_ret, l, ret_c, n_p, bs, t_s, 1, RET_HEADS, prev=o_b)
        mixed = proj_ab(o_a, o_b, w["w_a"], w["w_b"], l, big, 2 * RET_HEADS * RET_QK_DIM + 2 * RET_HEADS * RET_V_DIM)
        tm, tn = 1024, 1024
        x = matmul(mixed, w["w_o"], F32, tm=tm, tn=tn, epilogue=_ep_resid, wl=l,
                   extras=(x, cg[2]), extra_specs=[tile(tm, tn), gspec(tm, tn)], name="proj_o")

        outs["ckv_p"].append(ckvn[:n_p].reshape(bp, s_len, KV_LORA))
        outs["kr_p"].append(krr[:n_p, :ROPE_DIM].reshape(bp, s_len, ROPE_DIM))
        outs["st_p"].append(st_p)
        outs["ckv_s"].append(ckvn[n_p:].reshape(bs, t_s, KV_LORA))
        outs["kr_s"].append(krr[n_p:, :ROPE_DIM].reshape(bs, t_s, ROPE_DIM))
        outs["st_s"].append(st_s)

        h2t = modnorm(x, g_ffn[l], cg[4], cg[3], transposed=True)
        qt = matmul(w["w_pq_t"], h2t, F32, tm=2048, tn=512, xl=l, name="peer_q")
        thr, e1, s2, e2 = peer_route(qt, w["k1"], w["k2"], l)
        x = peer_experts(h2t, w["u"], w["v_t"], l, thr, e1, s2, e2, x, cg[5])

    y_p = final_rmsnorm(x, g_final, 0, n_p)
    y_s = final_rmsnorm(x, g_final, n_p, n_s)
    st = lambda k: jnp.stack(outs[k], axis=0)
    return (y_p.reshape(bp, s_len, d), y_s.reshape(bs, t_s, d),
            st("ckv_p"), st("kr_p"), st("st_p"), st("ckv_s"), st("kr_s"), st("st_s"))
```

```python
import functools

import numpy as np
import jax
import jax.numpy as jnp
from jax import lax
from jax.experimental import pallas as pl
from jax.experimental.pallas import tpu as pltpu

F32 = jnp.float32
BF16 = jnp.bfloat16

D_MODEL = 2048
CHUNK = 64
EPS = 1e-6
ROPE_THETA = 10000.0
MLA_HEADS = 16
Q_LORA = 512
KV_LORA = 512
NOPE_DIM = 128
ROPE_DIM = 64
V_DIM = 128
MLA_SCALE = (NOPE_DIM + ROPE_DIM) ** -0.5
LOG2E = 1.4426950408889634
QK_PAD = 256
KV_IN = KV_LORA + 128
RET_HEADS = 8
RET_QK_DIM = 128
RET_V_DIM = 256
PEER_HEADS = 8
N_KEYS = 128
PEER_HALF = 128
PEER_TOPK = 16
GROUP = 64
LANES = 128
SUBLANES = 8

VMEM_LIMIT = 60 * 1024 * 1024


def _cparams(*sem):
    return pltpu.CompilerParams(dimension_semantics=sem, vmem_limit_bytes=VMEM_LIMIT)


def _modnorm_kernel(x_ref, g_ref, sc_ref, sh_ref, o_ref, *, transposed):
    x = x_ref[...]
    tr, d = x.shape
    y = x * lax.rsqrt(jnp.mean(x * x, axis=-1, keepdims=True) + EPS) * g_ref[...]
    y = (y.reshape(tr // GROUP, GROUP, d) * (1.0 + sc_ref[...]) + sh_ref[...]).reshape(tr, d)
    if transposed:
        y = jnp.transpose(y)
    o_ref[...] = y.astype(o_ref.dtype)


def modnorm(x, g, scale_g, shift_g, tr=512, transposed=False):
    n, d = x.shape
    tr = min(tr, n)
    gpt = tr // GROUP
    return pl.pallas_call(
        functools.partial(_modnorm_kernel, transposed=transposed),
        out_shape=jax.ShapeDtypeStruct((d, n) if transposed else (n, d), BF16),
        grid=(n // tr,),
        in_specs=[pl.BlockSpec((tr, d), lambda i: (i, 0)),
                  pl.BlockSpec((1, d), lambda i: (0, 0)),
                  pl.BlockSpec((gpt, 1, d), lambda i: (i, 0, 0)),
                  pl.BlockSpec((gpt, 1, d), lambda i: (i, 0, 0))],
        out_specs=(pl.BlockSpec((d, tr), lambda i: (0, i)) if transposed
                   else pl.BlockSpec((tr, d), lambda i: (i, 0))),
        compiler_params=_cparams("parallel"),
        name="modnorm",
    )(x, g.reshape(1, d), scale_g, shift_g)


def _rmsnorm_kernel(x_ref, g_ref, o_ref):
    x = x_ref[...]
    o_ref[...] = x * lax.rsqrt(jnp.mean(x * x, axis=-1, keepdims=True) + EPS) * g_ref[...]


def final_rmsnorm(x, g, row0, nrows, tr=512):
    d = x.shape[1]
    tr = min(tr, nrows)
    rb = row0 // tr
    return pl.pallas_call(
        _rmsnorm_kernel,
        out_shape=jax.ShapeDtypeStruct((nrows, d), F32),
        grid=(nrows // tr,),
        in_specs=[pl.BlockSpec((tr, d), lambda i: (rb + i, 0)),
                  pl.BlockSpec((1, d), lambda i: (0, 0))],
        out_specs=pl.BlockSpec((tr, d), lambda i: (i, 0)),
        compiler_params=_cparams("parallel"),
        name="final_rmsnorm",
    )(x, g.reshape(1, d))


def _mm_kernel(x_ref, w_ref, *rest, epilogue):
    extras, o_ref = rest[:-1], rest[-1]
    acc = jnp.dot(x_ref[...].astype(BF16), w_ref[...].astype(BF16), preferred_element_type=F32)
    if epilogue is not None:
        acc = epilogue(acc, *[e[...] for e in extras])
    o_ref[...] = acc.astype(o_ref.dtype)


def matmul(x, w, out_dtype, tm=1024, tn=512, epilogue=None, extras=(), extra_specs=(), name="matmul",
           xl=None, wl=None, m=None):
    k = x.shape[-1]
    m = x.shape[-2] if m is None else m
    n = w.shape[-1]
    tm, tn = min(tm, m), min(tn, n)
    assert m % tm == 0 and n % tn == 0, (m, tm, n, tn)
    xspec = (pl.BlockSpec((tm, k), lambda i, j: (i, 0)) if xl is None
             else pl.BlockSpec((None, tm, k), lambda i, j: (xl, i, 0)))
    wspec = (pl.BlockSpec((k, tn), lambda i, j: (0, j)) if wl is None
             else pl.BlockSpec((None, k, tn), lambda i, j: (wl, 0, j)))
    return pl.pallas_call(
        functools.partial(_mm_kernel, epilogue=epilogue),
        out_shape=jax.ShapeDtypeStruct((m, n), out_dtype),
        grid=(m // tm, n // tn),
        in_specs=[xspec, wspec] + list(extra_specs),
        out_specs=pl.BlockSpec((tm, tn), lambda i, j: (i, j)),
        compiler_params=_cparams("parallel", "arbitrary"),
        name=name,
    )(x, w, *extras)


def _rope_half(x, cos, sa, sb):
    return x * cos + pltpu.roll(x, 96, 1) * sa + pltpu.roll(x, 32, 1) * sb


def _ep_qrope(acc, cos, sa, sb):
    parts = []
    for h in range(acc.shape[1] // QK_PAD):
        parts.append(acc[:, h * QK_PAD:h * QK_PAD + NOPE_DIM])
        parts.append(_rope_half(acc[:, h * QK_PAD + NOPE_DIM:(h + 1) * QK_PAD], cos, sa, sb))
    return jnp.concatenate(parts, axis=1)


def _proj_ab_kernel(oa_ref, ob_ref, wa_ref, wb_ref, ga_ref, gb_ref, o_ref):
    ya = jnp.dot(oa_ref[...], wa_ref[...], preferred_element_type=F32)
    yb = jnp.dot(ob_ref[...], wb_ref[...], preferred_element_type=F32)
    o_ref[...] = (jax.nn.sigmoid(ga_ref[...].astype(F32)) * ya
                  + jax.nn.sigmoid(gb_ref[...].astype(F32)) * yb).astype(o_ref.dtype)


def proj_ab(o_a, o_b, w_a, w_b, l, big, gate_col0, tm=1024, tn=1024):
    m, k = o_a.shape
    n = w_a.shape[-1]
    tm = min(tm, m)
    goff = gate_col0 // tn
    xspec = pl.BlockSpec((tm, k), lambda i, j: (i, 0))
    wspec = pl.BlockSpec((None, k, tn), lambda i, j: (l, 0, j))
    return pl.pallas_call(
        _proj_ab_kernel,
        out_shape=jax.ShapeDtypeStruct((m, n), BF16),
        grid=(m // tm, n // tn),
        in_specs=[xspec, xspec, wspec, wspec,
                  pl.BlockSpec((tm, tn), lambda i, j: (i, goff + j)),
                  pl.BlockSpec((tm, tn), lambda i, j: (i, goff + n // tn + j))],
        out_specs=pl.BlockSpec((tm, tn), lambda i, j: (i, j)),
        compiler_params=_cparams("parallel", "arbitrary"),
        name="proj_ab",
    )(o_a, o_b, w_a, w_b, big, big)


def _ep_resid(acc, x, gate):
    tm, tn = acc.shape
    y = acc.reshape(tm // GROUP, GROUP, tn) * gate
    return x + y.reshape(tm, tn)


def _inproj1_kernel(h_ref, w_ref, gq_ref, gkv_ref, cos_ref, sa_ref, sb_ref,
                    cq_ref, ckv_ref, kr_ref, kvin_ref):
    acc = jnp.dot(h_ref[...], w_ref[...], preferred_element_type=F32)
    cq = acc[:, :Q_LORA]
    ckv = acc[:, Q_LORA:Q_LORA + KV_LORA]
    kr = acc[:, Q_LORA + KV_LORA:]
    cqn = cq * lax.rsqrt(jnp.mean(cq * cq, axis=-1, keepdims=True) + EPS) * gq_ref[...]
    ckvn = ckv * lax.rsqrt(jnp.mean(ckv * ckv, axis=-1, keepdims=True) + EPS) * gkv_ref[...]
    krr = _rope_half(kr, cos_ref[...], sa_ref[...], sb_ref[...])
    cq_ref[...] = cqn.astype(BF16)
    ckv_ref[...] = ckvn
    kr_ref[...] = krr
    kvin_ref[:, :KV_LORA] = ckvn.astype(BF16)
    kvin_ref[:, KV_LORA:] = krr.astype(BF16)


def inproj1(h, w1, l, g_q, g_kv, cos, sa, sb, tm=512):
    n, k = h.shape
    tm = min(tm, n)
    w1n = w1.shape[-1]
    row = lambda i: (i, 0)
    fix = lambda i: (0, 0)
    return pl.pallas_call(
        _inproj1_kernel,
        out_shape=(jax.ShapeDtypeStruct((n, Q_LORA), BF16),
                   jax.ShapeDtypeStruct((n, KV_LORA), F32),
                   jax.ShapeDtypeStruct((n, LANES), F32),
                   jax.ShapeDtypeStruct((n, KV_IN), BF16)),
        grid=(n // tm,),
        in_specs=[pl.BlockSpec((tm, k), row), pl.BlockSpec((None, k, w1n), lambda i: (l, 0, 0)),
                  pl.BlockSpec((1, Q_LORA), fix), pl.BlockSpec((1, KV_LORA), fix),
                  pl.BlockSpec((tm, LANES), row), pl.BlockSpec((tm, LANES), row), pl.BlockSpec((tm, LANES), row)],
        out_specs=(pl.BlockSpec((tm, Q_LORA), row), pl.BlockSpec((tm, KV_LORA), row),
                   pl.BlockSpec((tm, LANES), row), pl.BlockSpec((tm, KV_IN), row)),
        compiler_params=_cparams("parallel"),
        name="inproj1",
    )(h, w1, g_q.reshape(1, -1), g_kv.reshape(1, -1), cos, sa, sb)


def _attn_prompt_kernel(q_ref, k_ref, v_ref, o_ref, *, tq, tk, nh):
    qi = pl.program_id(2)
    nt = (((1,), (1,)), ((), ()))
    qs = [q_ref[:, h * QK_PAD:(h + 1) * QK_PAD] for h in range(nh)]

    def scores(start, size, h):
        k = k_ref[pl.ds(start, size), h * QK_PAD:(h + 1) * QK_PAD]
        v = v_ref[pl.ds(start, size), h * V_DIM:(h + 1) * V_DIM]
        return lax.dot_general(qs[h], k, nt, preferred_element_type=F32), v

    def update(carry, s, v):
        m, l, acc = carry
        m_new = jnp.maximum(m, jnp.max(s, axis=-1, keepdims=True))
        alpha = jnp.exp2(m - m_new)
        p = jnp.exp2(s - m_new)
        l = alpha * l + jnp.sum(p, axis=-1, keepdims=True)
        acc = alpha * acc + jnp.dot(p.astype(BF16), v, preferred_element_type=F32)
        return m_new, l, acc

    def big_body(j, carry):
        start = pl.multiple_of(j * tk, tk)
        return tuple(update(carry[h], *scores(start, tk, h)) for h in range(nh))

    q0 = qi * tq
    nbig = q0 // tk

    def small_body(i, carry):
        start = pl.multiple_of(nbig * tk + i * tq, tq)
        return tuple(update(carry[h], *scores(start, tq, h)) for h in range(nh))

    init = tuple((jnp.full((tq, 1), -1e30, F32), jnp.zeros((tq, 1), F32), jnp.zeros((tq, V_DIM), F32))
                 for _ in range(nh))
    carry = lax.fori_loop(0, nbig, big_body, init)
    carry = lax.fori_loop(0, (q0 - nbig * tk) // tq, small_body, carry)
    rows = lax.broadcasted_iota(jnp.int32, (tq, tq), 0) // CHUNK
    cols = lax.broadcasted_iota(jnp.int32, (tq, tq), 1) // CHUNK
    for h in range(nh):
        s, v = scores(pl.multiple_of(q0, tq), tq, h)
        s = jnp.where(cols <= rows, s, -1e30)
        m, l, acc = update(carry[h], s, v)
        o_ref[:, h * V_DIM:(h + 1) * V_DIM] = (acc / l).astype(o_ref.dtype)


def attn_prompt(q_cat, kv, bp, s_len, tq=512, tk=1024, nh=4):
    tq = min(tq, s_len)
    tk = min(tk, s_len)
    nq = s_len // tq
    vcol0 = MLA_HEADS * QK_PAD // (V_DIM * nh)
    return pl.pallas_call(
        functools.partial(_attn_prompt_kernel, tq=tq, tk=tk, nh=nh),
        out_shape=jax.ShapeDtypeStruct((q_cat.shape[0], MLA_HEADS * V_DIM), BF16),
        grid=(bp, MLA_HEADS // nh, nq),
        in_specs=[pl.BlockSpec((tq, QK_PAD * nh), lambda b, h, i: (b * nq + i, h)),
                  pl.BlockSpec((s_len, QK_PAD * nh), lambda b, h, i: (b, h)),
                  pl.BlockSpec((s_len, V_DIM * nh), lambda b, h, i: (b, vcol0 + h))],
        out_specs=pl.BlockSpec((tq, V_DIM * nh), lambda b, h, i: (b * nq + i, h)),
        compiler_params=_cparams("parallel", "parallel", "arbitrary"),
        name="attn_prompt",
    )(q_cat, kv, kv)


def _attn_sample_kernel(q_ref, ckv_ref, kr_ref, new_ref, wuk_ref, wuv_ref, prev_ref, o_ref,
                        qall_ref, kall_ref, olat_ref, *, t):
    del prev_ref
    nt = (((1,), (1,)), ((), ()))
    past = ckv_ref.shape[0]
    for h in range(MLA_HEADS):
        qn = q_ref[:, h * QK_PAD:h * QK_PAD + NOPE_DIM]
        qlat = jnp.dot(qn, wuk_ref[h], preferred_element_type=F32)
        qall_ref[h * t:(h + 1) * t, :KV_LORA] = qlat.astype(BF16)
        qall_ref[h * t:(h + 1) * t, KV_LORA:] = q_ref[:, h * QK_PAD + NOPE_DIM:(h + 1) * QK_PAD]
    kall_ref[:, :KV_LORA] = ckv_ref[...].astype(BF16)
    kall_ref[:, KV_LORA:KV_LORA + ROPE_DIM] = kr_ref[...].astype(BF16)
    kall_ref[:, KV_LORA + ROPE_DIM:] = jnp.zeros((past, KV_IN - KV_LORA - ROPE_DIM), BF16)
    knew = new_ref[...]
    half = MLA_HEADS * t // 2
    for r0 in (0, half):
        qall = qall_ref[r0:r0 + half, :]
        s_past = lax.dot_general(qall, kall_ref[...], nt, preferred_element_type=F32)
        s_new = lax.dot_general(qall, knew, nt, preferred_element_type=F32)
        m = jnp.maximum(jnp.max(s_past, axis=-1, keepdims=True), jnp.max(s_new, axis=-1, keepdims=True))
        p_past = jnp.exp2(s_past - m)
        p_new = jnp.exp2(s_new - m)
        l = jnp.sum(p_past, axis=-1, keepdims=True) + jnp.sum(p_new, axis=-1, keepdims=True)
        olat = (jnp.dot(p_past.astype(BF16), kall_ref[:, :KV_LORA], preferred_element_type=F32)
                + jnp.dot(p_new.astype(BF16), knew[:, :KV_LORA], preferred_element_type=F32))
        olat_ref[r0:r0 + half, :] = (olat / l).astype(BF16)
    for h in range(MLA_HEADS):
        o = jnp.dot(olat_ref[h * t:(h + 1) * t, :], wuv_ref[h], preferred_element_type=F32)
        o_ref[:, h * V_DIM:(h + 1) * V_DIM] = o.astype(o_ref.dtype)


def attn_sample(q_cat, ckv_past, kr_past, kvin, wuk_t, wuv, l, prev, row0, t):
    _, bs, past, _ = ckv_past.shape
    rb = row0 // t
    return pl.pallas_call(
        functools.partial(_attn_sample_kernel, t=t),
        out_shape=jax.ShapeDtypeStruct(prev.shape, prev.dtype),
        grid=(bs,),
        in_specs=[pl.BlockSpec((t, MLA_HEADS * QK_PAD), lambda b: (rb + b, 0)),
                  pl.BlockSpec((None, None, past, KV_LORA), lambda b: (l, b, 0, 0)),
                  pl.BlockSpec((None, None, past, ROPE_DIM), lambda b: (l, b, 0, 0)),
                  pl.BlockSpec((t, KV_IN), lambda b: (rb + b, 0)),
                  pl.BlockSpec((None, MLA_HEADS, NOPE_DIM, KV_LORA), lambda b: (l, 0, 0, 0)),
                  pl.BlockSpec((None, MLA_HEADS, KV_LORA, V_DIM), lambda b: (l, 0, 0, 0)),
                  pl.BlockSpec(memory_space=pl.ANY)],
        out_specs=pl.BlockSpec((t, MLA_HEADS * V_DIM), lambda b: (rb + b, 0)),
        scratch_shapes=[pltpu.VMEM((MLA_HEADS * t, KV_IN), BF16),
                        pltpu.VMEM((past, KV_IN), BF16),
                        pltpu.VMEM((MLA_HEADS * t, KV_LORA), BF16)],
        input_output_aliases={6: 0},
        compiler_params=_cparams("parallel"),
        name="attn_sample",
    )(q_cat, ckv_past, kr_past, kvin, wuk_t, wuv, prev)


def _ret_consts():
    lg = np.log1p(-(2.0 ** (-5.0 - np.arange(RET_HEADS, dtype=np.float64))))
    idx = np.arange(CHUNK, dtype=np.float64)
    diff = idx[:, None] - idx[None, :]
    dmask = np.where(diff[None] >= 0, np.exp(np.maximum(diff, 0.0)[None] * lg[:, None, None]), 0.0)
    qd = np.exp((idx + 1.0)[None, :] * lg[:, None])
    kd = np.exp((CHUNK - 1.0 - idx)[None, :] * lg[:, None])
    g = np.exp(CHUNK * lg)
    qd = np.broadcast_to(qd[:, :, None], (RET_HEADS, CHUNK, RET_QK_DIM))
    kd = np.broadcast_to(kd[:, :, None], (RET_HEADS, CHUNK, RET_QK_DIM))
    g = np.broadcast_to(g[:, None, None], (RET_HEADS, 1, RET_V_DIM))
    return (jnp.asarray(dmask, F32), jnp.asarray(qd, F32), jnp.asarray(kd, F32), jnp.asarray(g, F32))


def _retention_kernel(q_ref, k_ref, v_ref, g_ref, cos_ref, sin_ref, st0_ref, dm_ref, qd_ref, kd_ref,
                      gam_ref, *rest, cps, hps):
    o_ref, st_ref, state = rest[-3:]
    step = pl.program_id(2)
    nt = (((1,), (1,)), ((), ()))

    @pl.when(step == 0)
    def _():
        state[...] = st0_ref[...]

    for c in range(cps):
        r = slice(c * CHUNK, (c + 1) * CHUNK)
        cos, sin = cos_ref[r, :], sin_ref[r, :]
        for hh in range(hps):
            qk = slice(hh * RET_QK_DIM, (hh + 1) * RET_QK_DIM)
            vv = slice(hh * RET_V_DIM, (hh + 1) * RET_V_DIM)
            qf = q_ref[r, qk].astype(F32)
            kf = k_ref[r, qk].astype(F32)
            q = qf * cos + pltpu.roll(qf, 64, 1) * sin
            k = (kf * cos + pltpu.roll(kf, 64, 1) * sin) * (RET_QK_DIM ** -0.5)
            v = v_ref[r, vv]
            st = state[hh]
            inner = lax.dot_general(q.astype(BF16), k.astype(BF16), nt, preferred_element_type=F32) * dm_ref[hh]
            o = (jnp.dot(inner.astype(BF16), v, preferred_element_type=F32)
                 + jnp.dot((q * qd_ref[hh]).astype(BF16), st.astype(BF16), preferred_element_type=F32))
            kt = jnp.transpose(k * kd_ref[hh]).astype(BF16)
            state[hh] = st * gam_ref[hh] + jnp.dot(kt, v, preferred_element_type=F32)
            mu = jnp.mean(o, axis=-1, keepdims=True)
            oc = o - mu
            var = jnp.mean(oc * oc, axis=-1, keepdims=True)
            gate = g_ref[r, vv].astype(F32)
            o_ref[r, vv] = (gate * jax.nn.sigmoid(gate) * (oc * lax.rsqrt(var + EPS))).astype(o_ref.dtype)

    @pl.when(step == pl.num_programs(2) - 1)
    def _():
        st_ref[0] = state[...]


def retention(big, cos, sin, state0, l, consts, row0, nseq, seq_len, cps, hps, prev=None):
    r = cps * CHUNK
    steps = seq_len // r
    rb0 = row0 // r
    ng = RET_HEADS // hps
    dmask, qd, kd, gam = consts
    rowblk = lambda s, h, t: rb0 + s * steps + t
    hconst = lambda s, h, t: (h, 0, 0)
    in_specs = [pl.BlockSpec((r, hps * RET_QK_DIM), lambda s, h, t: (rowblk(s, h, t), h)),
                pl.BlockSpec((r, hps * RET_QK_DIM), lambda s, h, t: (rowblk(s, h, t), ng + h)),
                pl.BlockSpec((r, hps * RET_V_DIM), lambda s, h, t: (rowblk(s, h, t), ng + h)),
                pl.BlockSpec((r, hps * RET_V_DIM), lambda s, h, t: (rowblk(s, h, t), 2 * ng + h)),
                pl.BlockSpec((r, RET_QK_DIM), lambda s, h, t: (rowblk(s, h, t), 0)),
                pl.BlockSpec((r, RET_QK_DIM), lambda s, h, t: (rowblk(s, h, t), 0)),
                pl.BlockSpec((None, None, hps, RET_QK_DIM, RET_V_DIM), lambda s, h, t: (l, s, h, 0, 0)),
                pl.BlockSpec((hps, CHUNK, CHUNK), hconst),
                pl.BlockSpec((hps, CHUNK, RET_QK_DIM), hconst),
                pl.BlockSpec((hps, CHUNK, RET_QK_DIM), hconst),
                pl.BlockSpec((hps, 1, RET_V_DIM), hconst)]
    args = [big, big, big, big, cos, sin, state0, dmask, qd, kd, gam]
    aliases = {}
    if prev is not None:
        in_specs.append(pl.BlockSpec(memory_space=pl.ANY))
        args.append(prev)
        aliases = {len(args) - 1: 0}
    return pl.pallas_call(
        functools.partial(_retention_kernel, cps=cps, hps=hps),
        out_shape=(jax.ShapeDtypeStruct((big.shape[0], RET_HEADS * RET_V_DIM), BF16),
                   jax.ShapeDtypeStruct((nseq, RET_HEADS, RET_QK_DIM, RET_V_DIM), F32)),
        grid=(nseq, ng, steps),
        in_specs=in_specs,
        out_specs=(pl.BlockSpec((r, hps * RET_V_DIM), lambda s, h, t: (rowblk(s, h, t), h)),
                   pl.BlockSpec((1, hps, RET_QK_DIM, RET_V_DIM), lambda s, h, t: (s, h, 0, 0))),
        scratch_shapes=[pltpu.VMEM((hps, RET_QK_DIM, RET_V_DIM), F32)],
        input_output_aliases=aliases,
        compiler_params=_cparams("parallel", "parallel", "arbitrary"),
        name="retention",
    )(*args)


_CAND = [(a, b) for a in range(PEER_TOPK) for b in range(PEER_TOPK) if (a + 1) * (b + 1) <= PEER_TOPK]


def _top_distinct(s):
    vals, cnts = [], []
    for _ in range(PEER_TOPK):
        m = jnp.max(s, axis=0, keepdims=True)
        hit = s == m
        vals.append(m)
        cnts.append(jnp.sum(jnp.where(hit, 1.0, 0.0), axis=0, keepdims=True))
        s = jnp.where(hit, -jnp.inf, s)
    return vals, cnts


def _route_kernel(qt_ref, k1_ref, k2_ref, thr_ref, e1_ref, s2_ref, e2_ref, cand_ref, mult_ref):
    tt = qt_ref.shape[1]
    pad = cand_ref.shape[0] - len(_CAND)
    cand_ref[len(_CAND):, :] = jnp.full((pad, LANES), -jnp.inf, F32)
    mult_ref[len(_CAND):, :] = jnp.zeros((pad, LANES), F32)
    for h, c in [(h, c) for h in range(PEER_HEADS) for c in range(tt // LANES)]:
        lanes = slice(c * LANES, (c + 1) * LANES)
        q1 = qt_ref[h * 2 * PEER_HALF:h * 2 * PEER_HALF + PEER_HALF, lanes].astype(BF16)
        q2 = qt_ref[h * 2 * PEER_HALF + PEER_HALF:(h + 1) * 2 * PEER_HALF, lanes].astype(BF16)
        s1 = jnp.dot(k1_ref[...], q1, preferred_element_type=F32)
        s2 = jnp.dot(k2_ref[...], q2, preferred_element_type=F32)
        v1, c1 = _top_distinct(s1)
        v2, c2 = _top_distinct(s2)
        for r, (a, b) in enumerate(_CAND):
            cand_ref[r:r + 1, :] = v1[a] + v2[b]
            mult_ref[r:r + 1, :] = c1[a] * c2[b]
        cand = cand_ref[...]
        mult = mult_ref[...]
        rest = cand
        cum = jnp.zeros_like(v1[0])
        tau = v1[0] + v2[0]
        for _ in range(PEER_TOPK):
            m = jnp.max(rest, axis=0, keepdims=True)
            hit = rest == m
            tau = jnp.where(cum < PEER_TOPK, m, tau)
            cum = cum + jnp.sum(jnp.where(hit, mult, 0.0), axis=0, keepdims=True)
            rest = jnp.where(hit, -jnp.inf, rest)
        top = v1[0] + v2[0]
        z = jnp.sum(jnp.where(cand >= tau, mult * jnp.exp(cand - top), 0.0), axis=0, keepdims=True)
        thr = jnp.full_like(s1, jnp.inf)
        for a in range(PEER_TOPK):
            th_a = jnp.full_like(tau, jnp.inf)
            for b in range(PEER_TOPK // (a + 1)):
                th_a = jnp.where(v1[a] + v2[b] >= tau, v2[b], th_a)
            thr = jnp.where(s1 == v1[a], th_a, thr)
        thr_ref[h, :, lanes] = thr
        s2_ref[h, :, lanes] = s2
        e1_ref[h, :, lanes] = 0.5 * jnp.exp(s1 - v1[0])
        e2_ref[h, :, lanes] = jnp.exp(s2 - v2[0]) / z


def peer_route(qt, k1, k2, l, tt=256):
    n = qt.shape[1]
    tt = min(tt, n)
    big = jax.ShapeDtypeStruct((PEER_HEADS, N_KEYS, n), F32)
    bspec = pl.BlockSpec((PEER_HEADS, N_KEYS, tt), lambda i: (0, 0, i))
    return pl.pallas_call(
        _route_kernel,
        out_shape=(big, big, big, big),
        grid=(n // tt,),
        in_specs=[pl.BlockSpec((PEER_HEADS * 2 * PEER_HALF, tt), lambda i: (0, i)),
                  pl.BlockSpec((None, N_KEYS, PEER_HALF), lambda i: (l, 0, 0)),
                  pl.BlockSpec((None, N_KEYS, PEER_HALF), lambda i: (l, 0, 0))],
        out_specs=(bspec, bspec, bspec, bspec),
        scratch_shapes=[pltpu.VMEM((-(-len(_CAND) // SUBLANES) * SUBLANES, LANES), F32)] * 2,
        compiler_params=_cparams("parallel"),
        name="peer_route",
    )(qt, k1, k2)


def _gate_rows(ii, thr_ref, e1_ref, s2_ref, e2_ref):
    w = None
    for h in range(PEER_HEADS):
        c = jnp.where(s2_ref[h] >= thr_ref[h, ii:ii + 1, :], e2_ref[h], 0.0) * e1_ref[h, ii:ii + 1, :]
        w = c if w is None else w + c
    return w


def _peer_expert_kernel(ht_ref, u_ref, vt_ref, thc_ref, e1c_ref, thn_ref, e1n_ref, s2_ref, e2_ref,
                        x_ref, gate_ref, o_ref, acc_ref, wg_ref, w_ref, *, ipt):
    e = pl.program_id(1)
    last = pl.num_programs(1) - 1

    def build(thr_ref, e1_ref, slot):
        for ii in range(ipt):
            w_ref[slot, ii * N_KEYS:(ii + 1) * N_KEYS, :] = _gate_rows(ii, thr_ref, e1_ref, s2_ref, e2_ref)

    @pl.when(e == 0)
    def _():
        acc_ref[...] = jnp.zeros_like(acc_ref)
        build(thc_ref, e1c_ref, 0)

    build(thn_ref, e1n_ref, (e + 1) % 2)
    act = jnp.dot(u_ref[...], ht_ref[...], preferred_element_type=F32)
    gelu2 = act * (1.0 + lax.erf(act * (2.0 ** -0.5)))
    wg_ref[...] = (w_ref[e % 2] * gelu2).astype(BF16)
    acc_ref[...] += jnp.dot(vt_ref[...], wg_ref[...], preferred_element_type=F32)

    @pl.when(e == last)
    def _():
        tt, d = o_ref.shape
        y = jnp.transpose(acc_ref[...]).reshape(tt // GROUP, GROUP, d) * gate_ref[...]
        o_ref[...] = x_ref[...] + y.reshape(tt, d)


def peer_experts(ht, u, vt, l, thr, e1, s2, e2, x, gate_g, tt=512, ipt=8):
    d, n = ht.shape
    ne = u.shape[1]
    tt = min(tt, n)
    te = ipt * N_KEYS
    nlast = ne // te - 1
    full = pl.BlockSpec((PEER_HEADS, N_KEYS, tt), lambda t, e: (0, 0, t))
    cur = pl.BlockSpec((PEER_HEADS, ipt, tt), lambda t, e: (0, e, t))
    nxt = pl.BlockSpec((PEER_HEADS, ipt, tt), lambda t, e: (0, jnp.minimum(e + 1, nlast), t))
    return pl.pallas_call(
        functools.partial(_peer_expert_kernel, ipt=ipt),
        out_shape=jax.ShapeDtypeStruct((n, d), F32),
        grid=(n // tt, ne // te),
        in_specs=[pl.BlockSpec((d, tt), lambda t, e: (0, t)),
                  pl.BlockSpec((None, te, d), lambda t, e: (l, e, 0)),
                  pl.BlockSpec((None, d, te), lambda t, e: (l, 0, e)),
                  cur, cur, nxt, nxt, full, full,
                  pl.BlockSpec((tt, d), lambda t, e: (t, 0), pipeline_mode=pl.Buffered(1)),
                  pl.BlockSpec((tt // GROUP, 1, d), lambda t, e: (t, 0, 0))],
        out_specs=pl.BlockSpec((tt, d), lambda t, e: (t, 0)),
        scratch_shapes=[pltpu.VMEM((d, tt), F32), pltpu.VMEM((te, tt), BF16), pltpu.VMEM((2, te, tt), F32)],
        compiler_params=_cparams("parallel", "arbitrary"),
        name="peer_experts",
    )(ht, u, vt, thr, e1, thr, e1, s2, e2, x, gate_g)


def _rope_tables(pos):
    pos = pos.astype(F32)[:, None]
    half = ROPE_DIM // 2
    inv = ROPE_THETA ** (-jnp.arange(half, dtype=F32) * 2.0 / ROPE_DIM)
    ang = pos * inv[None, :]
    c, s = jnp.cos(ang), jnp.sin(ang)
    z32 = jnp.zeros_like(c)
    z64 = jnp.concatenate([z32, z32], axis=1)
    cos64 = jnp.concatenate([c, c, z64], axis=1)
    sa64 = jnp.concatenate([-s, z32, z64], axis=1)
    sb64 = jnp.concatenate([z32, s, z64], axis=1)
    half = RET_QK_DIM // 2
    inv = ROPE_THETA ** (-jnp.arange(half, dtype=F32) * 2.0 / RET_QK_DIM)
    ang = pos * inv[None, :]
    c, s = jnp.cos(ang), jnp.sin(ang)
    return cos64, sa64, sb64, jnp.concatenate([c, c], axis=1), jnp.concatenate([-s, s], axis=1)


def _prep_weights(w_ada, w_in, w_uq, w_ukv, w_a, w_b, w_o, w_pq, k1, k2, u, v):
    nl = w_in.shape[0]
    o1 = Q_LORA + KV_LORA + ROPE_DIM
    w1 = jnp.pad(w_in[:, :, :o1], ((0, 0), (0, 0), (0, LANES - ROPE_DIM))).astype(BF16)
    w2 = w_in[:, :, o1:].astype(BF16)
    uq = w_uq.reshape(nl, Q_LORA, MLA_HEADS, NOPE_DIM + ROPE_DIM) * (MLA_SCALE * LOG2E)
    uq = jnp.pad(uq, ((0, 0), (0, 0), (0, 0), (0, QK_PAD - NOPE_DIM - ROPE_DIM)))
    uq = uq.reshape(nl, Q_LORA, MLA_HEADS * QK_PAD).astype(BF16)
    wk = jnp.pad(w_ukv[..., :NOPE_DIM], ((0, 0), (0, KV_IN - KV_LORA), (0, 0), (0, QK_PAD - NOPE_DIM)))
    eye = jnp.zeros((KV_IN, QK_PAD), F32).at[KV_LORA + jnp.arange(ROPE_DIM), NOPE_DIM + jnp.arange(ROPE_DIM)].set(1.0)
    wk = (wk + eye[None, :, None, :]).reshape(nl, KV_IN, MLA_HEADS * QK_PAD)
    wv = jnp.pad(w_ukv[..., NOPE_DIM:], ((0, 0), (0, KV_IN - KV_LORA), (0, 0), (0, 0)))
    wv = wv.reshape(nl, KV_IN, MLA_HEADS * V_DIM)
    wkv = jnp.concatenate([wk, wv], axis=2).astype(BF16)
    wuk_t = jnp.transpose(w_ukv[..., :NOPE_DIM], (0, 2, 3, 1)).astype(BF16)
    wuv = jnp.transpose(w_ukv[..., NOPE_DIM:], (0, 2, 1, 3)).astype(BF16)
    return dict(w_ada=w_ada, w1=w1, w2=w2, uq=uq, wkv=wkv, wuk_t=wuk_t, wuv=wuv,
                w_a=w_a.astype(BF16), w_b=w_b.astype(BF16), w_o=w_o.astype(BF16),
                w_pq_t=jnp.swapaxes(w_pq, 1, 2).astype(BF16), k1=k1.astype(BF16), k2=k2.astype(BF16),
                u=u.astype(BF16), v_t=jnp.swapaxes(v, 1, 2).astype(BF16))


def kernel(x_prompt, x_sample, c_prompt, c_sample, cache_ckv, cache_krope, state_ret, w_ada, b_ada, g_mix, g_ffn, w_in, g_q, w_uq, g_kv, w_ukv, w_a, w_b, w_o, w_pq, peer_k1, peer_k2, peer_u, peer_v, g_final):
    bp, s_len, d = x_prompt.shape
    bs, t_s, _ = x_sample.shape
    depth = w_in.shape[0]
    past = cache_ckv.shape[2]
    n_p, n_s = bp * s_len, bs * t_s
    n = n_p + n_s
    assert t_s == GROUP and s_len % GROUP == 0

    x = jnp.concatenate([x_prompt.reshape(n_p, d), x_sample.reshape(n_s, d)], axis=0)
    c_all = jnp.concatenate([c_prompt, c_sample], axis=0)
    gidx = jnp.concatenate([jnp.repeat(jnp.arange(bp), s_len // GROUP), bp + jnp.arange(bs)])
    pos = jnp.concatenate([jnp.tile(jnp.arange(s_len), bp), jnp.tile(past + jnp.arange(t_s), bs)])
    cos64, sa64, sb64, cos128, sin128 = _rope_tables(pos)
    ret_c = _ret_consts()
    w = _prep_weights(w_ada, w_in, w_uq, w_ukv, w_a, w_b, w_o, w_pq, peer_k1, peer_k2, peer_u, peer_v)
    zero_state = jnp.zeros((1, bp, RET_HEADS, RET_QK_DIM, RET_V_DIM), F32)
    silu_c = jax.nn.silu(c_all)
    gspec = lambda tm, tn: pl.BlockSpec((tm // GROUP, 1, tn), lambda i, j: (i, 0, j))
    tile = lambda tm, tn, off=0: pl.BlockSpec((tm, tn), lambda i, j: (i, j + off))
    rope_spec = lambda tm: pl.BlockSpec((tm, LANES), lambda i, j: (i, 0))

    outs = {k: [] for k in ("ckv_p", "kr_p", "st_p", "ckv_s", "kr_s", "st_s")}
    for l in range(depth):
        ada = matmul(silu_c, w["w_ada"], F32, wl=l, name="ada") + b_ada[l][None, :]
        cond = ada.reshape(bp + bs, 6, d)[gidx]
        cg = [cond[:, k, :][:, None, :] for k in range(6)]

        h = modnorm(x, g_mix[l], cg[1], cg[0])
        cqn, ckvn, krr, kvin = inproj1(h, w["w1"], l, g_q[l], g_kv[l], cos64, sa64, sb64)
        big = matmul(h, w["w2"], BF16, tm=1024, tn=2048, wl=l, name="inproj2")
        tmq = 1024
        q_cat = matmul(cqn, w["uq"], BF16, tm=tmq, tn=2048, epilogue=_ep_qrope, wl=l,
                       extras=(cos64, sa64, sb64), extra_specs=[rope_spec(tmq)] * 3, name="q_up")
        kv = matmul(kvin, w["wkv"], BF16, tm=1024, tn=2048, wl=l, m=n_p, name="kv_up")
        o_a = attn_prompt(q_cat, kv, bp, s_len)
        o_a = attn_sample(q_cat, cache_ckv, cache_krope, kvin, w["wuk_t"], w["wuv"], l, o_a, n_p, t_s)
        o_b, st_p = retention(big, cos128, sin128, zero_state, 0, ret_c, 0, bp, s_len, min(8, s_len // CHUNK), 4)
        o_b, st_s = retention(big, cos128, sin128, state_ret, l, ret_c, n_p, bs, t_s, 1, RET_HEADS, prev=o_b)
        mixed = proj_ab(o_a, o_b, w["w_a"], w["w_b"], l, big, 2 * RET_HEADS * RET_QK_DIM + 2 * RET_HEADS * RET_V_DIM)
        tm, tn = 1024, 1024
        x = matmul(mixed, w["w_o"], F32, tm=tm, tn=tn, epilogue=_ep_resid, wl=l,
                   extras=(x, cg[2]), extra_specs=[tile(tm, tn), gspec(tm, tn)], name="proj_o")

        outs["ckv_p"].append(ckvn[:n_p].reshape(bp, s_len, KV_LORA))
        outs["kr_p"].append(krr[:n_p, :ROPE_DIM].reshape(bp, s_len, ROPE_DIM))
        outs["st_p"].append(st_p)
        outs["ckv_s"].append(ckvn[n_p:].reshape(bs, t_s, KV_LORA))
        outs["kr_s"].append(krr[n_p:, :ROPE_DIM].reshape(bs, t_s, ROPE_DIM))
        outs["st_s"].append(st_s)

        h2t = modnorm(x, g_ffn[l], cg[4], cg[3], transposed=True)
        qt = matmul(w["w_pq_t"], h2t, F32, tm=2048, tn=1024, xl=l, name="peer_q")
        thr, e1, s2, e2 = peer_route(qt, w["k1"], w["k2"], l)
        x = peer_experts(h2t, w["u"], w["v_t"], l, thr, e1, s2, e2, x, cg[5])

    y_p = final_rmsnorm(x, g_final, 0, n_p)
    y_s = final_rmsnorm(x, g_final, n_p, n_s)
    st = lambda k: jnp.stack(outs[k], axis=0)
    return (y_p.reshape(bp, s_len, d), y_s.reshape(bs, t_s, d),
            st("ckv_p"), st("kr_p"), st("st_p"), st("ckv_s"), st("kr_s"), st("st_s"))
```

```python
import functools

import numpy as np
import jax
import jax.numpy as jnp
from jax import lax
from jax.experimental import pallas as pl
from jax.experimental.pallas import tpu as pltpu

F32 = jnp.float32
BF16 = jnp.bfloat16

D_MODEL = 2048
CHUNK = 64
EPS = 1e-6
ROPE_THETA = 10000.0
MLA_HEADS = 16
Q_LORA = 512
KV_LORA = 512
NOPE_DIM = 128
ROPE_DIM = 64
V_DIM = 128
MLA_SCALE = (NOPE_DIM + ROPE_DIM) ** -0.5
LOG2E = 1.4426950408889634
QK_PAD = 256
KV_IN = KV_LORA + 128
RET_HEADS = 8
RET_QK_DIM = 128
RET_V_DIM = 256
PEER_HEADS = 8
N_KEYS = 128
PEER_HALF = 128
PEER_TOPK = 16
GROUP = 64
LANES = 128
SUBLANES = 8

VMEM_LIMIT = 60 * 1024 * 1024


def _cparams(*sem):
    return pltpu.CompilerParams(dimension_semantics=sem, vmem_limit_bytes=VMEM_LIMIT)


def _modnorm_kernel(x_ref, g_ref, sc_ref, sh_ref, o_ref, *, transposed):
    x = x_ref[...]
    tr, d = x.shape
    y = x * lax.rsqrt(jnp.mean(x * x, axis=-1, keepdims=True) + EPS) * g_ref[...]
    y = (y.reshape(tr // GROUP, GROUP, d) * (1.0 + sc_ref[...]) + sh_ref[...]).reshape(tr, d)
    if transposed:
        y = jnp.transpose(y)
    o_ref[...] = y.astype(o_ref.dtype)


def modnorm(x, g, scale_g, shift_g, tr=512, transposed=False):
    n, d = x.shape
    tr = min(tr, n)
    gpt = tr // GROUP
    return pl.pallas_call(
        functools.partial(_modnorm_kernel, transposed=transposed),
        out_shape=jax.ShapeDtypeStruct((d, n) if transposed else (n, d), BF16),
        grid=(n // tr,),
        in_specs=[pl.BlockSpec((tr, d), lambda i: (i, 0)),
                  pl.BlockSpec((1, d), lambda i: (0, 0)),
                  pl.BlockSpec((gpt, 1, d), lambda i: (i, 0, 0)),
                  pl.BlockSpec((gpt, 1, d), lambda i: (i, 0, 0))],
        out_specs=(pl.BlockSpec((d, tr), lambda i: (0, i)) if transposed
                   else pl.BlockSpec((tr, d), lambda i: (i, 0))),
        compiler_params=_cparams("parallel"),
        name="modnorm",
    )(x, g.reshape(1, d), scale_g, shift_g)


def _rmsnorm_kernel(x_ref, g_ref, o_ref):
    x = x_ref[...]
    o_ref[...] = x * lax.rsqrt(jnp.mean(x * x, axis=-1, keepdims=True) + EPS) * g_ref[...]


def final_rmsnorm(x, g, row0, nrows, tr=512):
    d = x.shape[1]
    tr = min(tr, nrows)
    rb = row0 // tr
    return pl.pallas_call(
        _rmsnorm_kernel,
        out_shape=jax.ShapeDtypeStruct((nrows, d), F32),
        grid=(nrows // tr,),
        in_specs=[pl.BlockSpec((tr, d), lambda i: (rb + i, 0)),
                  pl.BlockSpec((1, d), lambda i: (0, 0))],
        out_specs=pl.BlockSpec((tr, d), lambda i: (i, 0)),
        compiler_params=_cparams("parallel"),
        name="final_rmsnorm",
    )(x, g.reshape(1, d))


def _mm_kernel(x_ref, w_ref, *rest, epilogue):
    extras, o_ref = rest[:-1], rest[-1]
    acc = jnp.dot(x_ref[...].astype(BF16), w_ref[...].astype(BF16), preferred_element_type=F32)
    if epilogue is not None:
        acc = epilogue(acc, *[e[...] for e in extras])
    o_ref[...] = acc.astype(o_ref.dtype)


def matmul(x, w, out_dtype, tm=1024, tn=512, epilogue=None, extras=(), extra_specs=(), name="matmul",
           xl=None, wl=None, m=None):
    k = x.shape[-1]
    m = x.shape[-2] if m is None else m
    n = w.shape[-1]
    tm, tn = min(tm, m), min(tn, n)
    assert m % tm == 0 and n % tn == 0, (m, tm, n, tn)
    xspec = (pl.BlockSpec((tm, k), lambda i, j: (i, 0)) if xl is None
             else pl.BlockSpec((None, tm, k), lambda i, j: (xl, i, 0)))
    wspec = (pl.BlockSpec((k, tn), lambda i, j: (0, j)) if wl is None
             else pl.BlockSpec((None, k, tn), lambda i, j: (wl, 0, j)))
    return pl.pallas_call(
        functools.partial(_mm_kernel, epilogue=epilogue),
        out_shape=jax.ShapeDtypeStruct((m, n), out_dtype),
        grid=(m // tm, n // tn),
        in_specs=[xspec, wspec] + list(extra_specs),
        out_specs=pl.BlockSpec((tm, tn), lambda i, j: (i, j)),
        compiler_params=_cparams("parallel", "arbitrary"),
        name=name,
    )(x, w, *extras)


def _rope_half(x, cos, sa, sb):
    return x * cos + pltpu.roll(x, 96, 1) * sa + pltpu.roll(x, 32, 1) * sb


def _ep_qrope(acc, cos, sa, sb):
    parts = []
    for h in range(acc.shape[1] // QK_PAD):
        parts.append(acc[:, h * QK_PAD:h * QK_PAD + NOPE_DIM])
        parts.append(_rope_half(acc[:, h * QK_PAD + NOPE_DIM:(h + 1) * QK_PAD], cos, sa, sb))
    return jnp.concatenate(parts, axis=1)


def _proj_ab_kernel(oa_ref, ob_ref, wa_ref, wb_ref, ga_ref, gb_ref, o_ref):
    ya = jnp.dot(oa_ref[...], wa_ref[...], preferred_element_type=F32)
    yb = jnp.dot(ob_ref[...], wb_ref[...], preferred_element_type=F32)
    o_ref[...] = (jax.nn.sigmoid(ga_ref[...].astype(F32)) * ya
                  + jax.nn.sigmoid(gb_ref[...].astype(F32)) * yb).astype(o_ref.dtype)


def proj_ab(o_a, o_b, w_a, w_b, l, big, gate_col0, tm=1024, tn=1024):
    m, k = o_a.shape
    n = w_a.shape[-1]
    tm = min(tm, m)
    goff = gate_col0 // tn
    xspec = pl.BlockSpec((tm, k), lambda i, j: (i, 0))
    wspec = pl.BlockSpec((None, k, tn), lambda i, j: (l, 0, j))
    return pl.pallas_call(
        _proj_ab_kernel,
        out_shape=jax.ShapeDtypeStruct((m, n), BF16),
        grid=(m // tm, n // tn),
        in_specs=[xspec, xspec, wspec, wspec,
                  pl.BlockSpec((tm, tn), lambda i, j: (i, goff + j)),
                  pl.BlockSpec((tm, tn), lambda i, j: (i, goff + n // tn + j))],
        out_specs=pl.BlockSpec((tm, tn), lambda i, j: (i, j)),
        compiler_params=_cparams("parallel", "arbitrary"),
        name="proj_ab",
    )(o_a, o_b, w_a, w_b, big, big)


def _ep_resid(acc, x, gate):
    tm, tn = acc.shape
    y = acc.reshape(tm // GROUP, GROUP, tn) * gate
    return x + y.reshape(tm, tn)


def _inproj1_kernel(h_ref, w_ref, gq_ref, gkv_ref, cos_ref, sa_ref, sb_ref,
                    cq_ref, ckv_ref, kr_ref, kvin_ref):
    acc = jnp.dot(h_ref[...], w_ref[...], preferred_element_type=F32)
    cq = acc[:, :Q_LORA]
    ckv = acc[:, Q_LORA:Q_LORA + KV_LORA]
    kr = acc[:, Q_LORA + KV_LORA:]
    cqn = cq * lax.rsqrt(jnp.mean(cq * cq, axis=-1, keepdims=True) + EPS) * gq_ref[...]
    ckvn = ckv * lax.rsqrt(jnp.mean(ckv * ckv, axis=-1, keepdims=True) + EPS) * gkv_ref[...]
    krr = _rope_half(kr, cos_ref[...], sa_ref[...], sb_ref[...])
    cq_ref[...] = cqn.astype(BF16)
    ckv_ref[...] = ckvn
    kr_ref[...] = krr
    kvin_ref[:, :KV_LORA] = ckvn.astype(BF16)
    kvin_ref[:, KV_LORA:] = krr.astype(BF16)


def inproj1(h, w1, l, g_q, g_kv, cos, sa, sb, tm=512):
    n, k = h.shape
    tm = min(tm, n)
    w1n = w1.shape[-1]
    row = lambda i: (i, 0)
    fix = lambda i: (0, 0)
    return pl.pallas_call(
        _inproj1_kernel,
        out_shape=(jax.ShapeDtypeStruct((n, Q_LORA), BF16),
                   jax.ShapeDtypeStruct((n, KV_LORA), F32),
                   jax.ShapeDtypeStruct((n, LANES), F32),
                   jax.ShapeDtypeStruct((n, KV_IN), BF16)),
        grid=(n // tm,),
        in_specs=[pl.BlockSpec((tm, k), row), pl.BlockSpec((None, k, w1n), lambda i: (l, 0, 0)),
                  pl.BlockSpec((1, Q_LORA), fix), pl.BlockSpec((1, KV_LORA), fix),
                  pl.BlockSpec((tm, LANES), row), pl.BlockSpec((tm, LANES), row), pl.BlockSpec((tm, LANES), row)],
        out_specs=(pl.BlockSpec((tm, Q_LORA), row), pl.BlockSpec((tm, KV_LORA), row),
                   pl.BlockSpec((tm, LANES), row), pl.BlockSpec((tm, KV_IN), row)),
        compiler_params=_cparams("parallel"),
        name="inproj1",
    )(h, w1, g_q.reshape(1, -1), g_kv.reshape(1, -1), cos, sa, sb)


def _attn_prompt_kernel(q_ref, k_ref, v_ref, o_ref, *, tq, tk, nh):
    qi = pl.program_id(2)
    nt = (((1,), (1,)), ((), ()))
    qs = [q_ref[:, h * QK_PAD:(h + 1) * QK_PAD] for h in range(nh)]

    def scores(start, size, h):
        k = k_ref[pl.ds(start, size), h * QK_PAD:(h + 1) * QK_PAD]
        v = v_ref[pl.ds(start, size), h * V_DIM:(h + 1) * V_DIM]
        return lax.dot_general(qs[h], k, nt, preferred_element_type=F32), v

    def update(carry, s, v):
        m, l, acc = carry
        m_new = jnp.maximum(m, jnp.max(s, axis=-1, keepdims=True))
        alpha = jnp.exp2(m - m_new)
        p = jnp.exp2(s - m_new)
        l = alpha * l + jnp.sum(p, axis=-1, keepdims=True)
        acc = alpha * acc + jnp.dot(p.astype(BF16), v, preferred_element_type=F32)
        return m_new, l, acc

    def big_body(j, carry):
        start = pl.multiple_of(j * tk, tk)
        return tuple(update(carry[h], *scores(start, tk, h)) for h in range(nh))

    q0 = qi * tq
    nbig = q0 // tk

    def small_body(i, carry):
        start = pl.multiple_of(nbig * tk + i * tq, tq)
        return tuple(update(carry[h], *scores(start, tq, h)) for h in range(nh))

    init = tuple((jnp.full((tq, 1), -1e30, F32), jnp.zeros((tq, 1), F32), jnp.zeros((tq, V_DIM), F32))
                 for _ in range(nh))
    carry = lax.fori_loop(0, nbig, big_body, init)
    carry = lax.fori_loop(0, (q0 - nbig * tk) // tq, small_body, carry)
    rows = lax.broadcasted_iota(jnp.int32, (tq, tq), 0) // CHUNK
    cols = lax.broadcasted_iota(jnp.int32, (tq, tq), 1) // CHUNK
    for h in range(nh):
        s, v = scores(pl.multiple_of(q0, tq), tq, h)
        s = jnp.where(cols <= rows, s, -1e30)
        m, l, acc = update(carry[h], s, v)
        o_ref[:, h * V_DIM:(h + 1) * V_DIM] = (acc / l).astype(o_ref.dtype)


def attn_prompt(q_cat, kv, bp, s_len, tq=512, tk=1024, nh=4):
    tq = min(tq, s_len)
    tk = min(tk, s_len)
    nq = s_len // tq
    vcol0 = MLA_HEADS * QK_PAD // (V_DIM * nh)
    return pl.pallas_call(
        functools.partial(_attn_prompt_kernel, tq=tq, tk=tk, nh=nh),
        out_shape=jax.ShapeDtypeStruct((q_cat.shape[0], MLA_HEADS * V_DIM), BF16),
        grid=(bp, MLA_HEADS // nh, nq),
        in_specs=[pl.BlockSpec((tq, QK_PAD * nh), lambda b, h, i: (b * nq + i, h)),
                  pl.BlockSpec((s_len, QK_PAD * nh), lambda b, h, i: (b, h)),
                  pl.BlockSpec((s_len, V_DIM * nh), lambda b, h, i: (b, vcol0 + h))],
        out_specs=pl.BlockSpec((tq, V_DIM * nh), lambda b, h, i: (b * nq + i, h)),
        compiler_params=_cparams("parallel", "parallel", "arbitrary"),
        name="attn_prompt",
    )(q_cat, kv, kv)


def _attn_sample_kernel(q_ref, ckv_ref, kr_ref, new_ref, wuk_ref, wuv_ref, prev_ref, o_ref,
                        qall_ref, kall_ref, olat_ref, *, t):
    del prev_ref
    nt = (((1,), (1,)), ((), ()))
    past = ckv_ref.shape[0]
    for h in range(MLA_HEADS):
        qn = q_ref[:, h * QK_PAD:h * QK_PAD + NOPE_DIM]
        qlat = jnp.dot(qn, wuk_ref[h], preferred_element_type=F32)
        qall_ref[h * t:(h + 1) * t, :KV_LORA] = qlat.astype(BF16)
        qall_ref[h * t:(h + 1) * t, KV_LORA:] = q_ref[:, h * QK_PAD + NOPE_DIM:(h + 1) * QK_PAD]
    kall_ref[:, :KV_LORA] = ckv_ref[...].astype(BF16)
    kall_ref[:, KV_LORA:KV_LORA + ROPE_DIM] = kr_ref[...].astype(BF16)
    kall_ref[:, KV_LORA + ROPE_DIM:] = jnp.zeros((past, KV_IN - KV_LORA - ROPE_DIM), BF16)
    knew = new_ref[...]
    half = MLA_HEADS * t // 2
    for r0 in (0, half):
        qall = qall_ref[r0:r0 + half, :]
        s_past = lax.dot_general(qall, kall_ref[...], nt, preferred_element_type=F32)
        s_new = lax.dot_general(qall, knew, nt, preferred_element_type=F32)
        m = jnp.maximum(jnp.max(s_past, axis=-1, keepdims=True), jnp.max(s_new, axis=-1, keepdims=True))
        p_past = jnp.exp2(s_past - m)
        p_new = jnp.exp2(s_new - m)
        l = jnp.sum(p_past, axis=-1, keepdims=True) + jnp.sum(p_new, axis=-1, keepdims=True)
        olat = (jnp.dot(p_past.astype(BF16), kall_ref[:, :KV_LORA], preferred_element_type=F32)
                + jnp.dot(p_new.astype(BF16), knew[:, :KV_LORA], preferred_element_type=F32))
        olat_ref[r0:r0 + half, :] = (olat / l).astype(BF16)
    for h in range(MLA_HEADS):
        o = jnp.dot(olat_ref[h * t:(h + 1) * t, :], wuv_ref[h], preferred_element_type=F32)
        o_ref[:, h * V_DIM:(h + 1) * V_DIM] = o.astype(o_ref.dtype)


def attn_sample(q_cat, ckv_past, kr_past, kvin, wuk_t, wuv, l, prev, row0, t):
    _, bs, past, _ = ckv_past.shape
    rb = row0 // t
    return pl.pallas_call(
        functools.partial(_attn_sample_kernel, t=t),
        out_shape=jax.ShapeDtypeStruct(prev.shape, prev.dtype),
        grid=(bs,),
        in_specs=[pl.BlockSpec((t, MLA_HEADS * QK_PAD), lambda b: (rb + b, 0)),
                  pl.BlockSpec((None, None, past, KV_LORA), lambda b: (l, b, 0, 0)),
                  pl.BlockSpec((None, None, past, ROPE_DIM), lambda b: (l, b, 0, 0)),
                  pl.BlockSpec((t, KV_IN), lambda b: (rb + b, 0)),
                  pl.BlockSpec((None, MLA_HEADS, NOPE_DIM, KV_LORA), lambda b: (l, 0, 0, 0)),
                  pl.BlockSpec((None, MLA_HEADS, KV_LORA, V_DIM), lambda b: (l, 0, 0, 0)),
                  pl.BlockSpec(memory_space=pl.ANY)],
        out_specs=pl.BlockSpec((t, MLA_HEADS * V_DIM), lambda b: (rb + b, 0)),
        scratch_shapes=[pltpu.VMEM((MLA_HEADS * t, KV_IN), BF16),
                        pltpu.VMEM((past, KV_IN), BF16),
                        pltpu.VMEM((MLA_HEADS * t, KV_LORA), BF16)],
        input_output_aliases={6: 0},
        compiler_params=_cparams("parallel"),
        name="attn_sample",
    )(q_cat, ckv_past, kr_past, kvin, wuk_t, wuv, prev)


def _ret_consts():
    lg = np.log1p(-(2.0 ** (-5.0 - np.arange(RET_HEADS, dtype=np.float64))))
    idx = np.arange(CHUNK, dtype=np.float64)
    diff = idx[:, None] - idx[None, :]
    dmask = np.where(diff[None] >= 0, np.exp(np.maximum(diff, 0.0)[None] * lg[:, None, None]), 0.0)
    qd = np.exp((idx + 1.0)[None, :] * lg[:, None])
    kd = np.exp((CHUNK - 1.0 - idx)[None, :] * lg[:, None])
    g = np.exp(CHUNK * lg)
    qd = np.broadcast_to(qd[:, :, None], (RET_HEADS, CHUNK, RET_QK_DIM))
    kd = np.broadcast_to(kd[:, :, None], (RET_HEADS, CHUNK, RET_QK_DIM))
    g = np.broadcast_to(g[:, None, None], (RET_HEADS, 1, RET_V_DIM))
    return (jnp.asarray(dmask, F32), jnp.asarray(qd, F32), jnp.asarray(kd, F32), jnp.asarray(g, F32))


def _retention_kernel(q_ref, k_ref, v_ref, g_ref, cos_ref, sin_ref, st0_ref, dm_ref, qd_ref, kd_ref,
                      gam_ref, *rest, cps, hps):
    o_ref, st_ref, state = rest[-3:]
    step = pl.program_id(2)
    nt = (((1,), (1,)), ((), ()))

    @pl.when(step == 0)
    def _():
        state[...] = st0_ref[...]

    for c in range(cps):
        r = slice(c * CHUNK, (c + 1) * CHUNK)
        cos, sin = cos_ref[r, :], sin_ref[r, :]
        for hh in range(hps):
            qk = slice(hh * RET_QK_DIM, (hh + 1) * RET_QK_DIM)
            vv = slice(hh * RET_V_DIM, (hh + 1) * RET_V_DIM)
            qf = q_ref[r, qk].astype(F32)
            kf = k_ref[r, qk].astype(F32)
            q = qf * cos + pltpu.roll(qf, 64, 1) * sin
            k = (kf * cos + pltpu.roll(kf, 64, 1) * sin) * (RET_QK_DIM ** -0.5)
            v = v_ref[r, vv]
            st = state[hh]
            inner = lax.dot_general(q.astype(BF16), k.astype(BF16), nt, preferred_element_type=F32) * dm_ref[hh]
            o = (jnp.dot(inner.astype(BF16), v, preferred_element_type=F32)
                 + jnp.dot((q * qd_ref[hh]).astype(BF16), st.astype(BF16), preferred_element_type=F32))
            kt = jnp.transpose(k * kd_ref[hh]).astype(BF16)
            state[hh] = st * gam_ref[hh] + jnp.dot(kt, v, preferred_element_type=F32)
            mu = jnp.mean(o, axis=-1, keepdims=True)
            oc = o - mu
            var = jnp.mean(oc * oc, axis=-1, keepdims=True)
            gate = g_ref[r, vv].astype(F32)
            o_ref[r, vv] = (gate * jax.nn.sigmoid(gate) * (oc * lax.rsqrt(var + EPS))).astype(o_ref.dtype)

    @pl.when(step == pl.num_programs(2) - 1)
    def _():
        st_ref[0] = state[...]


def retention(big, cos, sin, state0, l, consts, row0, nseq, seq_len, cps, hps, prev=None):
    r = cps * CHUNK
    steps = seq_len // r
    rb0 = row0 // r
    ng = RET_HEADS // hps
    dmask, qd, kd, gam = consts
    rowblk = lambda s, h, t: rb0 + s * steps + t
    hconst = lambda s, h, t: (h, 0, 0)
    in_specs = [pl.BlockSpec((r, hps * RET_QK_DIM), lambda s, h, t: (rowblk(s, h, t), h)),
                pl.BlockSpec((r, hps * RET_QK_DIM), lambda s, h, t: (rowblk(s, h, t), ng + h)),
                pl.BlockSpec((r, hps * RET_V_DIM), lambda s, h, t: (rowblk(s, h, t), ng + h)),
                pl.BlockSpec((r, hps * RET_V_DIM), lambda s, h, t: (rowblk(s, h, t), 2 * ng + h)),
                pl.BlockSpec((r, RET_QK_DIM), lambda s, h, t: (rowblk(s, h, t), 0)),
                pl.BlockSpec((r, RET_QK_DIM), lambda s, h, t: (rowblk(s, h, t), 0)),
                pl.BlockSpec((None, None, hps, RET_QK_DIM, RET_V_DIM), lambda s, h, t: (l, s, h, 0, 0)),
                pl.BlockSpec((hps, CHUNK, CHUNK), hconst),
                pl.BlockSpec((hps, CHUNK, RET_QK_DIM), hconst),
                pl.BlockSpec((hps, CHUNK, RET_QK_DIM), hconst),
                pl.BlockSpec((hps, 1, RET_V_DIM), hconst)]
    args = [big, big, big, big, cos, sin, state0, dmask, qd, kd, gam]
    aliases = {}
    if prev is not None:
        in_specs.append(pl.BlockSpec(memory_space=pl.ANY))
        args.append(prev)
        aliases = {len(args) - 1: 0}
    return pl.pallas_call(
        functools.partial(_retention_kernel, cps=cps, hps=hps),
        out_shape=(jax.ShapeDtypeStruct((big.shape[0], RET_HEADS * RET_V_DIM), BF16),
                   jax.ShapeDtypeStruct((nseq, RET_HEADS, RET_QK_DIM, RET_V_DIM), F32)),
        grid=(nseq, ng, steps),
        in_specs=in_specs,
        out_specs=(pl.BlockSpec((r, hps * RET_V_DIM), lambda s, h, t: (rowblk(s, h, t), h)),
                   pl.BlockSpec((1, hps, RET_QK_DIM, RET_V_DIM), lambda s, h, t: (s, h, 0, 0))),
        scratch_shapes=[pltpu.VMEM((hps, RET_QK_DIM, RET_V_DIM), F32)],
        input_output_aliases=aliases,
        compiler_params=_cparams("parallel", "parallel", "arbitrary"),
        name="retention",
    )(*args)


_CAND = [(a, b) for a in range(PEER_TOPK) for b in range(PEER_TOPK) if (a + 1) * (b + 1) <= PEER_TOPK]


def _top_distinct(s):
    vals, cnts = [], []
    for _ in range(PEER_TOPK):
        m = jnp.max(s, axis=0, keepdims=True)
        hit = s == m
        vals.append(m)
        cnts.append(jnp.sum(jnp.where(hit, 1.0, 0.0), axis=0, keepdims=True))
        s = jnp.where(hit, -jnp.inf, s)
    return vals, cnts


def _route_kernel(qt_ref, k1_ref, k2_ref, thr_ref, e1_ref, s2_ref, e2_ref, cand_ref, mult_ref):
    tt = qt_ref.shape[1]
    pad = cand_ref.shape[0] - len(_CAND)
    cand_ref[len(_CAND):, :] = jnp.full((pad, LANES), -jnp.inf, F32)
    mult_ref[len(_CAND):, :] = jnp.zeros((pad, LANES), F32)
    for h, c in [(h, c) for h in range(PEER_HEADS) for c in range(tt // LANES)]:
        lanes = slice(c * LANES, (c + 1) * LANES)
        q1 = qt_ref[h * 2 * PEER_HALF:h * 2 * PEER_HALF + PEER_HALF, lanes].astype(BF16)
        q2 = qt_ref[h * 2 * PEER_HALF + PEER_HALF:(h + 1) * 2 * PEER_HALF, lanes].astype(BF16)
        s1 = jnp.dot(k1_ref[...], q1, preferred_element_type=F32)
        s2 = jnp.dot(k2_ref[...], q2, preferred_element_type=F32)
        v1, c1 = _top_distinct(s1)
        v2, c2 = _top_distinct(s2)
        for r, (a, b) in enumerate(_CAND):
            cand_ref[r:r + 1, :] = v1[a] + v2[b]
            mult_ref[r:r + 1, :] = c1[a] * c2[b]
        cand = cand_ref[...]
        mult = mult_ref[...]
        rest = cand
        cum = jnp.zeros_like(v1[0])
        tau = v1[0] + v2[0]
        for _ in range(PEER_TOPK):
            m = jnp.max(rest, axis=0, keepdims=True)
            hit = rest == m
            tau = jnp.where(cum < PEER_TOPK, m, tau)
            cum = cum + jnp.sum(jnp.where(hit, mult, 0.0), axis=0, keepdims=True)
            rest = jnp.where(hit, -jnp.inf, rest)
        top = v1[0] + v2[0]
        z = jnp.sum(jnp.where(cand >= tau, mult * jnp.exp(cand - top), 0.0), axis=0, keepdims=True)
        thr = jnp.full_like(s1, jnp.inf)
        for a in range(PEER_TOPK):
            th_a = jnp.full_like(tau, jnp.inf)
            for b in range(PEER_TOPK // (a + 1)):
                th_a = jnp.where(v1[a] + v2[b] >= tau, v2[b], th_a)
            thr = jnp.where(s1 == v1[a], th_a, thr)
        thr_ref[h, :, lanes] = thr
        s2_ref[h, :, lanes] = s2
        e1_ref[h, :, lanes] = 0.5 * jnp.exp(s1 - v1[0])
        e2_ref[h, :, lanes] = jnp.exp(s2 - v2[0]) / z


def peer_route(qt, k1, k2, l, tt=256):
    n = qt.shape[1]
    tt = min(tt, n)
    big = jax.ShapeDtypeStruct((PEER_HEADS, N_KEYS, n), F32)
    bspec = pl.BlockSpec((PEER_HEADS, N_KEYS, tt), lambda i: (0, 0, i))
    return pl.pallas_call(
        _route_kernel,
        out_shape=(big, big, big, big),
        grid=(n // tt,),
        in_specs=[pl.BlockSpec((PEER_HEADS * 2 * PEER_HALF, tt), lambda i: (0, i)),
                  pl.BlockSpec((None, N_KEYS, PEER_HALF), lambda i: (l, 0, 0)),
                  pl.BlockSpec((None, N_KEYS, PEER_HALF), lambda i: (l, 0, 0))],
        out_specs=(bspec, bspec, bspec, bspec),
        scratch_shapes=[pltpu.VMEM((-(-len(_CAND) // SUBLANES) * SUBLANES, LANES), F32)] * 2,
        compiler_params=_cparams("parallel"),
        name="peer_route",
    )(qt, k1, k2)


def _gate_rows(ii, thr_ref, e1_ref, s2_ref, e2_ref):
    w = None
    for h in range(PEER_HEADS):
        c = jnp.where(s2_ref[h] >= thr_ref[h, ii:ii + 1, :], e2_ref[h], 0.0) * e1_ref[h, ii:ii + 1, :]
        w = c if w is None else w + c
    return w


def _peer_expert_kernel(ht_ref, u_ref, vt_ref, thc_ref, e1c_ref, thn_ref, e1n_ref, s2_ref, e2_ref,
                        x_ref, gate_ref, o_ref, acc_ref, wg_ref, w_ref, *, ipt):
    e = pl.program_id(1)
    last = pl.num_programs(1) - 1

    def build(thr_ref, e1_ref, slot):
        for ii in range(ipt):
            w_ref[slot, ii * N_KEYS:(ii + 1) * N_KEYS, :] = _gate_rows(ii, thr_ref, e1_ref, s2_ref, e2_ref)

    @pl.when(e == 0)
    def _():
        acc_ref[...] = jnp.zeros_like(acc_ref)
        build(thc_ref, e1c_ref, 0)

    build(thn_ref, e1n_ref, (e + 1) % 2)
    act = jnp.dot(u_ref[...], ht_ref[...], preferred_element_type=F32)
    gelu2 = act * (1.0 + lax.erf(act * (2.0 ** -0.5)))
    wg_ref[...] = (w_ref[e % 2] * gelu2).astype(BF16)
    acc_ref[...] += jnp.dot(vt_ref[...], wg_ref[...], preferred_element_type=F32)

    @pl.when(e == last)
    def _():
        tt, d = o_ref.shape
        y = jnp.transpose(acc_ref[...]).reshape(tt // GROUP, GROUP, d) * gate_ref[...]
        o_ref[...] = x_ref[...] + y.reshape(tt, d)


def peer_experts(ht, u, vt, l, thr, e1, s2, e2, x, gate_g, tt=512, ipt=8):
    d, n = ht.shape
    ne = u.shape[1]
    tt = min(tt, n)
    te = ipt * N_KEYS
    nlast = ne // te - 1
    full = pl.BlockSpec((PEER_HEADS, N_KEYS, tt), lambda t, e: (0, 0, t))
    cur = pl.BlockSpec((PEER_HEADS, ipt, tt), lambda t, e: (0, e, t))
    nxt = pl.BlockSpec((PEER_HEADS, ipt, tt), lambda t, e: (0, jnp.minimum(e + 1, nlast), t))
    return pl.pallas_call(
        functools.partial(_peer_expert_kernel, ipt=ipt),
        out_shape=jax.ShapeDtypeStruct((n, d), F32),
        grid=(n // tt, ne // te),
        in_specs=[pl.BlockSpec((d, tt), lambda t, e: (0, t)),
                  pl.BlockSpec((None, te, d), lambda t, e: (l, e, 0)),
                  pl.BlockSpec((None, d, te), lambda t, e: (l, 0, e)),
                  cur, cur, nxt, nxt, full, full,
                  pl.BlockSpec((tt, d), lambda t, e: (t, 0)),
                  pl.BlockSpec((tt // GROUP, 1, d), lambda t, e: (t, 0, 0))],
        out_specs=pl.BlockSpec((tt, d), lambda t, e: (t, 0)),
        scratch_shapes=[pltpu.VMEM((d, tt), F32), pltpu.VMEM((te, tt), BF16), pltpu.VMEM((2, te, tt), F32)],
        compiler_params=_cparams("parallel", "arbitrary"),
        name="peer_experts",
    )(ht, u, vt, thr, e1, thr, e1, s2, e2, x, gate_g)


def _rope_tables(pos):
    pos = pos.astype(F32)[:, None]
    half = ROPE_DIM // 2
    inv = ROPE_THETA ** (-jnp.arange(half, dtype=F32) * 2.0 / ROPE_DIM)
    ang = pos * inv[None, :]
    c, s = jnp.cos(ang), jnp.sin(ang)
    z32 = jnp.zeros_like(c)
    z64 = jnp.concatenate([z32, z32], axis=1)
    cos64 = jnp.concatenate([c, c, z64], axis=1)
    sa64 = jnp.concatenate([-s, z32, z64], axis=1)
    sb64 = jnp.concatenate([z32, s, z64], axis=1)
    half = RET_QK_DIM // 2
    inv = ROPE_THETA ** (-jnp.arange(half, dtype=F32) * 2.0 / RET_QK_DIM)
    ang = pos * inv[None, :]
    c, s = jnp.cos(ang), jnp.sin(ang)
    return cos64, sa64, sb64, jnp.concatenate([c, c], axis=1), jnp.concatenate([-s, s], axis=1)


def _prep_weights(w_ada, w_in, w_uq, w_ukv, w_a, w_b, w_o, w_pq, k1, k2, u, v):
    nl = w_in.shape[0]
    o1 = Q_LORA + KV_LORA + ROPE_DIM
    w1 = jnp.pad(w_in[:, :, :o1], ((0, 0), (0, 0), (0, LANES - ROPE_DIM))).astype(BF16)
    w2 = w_in[:, :, o1:].astype(BF16)
    uq = w_uq.reshape(nl, Q_LORA, MLA_HEADS, NOPE_DIM + ROPE_DIM) * (MLA_SCALE * LOG2E)
    uq = jnp.pad(uq, ((0, 0), (0, 0), (0, 0), (0, QK_PAD - NOPE_DIM - ROPE_DIM)))
    uq = uq.reshape(nl, Q_LORA, MLA_HEADS * QK_PAD).astype(BF16)
    wk = jnp.pad(w_ukv[..., :NOPE_DIM], ((0, 0), (0, KV_IN - KV_LORA), (0, 0), (0, QK_PAD - NOPE_DIM)))
    eye = jnp.zeros((KV_IN, QK_PAD), F32).at[KV_LORA + jnp.arange(ROPE_DIM), NOPE_DIM + jnp.arange(ROPE_DIM)].set(1.0)
    wk = (wk + eye[None, :, None, :]).reshape(nl, KV_IN, MLA_HEADS * QK_PAD)
    wv = jnp.pad(w_ukv[..., NOPE_DIM:], ((0, 0), (0, KV_IN - KV_LORA), (0, 0), (0, 0)))
    wv = wv.reshape(nl, KV_IN, MLA_HEADS * V_DIM)
    wkv = jnp.concatenate([wk, wv], axis=2).astype(BF16)
    wuk_t = jnp.transpose(w_ukv[..., :NOPE_DIM], (0, 2, 3, 1)).astype(BF16)
    wuv = jnp.transpose(w_ukv[..., NOPE_DIM:], (0, 2, 1, 3)).astype(BF16)
    return dict(w_ada=w_ada, w1=w1, w2=w2, uq=uq, wkv=wkv, wuk_t=wuk_t, wuv=wuv,
                w_a=w_a.astype(BF16), w_b=w_b.astype(BF16), w_o=w_o.astype(BF16),
                w_pq_t=jnp.swapaxes(w_pq, 1, 2).astype(BF16), k1=k1.astype(BF16), k2=k2.astype(BF16),
                u=u.astype(BF16), v_t=jnp.swapaxes(v, 1, 2).astype(BF16))


def kernel(x_prompt, x_sample, c_prompt, c_sample, cache_ckv, cache_krope, state_ret, w_ada, b_ada, g_mix, g_ffn, w_in, g_q, w_uq, g_kv, w_ukv, w_a, w_b, w_o, w_pq, peer_k1, peer_k2, peer_u, peer_v, g_final):
    bp, s_len, d = x_prompt.shape
    bs, t_s, _ = x_sample.shape
    depth = w_in.shape[0]
    past = cache_ckv.shape[2]
    n_p, n_s = bp * s_len, bs * t_s
    n = n_p + n_s
    assert t_s == GROUP and s_len % GROUP == 0

    x = jnp.concatenate([x_prompt.reshape(n_p, d), x_sample.reshape(n_s, d)], axis=0)
    c_all = jnp.concatenate([c_prompt, c_sample], axis=0)
    gidx = jnp.concatenate([jnp.repeat(jnp.arange(bp), s_len // GROUP), bp + jnp.arange(bs)])
    pos = jnp.concatenate([jnp.tile(jnp.arange(s_len), bp), jnp.tile(past + jnp.arange(t_s), bs)])
    cos64, sa64, sb64, cos128, sin128 = _rope_tables(pos)
    ret_c = _ret_consts()
    w = _prep_weights(w_ada, w_in, w_uq, w_ukv, w_a, w_b, w_o, w_pq, peer_k1, peer_k2, peer_u, peer_v)
    zero_state = jnp.zeros((1, bp, RET_HEADS, RET_QK_DIM, RET_V_DIM), F32)
    silu_c = jax.nn.silu(c_all)
    gspec = lambda tm, tn: pl.BlockSpec((tm // GROUP, 1, tn), lambda i, j: (i, 0, j))
    tile = lambda tm, tn, off=0: pl.BlockSpec((tm, tn), lambda i, j: (i, j + off))
    rope_spec = lambda tm: pl.BlockSpec((tm, LANES), lambda i, j: (i, 0))

    outs = {k: [] for k in ("ckv_p", "kr_p", "st_p", "ckv_s", "kr_s", "st_s")}
    for l in range(depth):
        ada = matmul(silu_c, w["w_ada"], F32, wl=l, name="ada") + b_ada[l][None, :]
        cond = ada.reshape(bp + bs, 6, d)[gidx]
        cg = [cond[:, k, :][:, None, :] for k in range(6)]

        h = modnorm(x, g_mix[l], cg[1], cg[0])
        cqn, ckvn, krr, kvin = inproj1(h, w["w1"], l, g_q[l], g_kv[l], cos64, sa64, sb64)
        big = matmul(h, w["w2"], BF16, tm=1024, tn=2048, wl=l, name="inproj2")
        tmq = 1024
        q_cat = matmul(cqn, w["uq"], BF16, tm=tmq, tn=2048, epilogue=_ep_qrope, wl=l,
                       extras=(cos64, sa64, sb64), extra_specs=[rope_spec(tmq)] * 3, name="q_up")
        kv = matmul(kvin, w["wkv"], BF16, tm=1024, tn=2048, wl=l, m=n_p, name="kv_up")
        o_a = attn_prompt(q_cat, kv, bp, s_len)
        o_a = attn_sample(q_cat, cache_ckv, cache_krope, kvin, w["wuk_t"], w["wuv"], l, o_a, n_p, t_s)
        o_b, st_p = retention(big, cos128, sin128, zero_state, 0, ret_c, 0, bp, s_len, min(8, s_len // CHUNK), 4)
        o_b, st_s = retention(big, cos128, sin128, state_ret, l, ret_c, n_p, bs, t_s, 1, RET_HEADS, prev=o_b)
        mixed = proj_ab(o_a, o_b, w["w_a"], w["w_b"], l, big, 2 * RET_HEADS * RET_QK_DIM + 2 * RET_HEADS * RET_V_DIM)
        tm, tn = 1024, 1024
        x = matmul(mixed, w["w_o"], F32, tm=tm, tn=tn, epilogue=_ep_resid, wl=l,
                   extras=(x, cg[2]), extra_specs=[tile(tm, tn), gspec(tm, tn)], name="proj_o")

        outs["ckv_p"].append(ckvn[:n_p].reshape(bp, s_len, KV_LORA))
        outs["kr_p"].append(krr[:n_p, :ROPE_DIM].reshape(bp, s_len, ROPE_DIM))
        outs["st_p"].append(st_p)
        outs["ckv_s"].append(ckvn[n_p:].reshape(bs, t_s, KV_LORA))
        outs["kr_s"].append(krr[n_p:, :ROPE_DIM].reshape(bs, t_s, ROPE_DIM))
        outs["st_s"].append(st_s)

        h2t = modnorm(x, g_ffn[l], cg[4], cg[3], transposed=True)
        qt = matmul(w["w_pq_t"], h2t, F32, tm=2048, tn=1024, xl=l, name="peer_q")
        thr, e1, s2, e2 = peer_route(qt, w["k1"], w["k2"], l)
        x = peer_experts(h2t, w["u"], w["v_t"], l, thr, e1, s2, e2, x, cg[5])

    y_p = final_rmsnorm(x, g_final, 0, n_p)
    y_s = final_rmsnorm(x, g_final, n_p, n_s)
    st = lambda k: jnp.stack(outs[k], axis=0)
    return (y_p.reshape(bp, s_len, d), y_s.reshape(bs, t_s, d),
            st("ckv_p"), st("kr_p"), st("st_p"), st("ckv_s"), st("kr_s"), st("st_s"))
```

```python
import functools

import numpy as np
import jax
import jax.numpy as jnp
from jax import lax
from jax.experimental import pallas as pl
from jax.experimental.pallas import tpu as pltpu

F32 = jnp.float32
BF16 = jnp.bfloat16

D_MODEL = 2048
CHUNK = 64
EPS = 1e-6
ROPE_THETA = 10000.0
MLA_HEADS = 16
Q_LORA = 512
KV_LORA = 512
NOPE_DIM = 128
ROPE_DIM = 64
V_DIM = 128
MLA_SCALE = (NOPE_DIM + ROPE_DIM) ** -0.5
LOG2E = 1.4426950408889634
QK_PAD = 256
KV_IN = KV_LORA + 128
RET_HEADS = 8
RET_QK_DIM = 128
RET_V_DIM = 256
PEER_HEADS = 8
N_KEYS = 128
PEER_HALF = 128
PEER_TOPK = 16
GROUP = 64
LANES = 128
SUBLANES = 8

VMEM_LIMIT = 60 * 1024 * 1024


def _cparams(*sem):
    return pltpu.CompilerParams(dimension_semantics=sem, vmem_limit_bytes=VMEM_LIMIT)


def _modnorm_kernel(x_ref, g_ref, sc_ref, sh_ref, o_ref, *, transposed):
    x = x_ref[...]
    tr, d = x.shape
    y = x * lax.rsqrt(jnp.mean(x * x, axis=-1, keepdims=True) + EPS) * g_ref[...]
    y = (y.reshape(tr // GROUP, GROUP, d) * (1.0 + sc_ref[...]) + sh_ref[...]).reshape(tr, d)
    if transposed:
        y = jnp.transpose(y)
    o_ref[...] = y.astype(o_ref.dtype)


def modnorm(x, g, scale_g, shift_g, tr=512, transposed=False):
    n, d = x.shape
    tr = min(tr, n)
    gpt = tr // GROUP
    return pl.pallas_call(
        functools.partial(_modnorm_kernel, transposed=transposed),
        out_shape=jax.ShapeDtypeStruct((d, n) if transposed else (n, d), BF16),
        grid=(n // tr,),
        in_specs=[pl.BlockSpec((tr, d), lambda i: (i, 0)),
                  pl.BlockSpec((1, d), lambda i: (0, 0)),
                  pl.BlockSpec((gpt, 1, d), lambda i: (i, 0, 0)),
                  pl.BlockSpec((gpt, 1, d), lambda i: (i, 0, 0))],
        out_specs=(pl.BlockSpec((d, tr), lambda i: (0, i)) if transposed
                   else pl.BlockSpec((tr, d), lambda i: (i, 0))),
        compiler_params=_cparams("parallel"),
        name="modnorm",
    )(x, g.reshape(1, d), scale_g, shift_g)


def _rmsnorm_kernel(x_ref, g_ref, o_ref):
    x = x_ref[...]
    o_ref[...] = x * lax.rsqrt(jnp.mean(x * x, axis=-1, keepdims=True) + EPS) * g_ref[...]


def final_rmsnorm(x, g, row0, nrows, tr=512):
    d = x.shape[1]
    tr = min(tr, nrows)
    rb = row0 // tr
    return pl.pallas_call(
        _rmsnorm_kernel,
        out_shape=jax.ShapeDtypeStruct((nrows, d), F32),
        grid=(nrows // tr,),
        in_specs=[pl.BlockSpec((tr, d), lambda i: (rb + i, 0)),
                  pl.BlockSpec((1, d), lambda i: (0, 0))],
        out_specs=pl.BlockSpec((tr, d), lambda i: (i, 0)),
        compiler_params=_cparams("parallel"),
        name="final_rmsnorm",
    )(x, g.reshape(1, d))


def _mm_kernel(x_ref, w_ref, *rest, epilogue):
    extras, o_ref = rest[:-1], rest[-1]
    acc = jnp.dot(x_ref[...].astype(BF16), w_ref[...].astype(BF16), preferred_element_type=F32)
    if epilogue is not None:
        acc = epilogue(acc, *[e[...] for e in extras])
    o_ref[...] = acc.astype(o_ref.dtype)


def matmul(x, w, out_dtype, tm=1024, tn=512, epilogue=None, extras=(), extra_specs=(), name="matmul",
           xl=None, wl=None, m=None):
    k = x.shape[-1]
    m = x.shape[-2] if m is None else m
    n = w.shape[-1]
    tm, tn = min(tm, m), min(tn, n)
    assert m % tm == 0 and n % tn == 0, (m, tm, n, tn)
    xspec = (pl.BlockSpec((tm, k), lambda i, j: (i, 0)) if xl is None
             else pl.BlockSpec((None, tm, k), lambda i, j: (xl, i, 0)))
    wspec = (pl.BlockSpec((k, tn), lambda i, j: (0, j)) if wl is None
             else pl.BlockSpec((None, k, tn), lambda i, j: (wl, 0, j)))
    return pl.pallas_call(
        functools.partial(_mm_kernel, epilogue=epilogue),
        out_shape=jax.ShapeDtypeStruct((m, n), out_dtype),
        grid=(m // tm, n // tn),
        in_specs=[xspec, wspec] + list(extra_specs),
        out_specs=pl.BlockSpec((tm, tn), lambda i, j: (i, j)),
        compiler_params=_cparams("parallel", "arbitrary"),
        name=name,
    )(x, w, *extras)


def _rope_half(x, cos, sa, sb):
    return x * cos + pltpu.roll(x, 96, 1) * sa + pltpu.roll(x, 32, 1) * sb


def _ep_qrope(acc, cos, sa, sb):
    parts = []
    for h in range(acc.shape[1] // QK_PAD):
        parts.append(acc[:, h * QK_PAD:h * QK_PAD + NOPE_DIM])
        parts.append(_rope_half(acc[:, h * QK_PAD + NOPE_DIM:(h + 1) * QK_PAD], cos, sa, sb))
    return jnp.concatenate(parts, axis=1)


def _proj_ab_kernel(oa_ref, ob_ref, wa_ref, wb_ref, ga_ref, gb_ref, o_ref):
    ya = jnp.dot(oa_ref[...], wa_ref[...], preferred_element_type=F32)
    yb = jnp.dot(ob_ref[...], wb_ref[...], preferred_element_type=F32)
    o_ref[...] = (jax.nn.sigmoid(ga_ref[...].astype(F32)) * ya
                  + jax.nn.sigmoid(gb_ref[...].astype(F32)) * yb).astype(o_ref.dtype)


def proj_ab(o_a, o_b, w_a, w_b, l, big, gate_col0, tm=1024, tn=1024):
    m, k = o_a.shape
    n = w_a.shape[-1]
    tm = min(tm, m)
    goff = gate_col0 // tn
    xspec = pl.BlockSpec((tm, k), lambda i, j: (i, 0))
    wspec = pl.BlockSpec((None, k, tn), lambda i, j: (l, 0, j))
    return pl.pallas_call(
        _proj_ab_kernel,
        out_shape=jax.ShapeDtypeStruct((m, n), BF16),
        grid=(m // tm, n // tn),
        in_specs=[xspec, xspec, wspec, wspec,
                  pl.BlockSpec((tm, tn), lambda i, j: (i, goff + j)),
                  pl.BlockSpec((tm, tn), lambda i, j: (i, goff + n // tn + j))],
        out_specs=pl.BlockSpec((tm, tn), lambda i, j: (i, j)),
        compiler_params=_cparams("parallel", "arbitrary"),
        name="proj_ab",
    )(o_a, o_b, w_a, w_b, big, big)


def _ep_resid(acc, x, gate):
    tm, tn = acc.shape
    y = acc.reshape(tm // GROUP, GROUP, tn) * gate
    return x + y.reshape(tm, tn)


def _inproj1_kernel(h_ref, w_ref, gq_ref, gkv_ref, cos_ref, sa_ref, sb_ref,
                    cq_ref, ckv_ref, kr_ref, kvin_ref):
    acc = jnp.dot(h_ref[...], w_ref[...], preferred_element_type=F32)
    cq = acc[:, :Q_LORA]
    ckv = acc[:, Q_LORA:Q_LORA + KV_LORA]
    kr = acc[:, Q_LORA + KV_LORA:]
    cqn = cq * lax.rsqrt(jnp.mean(cq * cq, axis=-1, keepdims=True) + EPS) * gq_ref[...]
    ckvn = ckv * lax.rsqrt(jnp.mean(ckv * ckv, axis=-1, keepdims=True) + EPS) * gkv_ref[...]
    krr = _rope_half(kr, cos_ref[...], sa_ref[...], sb_ref[...])
    cq_ref[...] = cqn.astype(BF16)
    ckv_ref[...] = ckvn
    kr_ref[...] = krr
    kvin_ref[:, :KV_LORA] = ckvn.astype(BF16)
    kvin_ref[:, KV_LORA:] = krr.astype(BF16)


def inproj1(h, w1, l, g_q, g_kv, cos, sa, sb, tm=512):
    n, k = h.shape
    tm = min(tm, n)
    w1n = w1.shape[-1]
    row = lambda i: (i, 0)
    fix = lambda i: (0, 0)
    return pl.pallas_call(
        _inproj1_kernel,
        out_shape=(jax.ShapeDtypeStruct((n, Q_LORA), BF16),
                   jax.ShapeDtypeStruct((n, KV_LORA), F32),
                   jax.ShapeDtypeStruct((n, LANES), F32),
                   jax.ShapeDtypeStruct((n, KV_IN), BF16)),
        grid=(n // tm,),
        in_specs=[pl.BlockSpec((tm, k), row), pl.BlockSpec((None, k, w1n), lambda i: (l, 0, 0)),
                  pl.BlockSpec((1, Q_LORA), fix), pl.BlockSpec((1, KV_LORA), fix),
                  pl.BlockSpec((tm, LANES), row), pl.BlockSpec((tm, LANES), row), pl.BlockSpec((tm, LANES), row)],
        out_specs=(pl.BlockSpec((tm, Q_LORA), row), pl.BlockSpec((tm, KV_LORA), row),
                   pl.BlockSpec((tm, LANES), row), pl.BlockSpec((tm, KV_IN), row)),
        compiler_params=_cparams("parallel"),
        name="inproj1",
    )(h, w1, g_q.reshape(1, -1), g_kv.reshape(1, -1), cos, sa, sb)


def _attn_prompt_kernel(q_ref, k_ref, v_ref, o_ref, *, tq, tk, nh):
    qi = pl.program_id(2)
    nt = (((1,), (1,)), ((), ()))
    qs = [q_ref[:, h * QK_PAD:(h + 1) * QK_PAD] for h in range(nh)]

    def scores(start, size, h):
        k = k_ref[pl.ds(start, size), h * QK_PAD:(h + 1) * QK_PAD]
        v = v_ref[pl.ds(start, size), h * V_DIM:(h + 1) * V_DIM]
        return lax.dot_general(qs[h], k, nt, preferred_element_type=F32), v

    def update(carry, s, v):
        m, l, acc = carry
        m_new = jnp.maximum(m, jnp.max(s, axis=-1, keepdims=True))
        alpha = jnp.exp2(m - m_new)
        p = jnp.exp2(s - m_new)
        l = alpha * l + jnp.sum(p, axis=-1, keepdims=True)
        acc = alpha * acc + jnp.dot(p.astype(BF16), v, preferred_element_type=F32)
        return m_new, l, acc

    def big_body(j, carry):
        start = pl.multiple_of(j * tk, tk)
        return tuple(update(carry[h], *scores(start, tk, h)) for h in range(nh))

    q0 = qi * tq
    nbig = q0 // tk

    def small_body(i, carry):
        start = pl.multiple_of(nbig * tk + i * tq, tq)
        return tuple(update(carry[h], *scores(start, tq, h)) for h in range(nh))

    init = tuple((jnp.full((tq, 1), -1e30, F32), jnp.zeros((tq, 1), F32), jnp.zeros((tq, V_DIM), F32))
                 for _ in range(nh))
    carry = lax.fori_loop(0, nbig, big_body, init)
    carry = lax.fori_loop(0, (q0 - nbig * tk) // tq, small_body, carry)
    rows = lax.broadcasted_iota(jnp.int32, (tq, tq), 0) // CHUNK
    cols = lax.broadcasted_iota(jnp.int32, (tq, tq), 1) // CHUNK
    for h in range(nh):
        s, v = scores(pl.multiple_of(q0, tq), tq, h)
        s = jnp.where(cols <= rows, s, -1e30)
        m, l, acc = update(carry[h], s, v)
        o_ref[:, h * V_DIM:(h + 1) * V_DIM] = (acc / l).astype(o_ref.dtype)


def attn_prompt(q_cat, kv, bp, s_len, tq=512, tk=1024, nh=4):
    tq = min(tq, s_len)
    tk = min(tk, s_len)
    nq = s_len // tq
    vcol0 = MLA_HEADS * QK_PAD // (V_DIM * nh)
    return pl.pallas_call(
        functools.partial(_attn_prompt_kernel, tq=tq, tk=tk, nh=nh),
        out_shape=jax.ShapeDtypeStruct((q_cat.shape[0], MLA_HEADS * V_DIM), BF16),
        grid=(bp, MLA_HEADS // nh, nq),
        in_specs=[pl.BlockSpec((tq, QK_PAD * nh), lambda b, h, i: (b * nq + i, h)),
                  pl.BlockSpec((s_len, QK_PAD * nh), lambda b, h, i: (b, h)),
                  pl.BlockSpec((s_len, V_DIM * nh), lambda b, h, i: (b, vcol0 + h))],
        out_specs=pl.BlockSpec((tq, V_DIM * nh), lambda b, h, i: (b * nq + i, h)),
        compiler_params=_cparams("parallel", "parallel", "arbitrary"),
        name="attn_prompt",
    )(q_cat, kv, kv)


def _attn_sample_kernel(q_ref, ckv_ref, kr_ref, new_ref, wuk_ref, wuv_ref, prev_ref, o_ref,
                        qall_ref, kall_ref, olat_ref, *, t):
    del prev_ref
    nt = (((1,), (1,)), ((), ()))
    past = ckv_ref.shape[0]
    for h in range(MLA_HEADS):
        qn = q_ref[:, h * QK_PAD:h * QK_PAD + NOPE_DIM]
        qlat = jnp.dot(qn, wuk_ref[h], preferred_element_type=F32)
        qall_ref[h * t:(h + 1) * t, :KV_LORA] = qlat.astype(BF16)
        qall_ref[h * t:(h + 1) * t, KV_LORA:] = q_ref[:, h * QK_PAD + NOPE_DIM:(h + 1) * QK_PAD]
    kall_ref[:, :KV_LORA] = ckv_ref[...].astype(BF16)
    kall_ref[:, KV_LORA:KV_LORA + ROPE_DIM] = kr_ref[...].astype(BF16)
    kall_ref[:, KV_LORA + ROPE_DIM:] = jnp.zeros((past, KV_IN - KV_LORA - ROPE_DIM), BF16)
    knew = new_ref[...]
    half = MLA_HEADS * t // 2
    for r0 in (0, half):
        qall = qall_ref[r0:r0 + half, :]
        s_past = lax.dot_general(qall, kall_ref[...], nt, preferred_element_type=F32)
        s_new = lax.dot_general(qall, knew, nt, preferred_element_type=F32)
        m = jnp.maximum(jnp.max(s_past, axis=-1, keepdims=True), jnp.max(s_new, axis=-1, keepdims=True))
        p_past = jnp.exp2(s_past - m)
        p_new = jnp.exp2(s_new - m)
        l = jnp.sum(p_past, axis=-1, keepdims=True) + jnp.sum(p_new, axis=-1, keepdims=True)
        olat = (jnp.dot(p_past.astype(BF16), kall_ref[:, :KV_LORA], preferred_element_type=F32)
                + jnp.dot(p_new.astype(BF16), knew[:, :KV_LORA], preferred_element_type=F32))
        olat_ref[r0:r0 + half, :] = (olat / l).astype(BF16)
    for h in range(MLA_HEADS):
        o = jnp.dot(olat_ref[h * t:(h + 1) * t, :], wuv_ref[h], preferred_element_type=F32)
        o_ref[:, h * V_DIM:(h + 1) * V_DIM] = o.astype(o_ref.dtype)


def attn_sample(q_cat, ckv_past, kr_past, kvin, wuk_t, wuv, l, prev, row0, t):
    _, bs, past, _ = ckv_past.shape
    rb = row0 // t
    return pl.pallas_call(
        functools.partial(_attn_sample_kernel, t=t),
        out_shape=jax.ShapeDtypeStruct(prev.shape, prev.dtype),
        grid=(bs,),
        in_specs=[pl.BlockSpec((t, MLA_HEADS * QK_PAD), lambda b: (rb + b, 0)),
                  pl.BlockSpec((None, None, past, KV_LORA), lambda b: (l, b, 0, 0)),
                  pl.BlockSpec((None, None, past, ROPE_DIM), lambda b: (l, b, 0, 0)),
                  pl.BlockSpec((t, KV_IN), lambda b: (rb + b, 0)),
                  pl.BlockSpec((None, MLA_HEADS, NOPE_DIM, KV_LORA), lambda b: (l, 0, 0, 0)),
                  pl.BlockSpec((None, MLA_HEADS, KV_LORA, V_DIM), lambda b: (l, 0, 0, 0)),
                  pl.BlockSpec(memory_space=pl.ANY)],
        out_specs=pl.BlockSpec((t, MLA_HEADS * V_DIM), lambda b: (rb + b, 0)),
        scratch_shapes=[pltpu.VMEM((MLA_HEADS * t, KV_IN), BF16),
                        pltpu.VMEM((past, KV_IN), BF16),
                        pltpu.VMEM((MLA_HEADS * t, KV_LORA), BF16)],
        input_output_aliases={6: 0},
        compiler_params=_cparams("parallel"),
        name="attn_sample",
    )(q_cat, ckv_past, kr_past, kvin, wuk_t, wuv, prev)


def _ret_consts():
    lg = np.log1p(-(2.0 ** (-5.0 - np.arange(RET_HEADS, dtype=np.float64))))
    idx = np.arange(CHUNK, dtype=np.float64)
    diff = idx[:, None] - idx[None, :]
    dmask = np.where(diff[None] >= 0, np.exp(np.maximum(diff, 0.0)[None] * lg[:, None, None]), 0.0)
    qd = np.exp((idx + 1.0)[None, :] * lg[:, None])
    kd = np.exp((CHUNK - 1.0 - idx)[None, :] * lg[:, None])
    g = np.exp(CHUNK * lg)
    qd = np.broadcast_to(qd[:, :, None], (RET_HEADS, CHUNK, RET_QK_DIM))
    kd = np.broadcast_to(kd[:, :, None], (RET_HEADS, CHUNK, RET_QK_DIM))
    g = np.broadcast_to(g[:, None, None], (RET_HEADS, 1, RET_V_DIM))
    return (jnp.asarray(dmask, F32), jnp.asarray(qd, F32), jnp.asarray(kd, F32), jnp.asarray(g, F32))


def _retention_kernel(q_ref, k_ref, v_ref, g_ref, cos_ref, sin_ref, st0_ref, dm_ref, qd_ref, kd_ref,
                      gam_ref, *rest, cps, hps):
    o_ref, st_ref, state = rest[-3:]
    step = pl.program_id(2)
    nt = (((1,), (1,)), ((), ()))

    @pl.when(step == 0)
    def _():
        state[...] = st0_ref[...]

    for c in range(cps):
        r = slice(c * CHUNK, (c + 1) * CHUNK)
        cos, sin = cos_ref[r, :], sin_ref[r, :]
        for hh in range(hps):
            qk = slice(hh * RET_QK_DIM, (hh + 1) * RET_QK_DIM)
            vv = slice(hh * RET_V_DIM, (hh + 1) * RET_V_DIM)
            qf = q_ref[r, qk].astype(F32)
            kf = k_ref[r, qk].astype(F32)
            q = qf * cos + pltpu.roll(qf, 64, 1) * sin
            k = (kf * cos + pltpu.roll(kf, 64, 1) * sin) * (RET_QK_DIM ** -0.5)
            v = v_ref[r, vv]
            st = state[hh]
            inner = lax.dot_general(q.astype(BF16), k.astype(BF16), nt, preferred_element_type=F32) * dm_ref[hh]
            o = (jnp.dot(inner.astype(BF16), v, preferred_element_type=F32)
                 + jnp.dot((q * qd_ref[hh]).astype(BF16), st.astype(BF16), preferred_element_type=F32))
            kt = jnp.transpose(k * kd_ref[hh]).astype(BF16)
            state[hh] = st * gam_ref[hh] + jnp.dot(kt, v, preferred_element_type=F32)
            mu = jnp.mean(o, axis=-1, keepdims=True)
            oc = o - mu
            var = jnp.mean(oc * oc, axis=-1, keepdims=True)
            gate = g_ref[r, vv].astype(F32)
            o_ref[r, vv] = (gate * jax.nn.sigmoid(gate) * (oc * lax.rsqrt(var + EPS))).astype(o_ref.dtype)

    @pl.when(step == pl.num_programs(2) - 1)
    def _():
        st_ref[0] = state[...]


def retention(big, cos, sin, state0, l, consts, row0, nseq, seq_len, cps, hps, prev=None):
    r = cps * CHUNK
    steps = seq_len // r
    rb0 = row0 // r
    ng = RET_HEADS // hps
    dmask, qd, kd, gam = consts
    rowblk = lambda s, h, t: rb0 + s * steps + t
    hconst = lambda s, h, t: (h, 0, 0)
    in_specs = [pl.BlockSpec((r, hps * RET_QK_DIM), lambda s, h, t: (rowblk(s, h, t), h)),
                pl.BlockSpec((r, hps * RET_QK_DIM), lambda s, h, t: (rowblk(s, h, t), ng + h)),
                pl.BlockSpec((r, hps * RET_V_DIM), lambda s, h, t: (rowblk(s, h, t), ng + h)),
                pl.BlockSpec((r, hps * RET_V_DIM), lambda s, h, t: (rowblk(s, h, t), 2 * ng + h)),
                pl.BlockSpec((r, RET_QK_DIM), lambda s, h, t: (rowblk(s, h, t), 0)),
                pl.BlockSpec((r, RET_QK_DIM), lambda s, h, t: (rowblk(s, h, t), 0)),
                pl.BlockSpec((None, None, hps, RET_QK_DIM, RET_V_DIM), lambda s, h, t: (l, s, h, 0, 0)),
                pl.BlockSpec((hps, CHUNK, CHUNK), hconst),
                pl.BlockSpec((hps, CHUNK, RET_QK_DIM), hconst),
                pl.BlockSpec((hps, CHUNK, RET_QK_DIM), hconst),
                pl.BlockSpec((hps, 1, RET_V_DIM), hconst)]
    args = [big, big, big, big, cos, sin, state0, dmask, qd, kd, gam]
    aliases = {}
    if prev is not None:
        in_specs.append(pl.BlockSpec(memory_space=pl.ANY))
        args.append(prev)
        aliases = {len(args) - 1: 0}
    return pl.pallas_call(
        functools.partial(_retention_kernel, cps=cps, hps=hps),
        out_shape=(jax.ShapeDtypeStruct((big.shape[0], RET_HEADS * RET_V_DIM), BF16),
                   jax.ShapeDtypeStruct((nseq, RET_HEADS, RET_QK_DIM, RET_V_DIM), F32)),
        grid=(nseq, ng, steps),
        in_specs=in_specs,
        out_specs=(pl.BlockSpec((r, hps * RET_V_DIM), lambda s, h, t: (rowblk(s, h, t), h)),
                   pl.BlockSpec((1, hps, RET_QK_DIM, RET_V_DIM), lambda s, h, t: (s, h, 0, 0))),
        scratch_shapes=[pltpu.VMEM((hps, RET_QK_DIM, RET_V_DIM), F32)],
        input_output_aliases=aliases,
        compiler_params=_cparams("parallel", "parallel", "arbitrary"),
        name="retention",
    )(*args)


_CAND = [(a, b) for a in range(PEER_TOPK) for b in range(PEER_TOPK) if (a + 1) * (b + 1) <= PEER_TOPK]


def _top_distinct(s):
    vals, cnts = [], []
    for _ in range(PEER_TOPK):
        m = jnp.max(s, axis=0, keepdims=True)
        hit = s == m
        vals.append(m)
        cnts.append(jnp.sum(jnp.where(hit, 1.0, 0.0), axis=0, keepdims=True))
        s = jnp.where(hit, -jnp.inf, s)
    return vals, cnts


def _route_kernel(qt_ref, k1_ref, k2_ref, thr_ref, e1_ref, s2_ref, e2_ref, cand_ref, mult_ref):
    tt = qt_ref.shape[1]
    pad = cand_ref.shape[0] - len(_CAND)
    cand_ref[len(_CAND):, :] = jnp.full((pad, LANES), -jnp.inf, F32)
    mult_ref[len(_CAND):, :] = jnp.zeros((pad, LANES), F32)
    for h, c in [(h, c) for h in range(PEER_HEADS) for c in range(tt // LANES)]:
        lanes = slice(c * LANES, (c + 1) * LANES)
        q1 = qt_ref[h * 2 * PEER_HALF:h * 2 * PEER_HALF + PEER_HALF, lanes].astype(BF16)
        q2 = qt_ref[h * 2 * PEER_HALF + PEER_HALF:(h + 1) * 2 * PEER_HALF, lanes].astype(BF16)
        s1 = jnp.dot(k1_ref[...], q1, preferred_element_type=F32)
        s2 = jnp.dot(k2_ref[...], q2, preferred_element_type=F32)
        v1, c1 = _top_distinct(s1)
        v2, c2 = _top_distinct(s2)
        for r, (a, b) in enumerate(_CAND):
            cand_ref[r:r + 1, :] = v1[a] + v2[b]
            mult_ref[r:r + 1, :] = c1[a] * c2[b]
        cand = cand_ref[...]
        mult = mult_ref[...]
        rest = cand
        cum = jnp.zeros_like(v1[0])
        tau = v1[0] + v2[0]
        for _ in range(PEER_TOPK):
            m = jnp.max(rest, axis=0, keepdims=True)
            hit = rest == m
            tau = jnp.where(cum < PEER_TOPK, m, tau)
            cum = cum + jnp.sum(jnp.where(hit, mult, 0.0), axis=0, keepdims=True)
            rest = jnp.where(hit, -jnp.inf, rest)
        top = v1[0] + v2[0]
        z = jnp.sum(jnp.where(cand >= tau, mult * jnp.exp(cand - top), 0.0), axis=0, keepdims=True)
        thr = jnp.full_like(s1, jnp.inf)
        for a in range(PEER_TOPK):
            th_a = jnp.full_like(tau, jnp.inf)
            for b in range(PEER_TOPK // (a + 1)):
                th_a = jnp.where(v1[a] + v2[b] >= tau, v2[b], th_a)
            thr = jnp.where(s1 == v1[a], th_a, thr)
        thr_ref[h, :, lanes] = thr
        s2_ref[h, :, lanes] = s2
        e1_ref[h, :, lanes] = 0.5 * jnp.exp(s1 - v1[0])
        e2_ref[h, :, lanes] = jnp.exp(s2 - v2[0]) / z


def peer_route(qt, k1, k2, l, tt=256):
    n = qt.shape[1]
    tt = min(tt, n)
    big = jax.ShapeDtypeStruct((PEER_HEADS, N_KEYS, n), F32)
    bspec = pl.BlockSpec((PEER_HEADS, N_KEYS, tt), lambda i: (0, 0, i))
    return pl.pallas_call(
        _route_kernel,
        out_shape=(big, big, big, big),
        grid=(n // tt,),
        in_specs=[pl.BlockSpec((PEER_HEADS * 2 * PEER_HALF, tt), lambda i: (0, i)),
                  pl.BlockSpec((None, N_KEYS, PEER_HALF), lambda i: (l, 0, 0)),
                  pl.BlockSpec((None, N_KEYS, PEER_HALF), lambda i: (l, 0, 0))],
        out_specs=(bspec, bspec, bspec, bspec),
        scratch_shapes=[pltpu.VMEM((-(-len(_CAND) // SUBLANES) * SUBLANES, LANES), F32)] * 2,
        compiler_params=_cparams("parallel"),
        name="peer_route",
    )(qt, k1, k2)


def _gate_rows(ii, thr_ref, e1_ref, s2_ref, e2_ref):
    w = None
    for h in range(PEER_HEADS):
        c = jnp.where(s2_ref[h] >= thr_ref[h, ii:ii + 1, :], e2_ref[h], 0.0) * e1_ref[h, ii:ii + 1, :]
        w = c if w is None else w + c
    return w


def _peer_expert_kernel(ht_ref, u_ref, vt_ref, thc_ref, e1c_ref, thn_ref, e1n_ref, s2_ref, e2_ref,
                        x_ref, gate_ref, o_ref, acc_ref, wg_ref, w_ref, *, ipt):
    e = pl.program_id(1)
    last = pl.num_programs(1) - 1

    def build(thr_ref, e1_ref, slot):
        for ii in range(ipt):
            w_ref[slot, ii * N_KEYS:(ii + 1) * N_KEYS, :] = _gate_rows(ii, thr_ref, e1_ref, s2_ref, e2_ref)

    @pl.when(e == 0)
    def _():
        acc_ref[...] = jnp.zeros_like(acc_ref)
        build(thc_ref, e1c_ref, 0)

    @pl.when(e < last)
    def _():
        build(thn_ref, e1n_ref, (e + 1) % 2)

    act = jnp.dot(u_ref[...], ht_ref[...], preferred_element_type=F32)
    gelu2 = act * (1.0 + lax.erf(act * (2.0 ** -0.5)))
    wg_ref[...] = (w_ref[e % 2] * gelu2).astype(BF16)
    acc_ref[...] += jnp.dot(vt_ref[...], wg_ref[...], preferred_element_type=F32)

    @pl.when(e == last)
    def _():
        tt, d = o_ref.shape
        y = jnp.transpose(acc_ref[...]).reshape(tt // GROUP, GROUP, d) * gate_ref[...]
        o_ref[...] = x_ref[...] + y.reshape(tt, d)


def peer_experts(ht, u, vt, l, thr, e1, s2, e2, x, gate_g, tt=512, ipt=8):
    d, n = ht.shape
    ne = u.shape[1]
    tt = min(tt, n)
    te = ipt * N_KEYS
    nlast = ne // te - 1
    full = pl.BlockSpec((PEER_HEADS, N_KEYS, tt), lambda t, e: (0, 0, t))
    cur = pl.BlockSpec((PEER_HEADS, ipt, tt), lambda t, e: (0, e, t))
    nxt = pl.BlockSpec((PEER_HEADS, ipt, tt), lambda t, e: (0, jnp.minimum(e + 1, nlast), t))
    return pl.pallas_call(
        functools.partial(_peer_expert_kernel, ipt=ipt),
        out_shape=jax.ShapeDtypeStruct((n, d), F32),
        grid=(n // tt, ne // te),
        in_specs=[pl.BlockSpec((d, tt), lambda t, e: (0, t)),
                  pl.BlockSpec((None, te, d), lambda t, e: (l, e, 0)),
                  pl.BlockSpec((None, d, te), lambda t, e: (l, 0, e)),
                  cur, cur, nxt, nxt, full, full,
                  pl.BlockSpec((tt, d), lambda t, e: (t, 0)),
                  pl.BlockSpec((tt // GROUP, 1, d), lambda t, e: (t, 0, 0))],
        out_specs=pl.BlockSpec((tt, d), lambda t, e: (t, 0)),
        scratch_shapes=[pltpu.VMEM((d, tt), F32), pltpu.VMEM((te, tt), BF16), pltpu.VMEM((2, te, tt), F32)],
        compiler_params=_cparams("parallel", "arbitrary"),
        name="peer_experts",
    )(ht, u, vt, thr, e1, thr, e1, s2, e2, x, gate_g)


def _rope_tables(pos):
    pos = pos.astype(F32)[:, None]
    half = ROPE_DIM // 2
    inv = ROPE_THETA ** (-jnp.arange(half, dtype=F32) * 2.0 / ROPE_DIM)
    ang = pos * inv[None, :]
    c, s = jnp.cos(ang), jnp.sin(ang)
    z32 = jnp.zeros_like(c)
    z64 = jnp.concatenate([z32, z32], axis=1)
    cos64 = jnp.concatenate([c, c, z64], axis=1)
    sa64 = jnp.concatenate([-s, z32, z64], axis=1)
    sb64 = jnp.concatenate([z32, s, z64], axis=1)
    half = RET_QK_DIM // 2
    inv = ROPE_THETA ** (-jnp.arange(half, dtype=F32) * 2.0 / RET_QK_DIM)
    ang = pos * inv[None, :]
    c, s = jnp.cos(ang), jnp.sin(ang)
    return cos64, sa64, sb64, jnp.concatenate([c, c], axis=1), jnp.concatenate([-s, s], axis=1)


def _prep_weights(w_ada, w_in, w_uq, w_ukv, w_a, w_b, w_o, w_pq, k1, k2, u, v):
    nl = w_in.shape[0]
    o1 = Q_LORA + KV_LORA + ROPE_DIM
    w1 = jnp.pad(w_in[:, :, :o1], ((0, 0), (0, 0), (0, LANES - ROPE_DIM))).astype(BF16)
    w2 = w_in[:, :, o1:].astype(BF16)
    uq = w_uq.reshape(nl, Q_LORA, MLA_HEADS, NOPE_DIM + ROPE_DIM) * (MLA_SCALE * LOG2E)
    uq = jnp.pad(uq, ((0, 0), (0, 0), (0, 0), (0, QK_PAD - NOPE_DIM - ROPE_DIM)))
    uq = uq.reshape(nl, Q_LORA, MLA_HEADS * QK_PAD).astype(BF16)
    wk = jnp.pad(w_ukv[..., :NOPE_DIM], ((0, 0), (0, KV_IN - KV_LORA), (0, 0), (0, QK_PAD - NOPE_DIM)))
    eye = jnp.zeros((KV_IN, QK_PAD), F32).at[KV_LORA + jnp.arange(ROPE_DIM), NOPE_DIM + jnp.arange(ROPE_DIM)].set(1.0)
    wk = (wk + eye[None, :, None, :]).reshape(nl, KV_IN, MLA_HEADS * QK_PAD)
    wv = jnp.pad(w_ukv[..., NOPE_DIM:], ((0, 0), (0, KV_IN - KV_LORA), (0, 0), (0, 0)))
    wv = wv.reshape(nl, KV_IN, MLA_HEADS * V_DIM)
    wkv = jnp.concatenate([wk, wv], axis=2).astype(BF16)
    wuk_t = jnp.transpose(w_ukv[..., :NOPE_DIM], (0, 2, 3, 1)).astype(BF16)
    wuv = jnp.transpose(w_ukv[..., NOPE_DIM:], (0, 2, 1, 3)).astype(BF16)
    return dict(w_ada=w_ada, w1=w1, w2=w2, uq=uq, wkv=wkv, wuk_t=wuk_t, wuv=wuv,
                w_a=w_a.astype(BF16), w_b=w_b.astype(BF16), w_o=w_o.astype(BF16),
                w_pq_t=jnp.swapaxes(w_pq, 1, 2).astype(BF16), k1=k1.astype(BF16), k2=k2.astype(BF16),
                u=u.astype(BF16), v_t=jnp.swapaxes(v, 1, 2).astype(BF16))


def kernel(x_prompt, x_sample, c_prompt, c_sample, cache_ckv, cache_krope, state_ret, w_ada, b_ada, g_mix, g_ffn, w_in, g_q, w_uq, g_kv, w_ukv, w_a, w_b, w_o, w_pq, peer_k1, peer_k2, peer_u, peer_v, g_final):
    bp, s_len, d = x_prompt.shape
    bs, t_s, _ = x_sample.shape
    depth = w_in.shape[0]
    past = cache_ckv.shape[2]
    n_p, n_s = bp * s_len, bs * t_s
    n = n_p + n_s
    assert t_s == GROUP and s_len % GROUP == 0

    x = jnp.concatenate([x_prompt.reshape(n_p, d), x_sample.reshape(n_s, d)], axis=0)
    c_all = jnp.concatenate([c_prompt, c_sample], axis=0)
    gidx = jnp.concatenate([jnp.repeat(jnp.arange(bp), s_len // GROUP), bp + jnp.arange(bs)])
    pos = jnp.concatenate([jnp.tile(jnp.arange(s_len), bp), jnp.tile(past + jnp.arange(t_s), bs)])
    cos64, sa64, sb64, cos128, sin128 = _rope_tables(pos)
    ret_c = _ret_consts()
    w = _prep_weights(w_ada, w_in, w_uq, w_ukv, w_a, w_b, w_o, w_pq, peer_k1, peer_k2, peer_u, peer_v)
    zero_state = jnp.zeros((1, bp, RET_HEADS, RET_QK_DIM, RET_V_DIM), F32)
    silu_c = jax.nn.silu(c_all)
    gspec = lambda tm, tn: pl.BlockSpec((tm // GROUP, 1, tn), lambda i, j: (i, 0, j))
    tile = lambda tm, tn, off=0: pl.BlockSpec((tm, tn), lambda i, j: (i, j + off))
    rope_spec = lambda tm: pl.BlockSpec((tm, LANES), lambda i, j: (i, 0))

    outs = {k: [] for k in ("ckv_p", "kr_p", "st_p", "ckv_s", "kr_s", "st_s")}
    for l in range(depth):
        ada = matmul(silu_c, w["w_ada"], F32, wl=l, name="ada") + b_ada[l][None, :]
        cond = ada.reshape(bp + bs, 6, d)[gidx]
        cg = [cond[:, k, :][:, None, :] for k in range(6)]

        h = modnorm(x, g_mix[l], cg[1], cg[0])
        cqn, ckvn, krr, kvin = inproj1(h, w["w1"], l, g_q[l], g_kv[l], cos64, sa64, sb64)
        big = matmul(h, w["w2"], BF16, tm=1024, tn=2048, wl=l, name="inproj2")
        tmq = 1024
        q_cat = matmul(cqn, w["uq"], BF16, tm=tmq, tn=2048, epilogue=_ep_qrope, wl=l,
                       extras=(cos64, sa64, sb64), extra_specs=[rope_spec(tmq)] * 3, name="q_up")
        kv = matmul(kvin, w["wkv"], BF16, tm=1024, tn=2048, wl=l, m=n_p, name="kv_up")
        o_a = attn_prompt(q_cat, kv, bp, s_len)
        o_a = attn_sample(q_cat, cache_ckv, cache_krope, kvin, w["wuk_t"], w["wuv"], l, o_a, n_p, t_s)
        o_b, st_p = retention(big, cos128, sin128, zero_state, 0, ret_c, 0, bp, s_len, min(8, s_len // CHUNK), 4)
        o_b, st_s = retention(big, cos128, sin128, state_ret, l, ret_c, n_p, bs, t_s, 1, RET_HEADS, prev=o_b)
        mixed = proj_ab(o_a, o_b, w["w_a"], w["w_b"], l, big, 2 * RET_HEADS * RET_QK_DIM + 2 * RET_HEADS * RET_V_DIM)
        tm, tn = 1024, 1024
        x = matmul(mixed, w["w_o"], F32, tm=tm, tn=tn, epilogue=_ep_resid, wl=l,
                   extras=(x, cg[2]), extra_specs=[tile(tm, tn), gspec(tm, tn)], name="proj_o")

        outs["ckv_p"].append(ckvn[:n_p].reshape(bp, s_len, KV_LORA))
        outs["kr_p"].append(krr[:n_p, :ROPE_DIM].reshape(bp, s_len, ROPE_DIM))
        outs["st_p"].append(st_p)
        outs["ckv_s"].append(ckvn[n_p:].reshape(bs, t_s, KV_LORA))
        outs["kr_s"].append(krr[n_p:, :ROPE_DIM].reshape(bs, t_s, ROPE_DIM))
        outs["st_s"].append(st_s)

        h2t = modnorm(x, g_ffn[l], cg[4], cg[3], transposed=True)
        qt = matmul(w["w_pq_t"], h2t, F32, tm=2048, tn=1024, xl=l, name="peer_q")
        thr, e1, s2, e2 = peer_route(qt, w["k1"], w["k2"], l)
        x = peer_experts(h2t, w["u"], w["v_t"], l, thr, e1, s2, e2, x, cg[5])

    y_p = final_rmsnorm(x, g_final, 0, n_p)
    y_s = final_rmsnorm(x, g_final, n_p, n_s)
    st = lambda k: jnp.stack(outs[k], axis=0)
    return (y_p.reshape(bp, s_len, d), y_s.reshape(bs, t_s, d),
            st("ckv_p"), st("kr_p"), st("st_p"), st("ckv_s"), st("kr_s"), st("st_s"))
```
